```python
import math
import jax, jax.numpy as jnp
from jax import lax
import numpy as np

D_MODEL = 1024
BATCH = 4
SEQ = 4096
DEPTH = 4

HEAD_DIM = 64
H_A = 8
H_B = 8
G_B = 2
R_B = H_B // G_B
H_TOT = H_A + H_B
D_A = H_A * HEAD_DIM
D_B = H_B * HEAD_DIM
D_MIX = D_A + D_B
D_KV_B = G_B * HEAD_DIM
DILATED = ((128, 1), (512, 4), (2048, 16))
CMP_LEN = 32
CMP_STRIDE = 16
CMP_HIDDEN = 256
SLC_BLOCK = 64
SLC_TOPK = 16
SLC_LOCAL = 2
WIN = 512
N_BUCKETS = 32
MAX_DIST = 2048
BLOCK_Q = 128
LN_EPS = 1e-5
NEG = -1e30
ALPHA = (2 * DEPTH) ** 0.25
BETA = (8 * DEPTH) ** -0.25
IN_SIZES = (D_A, D_A, D_A, D_A, D_B, 6 * D_KV_B, 3 * H_B, D_B)
IN_OFFSETS = tuple(int(v) for v in np.cumsum(IN_SIZES)[:-1])
D_IN = sum(IN_SIZES)

kernel_name = 'hymba_dilated_nsa_deepnorm_adaln'


def t5_bucket(dist):
    max_exact = N_BUCKETS // 2
    d = jnp.maximum(dist, 0)
    large = max_exact + (jnp.log(jnp.maximum(d, 1).astype(jnp.float32) / max_exact)
                         / math.log(MAX_DIST / max_exact) * (N_BUCKETS - max_exact)).astype(jnp.int32)
    return jnp.where(d < max_exact, d, jnp.minimum(large, N_BUCKETS - 1))


def layer_norm(x, g, b):
    xf = x.astype(jnp.float32)
    mu = xf.mean(-1, keepdims=True)
    var = jnp.square(xf - mu).mean(-1, keepdims=True)
    return ((xf - mu) * lax.rsqrt(var + LN_EPS)).astype(x.dtype) * g + b


def banded_attention(q, k, v, bias_tbl, window, dist_scale):
    B, G, R, L, hd = q.shape
    n_prev = -(-window // BLOCK_Q)
    L_pad = -(-L // BLOCK_Q) * BLOCK_Q
    nb = L_pad // BLOCK_Q
    pad_end = L_pad - L
    q = jnp.pad(q, ((0, 0), (0, 0), (0, 0), (0, pad_end), (0, 0)))
    kp = jnp.pad(k, ((0, 0), (0, 0), (n_prev * BLOCK_Q, pad_end), (0, 0)))
    vp = jnp.pad(v, ((0, 0), (0, 0), (n_prev * BLOCK_Q, pad_end), (0, 0)))

    def band(t):
        parts = [t[:, :, o * BLOCK_Q:o * BLOCK_Q + L_pad].reshape(B, G, nb, BLOCK_Q, hd)
                 for o in range(n_prev + 1)]
        return jnp.concatenate(parts, axis=3)

    kb, vb = band(kp), band(vp)
    qb = q.reshape(B, G, R, nb, BLOCK_Q, hd)
    logits = jnp.einsum('bgrnqd,bgnkd->bgrnqk', qb, kb).astype(jnp.float32)
    kj = jnp.arange((n_prev + 1) * BLOCK_Q)[None, :]
    dist = jnp.arange(BLOCK_Q)[:, None] + n_prev * BLOCK_Q - kj
    key_pos = jnp.arange(nb)[:, None, None] * BLOCK_Q - n_prev * BLOCK_Q + kj[None]
    mask = (dist >= 0) & (dist <= window) & (key_pos >= 0)
    bias = bias_tbl[:, :, t5_bucket(dist * dist_scale)]
    logits = jnp.where(mask, logits + bias[:, :, None], NEG)
    lse = jax.nn.logsumexp(logits, axis=-1)
    p = jnp.exp(logits - lse[..., None]).astype(v.dtype)
    out = jnp.einsum('bgrnqk,bgnkd->bgrnqd', p, vb).reshape(B, G, R, L_pad, hd)[:, :, :, :L]
    return out, lse.reshape(B, G, R, L_pad)[..., :L]


def dilated_attention(q, k, v, bias_tbl):
    B, H, S, hd = q.shape
    outs, lses = [], []
    for window, dil in DILATED:
        def split(t):
            return t.reshape(B, H, S // dil, dil, hd).transpose(0, 1, 3, 2, 4).reshape(B, H * dil, S // dil, hd)
        tbl = jnp.repeat(bias_tbl, dil, axis=0)[:, None]
        o, lse = banded_attention(split(q)[:, :, None], split(k), split(v), tbl, window // dil, dil)
        outs.append(o[:, :, 0].reshape(B, H, dil, S // dil, hd).transpose(0, 1, 3, 2, 4).reshape(B, H, S, hd))
        lses.append(lse[:, :, 0].reshape(B, H, dil, S // dil).transpose(0, 1, 3, 2).reshape(B, H, S))
    w = jax.nn.softmax(jnp.stack(lses), axis=0)
    return jnp.einsum('ibhs,ibhsd->bhsd', w.astype(q.dtype), jnp.stack(outs))


def compress(t, pe, w1, w2):
    B, G, S, hd = t.shape
    n_sub = CMP_LEN // CMP_STRIDE
    chunks = t.reshape(B, G, S // CMP_STRIDE, CMP_STRIDE, hd)
    n_cmp = S // CMP_STRIDE - n_sub + 1
    blocks = jnp.concatenate([chunks[:, :, o:o + n_cmp] for o in range(n_sub)], axis=3)
    blocks = (blocks + pe).reshape(B, G, n_cmp, CMP_LEN * hd)
    return jax.nn.silu(blocks @ w1) @ w2


def nsa_attention(q, kc, vc, k_s, v_s, k_w, v_w, gates, bias_tbl):
    B, G, R, S, hd = q.shape
    n_cmp = kc.shape[2]
    pos = jnp.arange(S)
    blk_end = jnp.arange(n_cmp) * CMP_STRIDE + CMP_LEN - 1
    dist_c = pos[:, None] - blk_end[None]
    mask_c = dist_c >= 0
    logits = jnp.einsum('bgrsd,bgnd->bgrsn', q, kc).astype(jnp.float32) + bias_tbl[:, :, t5_bucket(dist_c)]
    logits = jnp.where(mask_c, logits, NEG)
    p_c = jax.nn.softmax(logits, axis=-1) * mask_c.any(-1)[:, None]
    o_cmp = jnp.einsum('bgrsn,bgnd->bgrsd', p_c.astype(vc.dtype), vc)
    n_slc = S // SLC_BLOCK
    cmp_start = jnp.arange(n_cmp) * CMP_STRIDE
    slc_start = jnp.arange(n_slc) * SLC_BLOCK
    overlap = ((cmp_start[:, None] < slc_start[None] + SLC_BLOCK)
               & (cmp_start[:, None] + CMP_LEN > slc_start[None])).astype(jnp.float32)
    score = jnp.einsum('bgrsn,nm->bgsm', p_c, overlap)
    cur = (pos // SLC_BLOCK)[:, None]
    blk = jnp.arange(n_slc)[None]
    forced = (blk == 0) | ((cur - blk >= 0) & (cur - blk < SLC_LOCAL))
    score = jnp.where(forced, 1e9, jnp.where(blk > cur, -1e9, score))
    k_sel = min(SLC_TOPK, n_slc)
    _, idx = lax.top_k(score, k_sel)
    ks_blocks = k_s.reshape(B, G, n_slc, SLC_BLOCK, hd)
    vs_blocks = v_s.reshape(B, G, n_slc, SLC_BLOCK, hd)
    nq = S // BLOCK_Q
    q_blk = q.reshape(B, G, R, nq, BLOCK_Q, hd).transpose(3, 0, 1, 2, 4, 5)
    idx_blk = idx.reshape(B, G, nq, BLOCK_Q, k_sel).transpose(2, 0, 1, 3, 4)
    gather = jax.vmap(jax.vmap(lambda blocks, ind: blocks[ind]))
    g_idx = jnp.arange(G)[:, None, None, None]
    r_idx = jnp.arange(R)[:, None, None]

    def one_block(args):
        qb, ib, n = args
        kg = gather(ks_blocks, ib).reshape(B, G, BLOCK_Q, k_sel * SLC_BLOCK, hd)
        vg = gather(vs_blocks, ib).reshape(B, G, BLOCK_Q, k_sel * SLC_BLOCK, hd)
        k_pos = (ib[..., None] * SLC_BLOCK + jnp.arange(SLC_BLOCK)).reshape(B, G, BLOCK_Q, k_sel * SLC_BLOCK)
        dist = (n * BLOCK_Q + jnp.arange(BLOCK_Q))[:, None] - k_pos
        bias = bias_tbl[g_idx, r_idx, t5_bucket(dist)[:, :, None]]
        lg = jnp.einsum('bgrqd,bgqkd->bgrqk', qb, kg).astype(jnp.float32)
        lg = jnp.where((dist >= 0)[:, :, None], lg + bias, NEG)
        p = jax.nn.softmax(lg, axis=-1).astype(vg.dtype)
        return jnp.einsum('bgrqk,bgqkd->bgrqd', p, vg)

    o_slc = lax.map(one_block, (q_blk, idx_blk, jnp.arange(nq)))
    o_slc = o_slc.transpose(1, 2, 3, 0, 4, 5).reshape(B, G, R, S, hd)
    o_win, _ = banded_attention(q, k_w, v_w, bias_tbl, WIN - 1, 1)
    gt = gates.reshape(B, S, 3, G, R).transpose(2, 0, 3, 4, 1)[..., None].astype(q.dtype)
    return gt[0] * o_cmp + gt[1] * o_slc + gt[2] * o_win


def hybrid_layer(x, c, w_in, w_out, cmp_pe, cmp_w1, cmp_w2, w_ada, b_ada, ln_g, ln_b, rel_bias):
    B, S, _ = x.shape
    shift, scl, gate = jnp.split(c @ w_ada + b_ada, 3, axis=-1)
    h = x * (1 + scl[:, None]) + shift[:, None]
    qa, ka, va, za, qb, kvb, gb, zb = jnp.split(h @ w_in, IN_OFFSETS, axis=-1)
    heads = lambda t, n: t.reshape(B, S, n, HEAD_DIM).transpose(0, 2, 1, 3)
    qscale = HEAD_DIM ** -0.5
    o_a = dilated_attention(heads(qa, H_A) * qscale, heads(ka, H_A), heads(va, H_A), rel_bias[:, :H_A].T)
    o_a = o_a.transpose(0, 2, 1, 3).reshape(B, S, D_A)
    q_b = (heads(qb, H_B) * qscale).reshape(B, G_B, R_B, S, HEAD_DIM)
    k_c, v_c, k_s, v_s, k_w, v_w = [heads(t, G_B) for t in jnp.split(kvb, 6, axis=-1)]
    kc = compress(k_c, cmp_pe[0], cmp_w1[0], cmp_w2[0])
    vc = compress(v_c, cmp_pe[1], cmp_w1[1], cmp_w2[1])
    tbl_b = rel_bias[:, H_A:].T.reshape(G_B, R_B, N_BUCKETS)
    o_b = nsa_attention(q_b, kc, vc, k_s, v_s, k_w, v_w, jax.nn.sigmoid(gb), tbl_b)
    o_b = o_b.reshape(B, H_B, S, HEAD_DIM).transpose(0, 2, 1, 3).reshape(B, S, D_B)
    y = jnp.concatenate([o_a * jax.nn.silu(za), o_b * jax.nn.silu(zb)], axis=-1) @ w_out
    return layer_norm(ALPHA * x + (1 + gate[:, None]) * y, ln_g, ln_b)


def setup_inputs(seed: int = 0) -> dict:
    key = jax.random.key(seed)
    ks = jax.random.split(key, 12)
    f32 = jnp.float32
    col_scale = jnp.concatenate([
        jnp.ones(2 * D_A, f32), jnp.full((D_A,), BETA, f32), jnp.ones(D_A + D_B, f32),
        jnp.tile(jnp.repeat(jnp.array([1.0, BETA], f32), D_KV_B), 3),
        jnp.ones(3 * H_B + D_B, f32)])
    return {
        'x': jax.random.normal(ks[0], (BATCH, SEQ, D_MODEL), f32),
        'c': jax.random.normal(ks[1], (BATCH, D_MODEL), f32),
        'w_in': jax.random.normal(ks[2], (DEPTH, D_MODEL, D_IN), f32) * D_MODEL ** -0.5 * col_scale,
        'w_out': jax.random.normal(ks[3], (DEPTH, D_MIX, D_MODEL), f32) * D_MIX ** -0.5 * BETA,
        'cmp_pe': 0.02 * jax.random.normal(ks[4], (DEPTH, 2, CMP_LEN, HEAD_DIM), f32),
        'cmp_w1': jax.random.normal(ks[5], (DEPTH, 2, CMP_LEN * HEAD_DIM, CMP_HIDDEN), f32) * (CMP_LEN * HEAD_DIM) ** -0.5,
        'cmp_w2': jax.random.normal(ks[6], (DEPTH, 2, CMP_HIDDEN, HEAD_DIM), f32) * CMP_HIDDEN ** -0.5,
        'w_ada': jax.random.normal(ks[7], (DEPTH, D_MODEL, 3 * D_MODEL), f32) * 0.1 * D_MODEL ** -0.5,
        'b_ada': 0.01 * jax.random.normal(ks[8], (DEPTH, 3 * D_MODEL), f32),
        'ln_g': 1.0 + 0.01 * jax.random.normal(ks[9], (DEPTH, D_MODEL), f32),
        'ln_b': 0.01 * jax.random.normal(ks[10], (DEPTH, D_MODEL), f32),
        'rel_bias': 0.1 * jax.random.normal(ks[11], (N_BUCKETS, H_TOT), f32),
    }


def reference(x, c, w_in, w_out, cmp_pe, cmp_w1, cmp_w2, w_ada, b_ada, ln_g, ln_b, rel_bias):
    for l in range(DEPTH):
        x = hybrid_layer(x, c, w_in[l], w_out[l], cmp_pe[l], cmp_w1[l], cmp_w2[l],
                         w_ada[l], b_ada[l], ln_g[l], ln_b[l], rel_bias)
    return x
```

```python
import functools
import math

import numpy as np
import jax
import jax.numpy as jnp
from jax import lax
from jax.experimental import pallas as pl
from jax.experimental.pallas import tpu as pltpu

F32 = jnp.float32
BF16 = jnp.bfloat16

D_MODEL = 1024
HEAD_DIM = 64
H_A = 8
H_B = 8
G_B = 2
R_B = H_B // G_B
D_A = H_A * HEAD_DIM
D_B = H_B * HEAD_DIM
DILATED = ((128, 1), (512, 4), (2048, 16))
CMP_LEN = 32
CMP_STRIDE = 16
CMP_HIDDEN = 256
SLC_BLOCK = 64
SLC_TOPK = 16
SLC_LOCAL = 2
WIN = 512
N_BUCKETS = 32
MAX_DIST = 2048
LN_EPS = 1e-5
NEG = -1e30
LOG2E = 1.4426950408889634

LANES = 128
BLK = 128
VMEM_LIMIT = 56 * 1024 * 1024

_C_QA, _C_KA, _C_VA, _C_ZA, _C_QB = 0, 512, 1024, 1536, 2048
_C_KC, _C_VC, _C_KS, _C_VS, _C_KW, _C_VW = 2560, 2688, 2816, 2944, 3072, 3200
_C_ZB, _C_GB, _C_END = 3328, 3840, 3968


def _cparams(sem):
    return pltpu.CompilerParams(dimension_semantics=sem, vmem_limit_bytes=VMEM_LIMIT)


def _lo_mask():
    return lax.broadcasted_iota(jnp.int32, (1, LANES), 1) < HEAD_DIM


def _dot_nt(a, b):
    return lax.dot_general(a, b, (((1,), (1,)), ((), ())), preferred_element_type=F32)


def _bucket_np(dist):
    d = np.maximum(dist, 0)
    max_exact = N_BUCKETS // 2
    large = max_exact + (np.log(np.maximum(d, 1).astype(np.float32) / np.float32(max_exact))
                         / np.float32(math.log(MAX_DIST / max_exact))
                         * np.float32(N_BUCKETS - max_exact)).astype(np.int32)
    return np.where(d < max_exact, d, np.minimum(large, N_BUCKETS - 1)).astype(np.int32)


def _masked_bias(bias_by_bucket, dist, valid):
    t = jnp.moveaxis(bias_by_bucket[_bucket_np(dist)], -1, 0)
    return jnp.where(jnp.asarray(valid)[None], t, NEG)


def _band_table(bias_by_bucket, n_prev, window, scale):
    w = (n_prev + 1) * BLK
    dist = np.arange(BLK)[:, None] + n_prev * BLK - np.arange(w)[None, :]
    return _masked_bias(bias_by_bucket, dist * scale, (dist >= 0) & (dist <= window))


def _diag_table(bias_by_bucket, n_diag):
    i = np.arange(BLK)[:, None] - np.arange(BLK)[None, :]
    delta = np.arange(-1, n_diag + 1)[:, None, None]
    dist = delta * BLK + i[None]
    return _masked_bias(bias_by_bucket, dist, dist >= 0)


def _cmp_table(bias_by_bucket, seq):
    n_col = seq // CMP_STRIDE
    dist = np.arange(seq)[:, None] - (np.arange(n_col)[None, :] * CMP_STRIDE + CMP_LEN - 1)
    return _masked_bias(bias_by_bucket, dist, dist >= 0)


def _ada_kernel(c_ref, w_ref, b_ref, o_ref):
    o_ref[0] = jnp.dot(c_ref[...], w_ref[0].astype(BF16), preferred_element_type=F32) + b_ref[0]


def _ada_mod(c, w_ada, b_ada):
    depth, d, d3 = w_ada.shape
    cp = jnp.zeros((8, d), BF16).at[:c.shape[0]].set(c.astype(BF16))
    nj = d3 // d
    return pl.pallas_call(
        _ada_kernel,
        grid=(depth, nj),
        in_specs=[pl.BlockSpec((8, d), lambda l, j: (0, 0)),
                  pl.BlockSpec((1, d, d), lambda l, j: (l, 0, j)),
                  pl.BlockSpec((1, 1, d), lambda l, j: (l, 0, j))],
        out_specs=pl.BlockSpec((1, 8, d), lambda l, j: (l, 0, j)),
        out_shape=jax.ShapeDtypeStruct((depth, 8, d3), F32),
        compiler_params=_cparams(("parallel", "parallel")),
        name="ada_mod",
    )(cp, w_ada, b_ada.reshape(depth, 1, d3))


def _proj_kernel(x_ref, sh_ref, sc_ref, w_ref, qa, ka, va, za, qb, kc, vc, ks, vs, kw, vw, zb, gb, h_ref):
    h_ref[...] = (x_ref[0] * (1.0 + sc_ref[0]) + sh_ref[0]).astype(BF16)

    def mm(c0, n):
        return jnp.dot(h_ref[...], w_ref[:, c0:c0 + n], preferred_element_type=F32)

    qs = HEAD_DIM ** -0.5 * LOG2E
    qa[0] = (mm(_C_QA, D_A) * qs).astype(BF16)
    ka[0] = mm(_C_KA, D_A).astype(BF16)
    va[0] = mm(_C_VA, D_A).astype(BF16)
    za[0] = mm(_C_ZA, D_A).astype(BF16)
    qb[0] = (mm(_C_QB, D_B) * qs).astype(BF16)
    lo = _lo_mask()
    for ref, c0 in ((kc, _C_KC), (vc, _C_VC)):
        a = mm(c0, LANES)
        r = pltpu.roll(a, HEAD_DIM, 1)
        ref[0, 0] = a[:, :HEAD_DIM].astype(BF16)
        ref[0, 1] = r[:, :HEAD_DIM].astype(BF16)
    for ref, c0 in ((ks, _C_KS), (vs, _C_VS), (kw, _C_KW), (vw, _C_VW)):
        a = mm(c0, LANES)
        r = pltpu.roll(a, HEAD_DIM, 1)
        ref[0, :, :LANES] = jnp.where(lo, a, r).astype(BF16)
        ref[0, :, LANES:] = jnp.where(lo, r, a).astype(BF16)
    zb[0] = mm(_C_ZB, D_B).astype(BF16)
    gb[0] = mm(_C_GB, LANES)


def _project(x, shift, scl, w, tm=512):
    b, s, d = x.shape
    bs = lambda n: pl.BlockSpec((1, tm, n), lambda i, j: (i, j, 0))
    gs = pl.BlockSpec((1, G_B, tm, HEAD_DIM), lambda i, j: (i, 0, j, 0))
    sd = lambda n, dt=BF16: jax.ShapeDtypeStruct((b, s, n), dt)
    gd = jax.ShapeDtypeStruct((b, G_B, s, HEAD_DIM), BF16)
    mod = pl.BlockSpec((1, 1, d), lambda i, j: (i, 0, 0))
    return pl.pallas_call(
        _proj_kernel,
        grid=(b, s // tm),
        in_specs=[pl.BlockSpec((1, tm, d), lambda i, j: (i, j, 0)), mod, mod,
                  pl.BlockSpec((d, _C_END), lambda i, j: (0, 0))],
        out_specs=[bs(D_A), bs(D_A), bs(D_A), bs(D_A), bs(D_B), gs, gs,
                   bs(2 * LANES), bs(2 * LANES), bs(2 * LANES), bs(2 * LANES), bs(D_B), bs(LANES)],
        out_shape=[sd(D_A), sd(D_A), sd(D_A), sd(D_A), sd(D_B), gd, gd,
                   sd(2 * LANES), sd(2 * LANES), sd(2 * LANES), sd(2 * LANES), sd(D_B), sd(LANES, F32)],
        scratch_shapes=[pltpu.VMEM((tm, d), BF16)],
        compiler_params=_cparams(("parallel", "parallel")),
        name="in_proj",
    )(x, shift, scl, w)


def _band_kernel(q_ref, k_ref, v_ref, t_ref, *rest, n_prev, nblk, with_lse):
    if with_lse:
        o_ref, lse_ref, qlo, qhi, vlo, vhi = rest
    else:
        o_ref, qlo, qhi, vlo, vhi = rest
        lse_ref = None
    lo = _lo_mask()
    q = q_ref[0]
    v = v_ref[0]
    zero = jnp.zeros_like(q)
    qlo[...] = jnp.where(lo, q, zero)
    qhi[...] = jnp.where(lo, zero, q)
    vlo[...] = jnp.where(lo, v, zero)
    vhi[...] = jnp.where(lo, zero, v)

    def block(n, nk):
        r0 = n * BLK
        k0 = (n - (nk - 1)) * BLK
        if not isinstance(n, int):
            r0 = pl.multiple_of(r0, BLK)
            k0 = pl.multiple_of(k0, BLK)
        w0 = (n_prev + 1 - nk) * BLK
        kt = k_ref[0, pl.ds(k0, nk * BLK), :]
        o = None
        lse = []
        for h, (qsrc, vsrc) in enumerate(((qlo, vlo), (qhi, vhi))):
            s = _dot_nt(qsrc[pl.ds(r0, BLK), :], kt) + t_ref[0, h, :, w0:]
            m = jnp.max(s, axis=-1, keepdims=True)
            p = jnp.exp2(s - m)
            l = jnp.sum(p, axis=-1, keepdims=True)
            pv = jnp.dot(p.astype(BF16), vsrc[pl.ds(k0, nk * BLK), :], preferred_element_type=F32)
            pv = pv * (1.0 / l)
            o = pv if o is None else o + pv
            lse.append(m + jnp.log2(l))
        o_ref[0, pl.ds(r0, BLK), :] = o.astype(o_ref.dtype)
        if with_lse:
            lse_ref[0, pl.ds(r0, BLK), :] = jnp.where(lo, lse[0], lse[1])

    for n in range(min(n_prev, nblk)):
        block(n, n + 1)
    if nblk > n_prev:
        def body(n, carry):
            block(n, n_prev + 1)
            return carry
        lax.fori_loop(n_prev, nblk, body, 0)


def _band_attention(q, k, v, table, *, n_prev, kv_of_q, tbl_of_q, with_lse):
    b, r, cq = q.shape
    nblk = r // BLK
    w = table.shape[-1]
    qs = pl.BlockSpec((1, r, LANES), lambda i, j: (i, 0, j))
    ks = pl.BlockSpec((1, r, LANES), lambda i, j: (i, 0, kv_of_q(j)))
    ts = pl.BlockSpec((1, 2, BLK, w), lambda i, j: (tbl_of_q(j), 0, 0, 0))
    out_specs = [qs]
    out_shape = [jax.ShapeDtypeStruct((b, r, cq), BF16)]
    if with_lse:
        out_specs.append(qs)
        out_shape.append(jax.ShapeDtypeStruct((b, r, cq), F32))
    return pl.pallas_call(
        functools.partial(_band_kernel, n_prev=n_prev, nblk=nblk, with_lse=with_lse),
        grid=(b, cq // LANES),
        in_specs=[qs, ks, ks, ts],
        out_specs=out_specs,
        out_shape=out_shape,
        scratch_shapes=[pltpu.VMEM((r, LANES), BF16)] * 4,
        compiler_params=_cparams(("parallel", "parallel")),
        name=f"band_attn_p{n_prev}_r{r}",
    )(q, k, v, table)


def _compress_kernel(kin, vin, w1, w2, pe, kout, vout):
    n_chunk = kin.shape[2]
    half = w1.shape[1] // 2
    rows = lax.broadcasted_iota(jnp.int32, (n_chunk, 1), 0)
    for t, (xin, out) in enumerate(((kin, kout), (vin, vout))):
        x = xin[0, 0]
        a = jnp.dot(x, w1[t, :half], preferred_element_type=F32)
        b = jnp.dot(x, w1[t, half:], preferred_element_type=F32)
        c = jnp.dot(pe[t], w1[t], preferred_element_type=F32)[0:1]
        hid = a + pltpu.roll(b, n_chunk - 1, 0) + c
        act = hid * jax.nn.sigmoid(hid)
        o = jnp.dot(act.astype(BF16), w2[t], preferred_element_type=F32)
        out[0, 0] = jnp.where(rows < n_chunk - 1, o, 0.0).astype(BF16)


def _compress(kc_in, vc_in, w1, w2d, pe):
    b, g, n_chunk, f = kc_in.shape
    xs = pl.BlockSpec((1, 1, n_chunk, f), lambda i, j: (i, j, 0, 0))
    os_ = pl.BlockSpec((1, 1, n_chunk, LANES), lambda i, j: (i, j, 0, 0))
    full = lambda a: pl.BlockSpec(a.shape, lambda i, j: (0,) * a.ndim)
    od = jax.ShapeDtypeStruct((b, g, n_chunk, LANES), BF16)
    return pl.pallas_call(
        _compress_kernel,
        grid=(b, g),
        in_specs=[xs, xs, full(w1), full(w2d), full(pe)],
        out_specs=[os_, os_],
        out_shape=[od, od],
        compiler_params=_cparams(("parallel", "parallel")),
        name="nsa_compress",
    )(kc_in, vc_in, w1, w2d, pe)


def _cmp_kernel(q_ref, kc_ref, vc_ref, t_ref, ov_ref, o_ref, sb_ref, *, tq):
    i = pl.program_id(2)
    lo = _lo_mask()
    kcd = kc_ref[0, 0]
    vcd = vc_ref[0, 0]
    zero = jnp.zeros_like(vcd)
    vlo = jnp.where(lo, vcd, zero)
    vhi = jnp.where(lo, zero, vcd)
    psum = None
    for pr in range(R_B // 2):
        q = q_ref[0, :, pr * LANES:(pr + 1) * LANES]
        qz = jnp.zeros_like(q)
        acc = None
        for hh, (qm, vm) in enumerate(((jnp.where(lo, q, qz), vlo), (jnp.where(lo, qz, q), vhi))):
            s = _dot_nt(qm, kcd) + t_ref[2 * pr + hh]
            m = jnp.max(s, axis=-1, keepdims=True)
            e = jnp.where(s > 0.5 * NEG, jnp.exp2(s - m), 0.0)
            l = jnp.sum(e, axis=-1, keepdims=True)
            p = e * (1.0 / jnp.where(l > 0.0, l, 1.0))
            psum = p if psum is None else psum + p
            pv = jnp.dot(p.astype(BF16), vm, preferred_element_type=F32)
            acc = pv if acc is None else acc + pv
        o_ref[0, :, pr * LANES:(pr + 1) * LANES] = acc.astype(o_ref.dtype)

    p_hi = psum.astype(BF16)
    p_lo = (psum - p_hi.astype(F32)).astype(BF16)
    ov = ov_ref[...]
    score = (_dot_nt(ov, p_hi) + _dot_nt(ov, p_lo))
    n_slc = score.shape[0]
    blk = lax.broadcasted_iota(jnp.int32, (n_slc, 1), 0)
    pos = i * tq + lax.broadcasted_iota(jnp.int32, (1, tq), 1)
    cur = pos >> int(math.log2(SLC_BLOCK))
    forced = (blk == 0) | ((cur - blk >= 0) & (cur - blk < SLC_LOCAL))
    score = jnp.where(forced, 1e9, jnp.where(blk > cur, -1e9, score))
    sub = lax.broadcasted_iota(jnp.int32, (8, 1), 0)
    sel_rows = []
    for g8 in range(n_slc // 8):
        sg = score[g8 * 8:(g8 + 1) * 8]
        rank = jnp.zeros(sg.shape, F32)
        for mp in range(n_slc):
            row = score[mp:mp + 1]
            if mp < g8 * 8:
                ahead = jnp.where(row >= sg, 1.0, 0.0)
            elif mp >= (g8 + 1) * 8:
                ahead = jnp.where(row > sg, 1.0, 0.0)
            else:
                ahead = jnp.where(sub > mp - g8 * 8, jnp.where(row >= sg, 1.0, 0.0), jnp.where(row > sg, 1.0, 0.0))
            rank = rank + ahead
        sel_rows.append(jnp.where(rank < float(min(SLC_TOPK, n_slc)), 0.0, NEG))
    pad = LANES // n_slc
    selb = jnp.concatenate(sel_rows * pad, axis=0)
    sb_ref[0, 0] = selb.T.astype(BF16)


def _cmp_attention(qb, kcd, vcd, table, ov_t, tq=256):
    b, s, _ = qb.shape
    n_col = kcd.shape[2]
    gw = R_B * HEAD_DIM
    return pl.pallas_call(
        functools.partial(_cmp_kernel, tq=tq),
        grid=(b, G_B, s // tq),
        in_specs=[pl.BlockSpec((1, tq, gw), lambda i, g, j: (i, j, g)),
                  pl.BlockSpec((1, 1, n_col, LANES), lambda i, g, j: (i, g, 0, 0)),
                  pl.BlockSpec((1, 1, n_col, LANES), lambda i, g, j: (i, g, 0, 0)),
                  pl.BlockSpec((R_B, tq, n_col), lambda i, g, j: (g, j, 0)),
                  pl.BlockSpec(ov_t.shape, lambda i, g, j: (0, 0))],
        out_specs=[pl.BlockSpec((1, tq, gw), lambda i, g, j: (i, j, g)),
                   pl.BlockSpec((1, 1, tq, LANES), lambda i, g, j: (i, g, j, 0))],
        out_shape=[jax.ShapeDtypeStruct((b, s, D_B), BF16),
                   jax.ShapeDtypeStruct((b, G_B, s, LANES), BF16)],
        compiler_params=_cparams(("parallel", "parallel", "parallel")),
        name="nsa_cmp_topk",
    )(qb, kcd, vcd, table, ov_t)


def _sel_kernel(q_ref, sb_ref, k_ref, v_ref, t_ref, o_ref, kaug, vlo, vhi, m_ref, l_ref, acc_ref, *, nblk):
    lo = _lo_mask()
    s_len = nblk * BLK
    n_diag = t_ref.shape[1] - 1
    rowblk = lax.broadcasted_iota(jnp.int32, (s_len, LANES), 0) >> int(math.log2(SLC_BLOCK))
    lane = lax.broadcasted_iota(jnp.int32, (s_len, LANES), 1)
    onehot = jnp.where(lane - HEAD_DIM == rowblk, 1.0, 0.0).astype(BF16)
    kaug[...] = jnp.where(lo, k_ref[0], onehot)
    v = v_ref[0]
    zero = jnp.zeros_like(v)
    vlo[...] = jnp.where(lo, v, zero)
    vhi[...] = jnp.where(lo, zero, v)

    def qtile(n, carry):
        r0 = pl.multiple_of(n * BLK, BLK)
        sbf = sb_ref[0, 0, pl.ds(r0, BLK), :].astype(F32)
        rows = []
        for pr in range(R_B // 2):
            q = q_ref[0, pl.ds(r0, BLK), pr * LANES:(pr + 1) * LANES].astype(F32)
            rows.append(jnp.where(lo, q, sbf))
            rows.append(jnp.where(lo, pltpu.roll(q, HEAD_DIM, 1), sbf))
        qst = jnp.concatenate(rows, axis=0).astype(BF16)
        m_ref[...] = jnp.full(m_ref.shape, NEG, F32)
        l_ref[...] = jnp.zeros(l_ref.shape, F32)
        acc_ref[...] = jnp.zeros(acc_ref.shape, F32)

        def ktile(jt, c):
            k0 = pl.multiple_of(jt * 2 * BLK, 2 * BLK)
            s = _dot_nt(qst, kaug[pl.ds(k0, 2 * BLK), :])
            e0 = jnp.minimum(n - 2 * jt + 1, n_diag)
            e1 = jnp.minimum(n - 2 * jt, n_diag)
            for pr in range(R_B // 2):
                alphas = []
                pvs = []
                for hh, vsrc in enumerate((vlo, vhi)):
                    h = 2 * pr + hh
                    sh = s[h * BLK:(h + 1) * BLK] + jnp.concatenate([t_ref[h, e0], t_ref[h, e1]], axis=1)
                    m_prev = m_ref[h]
                    m_new = jnp.maximum(m_prev, jnp.max(sh, axis=-1, keepdims=True))
                    alpha = jnp.exp2(m_prev - m_new)
                    p = jnp.exp2(sh - jnp.concatenate([m_new, m_new], axis=1))
                    l_ref[h] = alpha * l_ref[h] + jnp.sum(p, axis=-1, keepdims=True)
                    m_ref[h] = m_new
                    alphas.append(alpha)
                    pvs.append(jnp.dot(p.astype(BF16), vsrc[pl.ds(k0, 2 * BLK), :], preferred_element_type=F32))
                acc_ref[pr] = jnp.where(lo, alphas[0], alphas[1]) * acc_ref[pr] + pvs[0] + pvs[1]
            return c

        lax.fori_loop(0, (n >> 1) + 1, ktile, 0)
        for pr in range(R_B // 2):
            inv = jnp.where(lo, 1.0 / l_ref[2 * pr], 1.0 / l_ref[2 * pr + 1])
            o_ref[0, pl.ds(r0, BLK), pr * LANES:(pr + 1) * LANES] = (acc_ref[pr] * inv).astype(o_ref.dtype)
        return carry

    lax.fori_loop(0, nblk, qtile, 0)


def _sel_attention(qb, selb, ks_dup, vs_dup, table):
    b, s, _ = qb.shape
    gw = R_B * HEAD_DIM
    n_e = table.shape[1]
    return pl.pallas_call(
        functools.partial(_sel_kernel, nblk=s // BLK),
        grid=(b, G_B),
        in_specs=[pl.BlockSpec((1, s, gw), lambda i, g: (i, 0, g)),
                  pl.BlockSpec((1, 1, s, LANES), lambda i, g: (i, g, 0, 0)),
                  pl.BlockSpec((1, s, LANES), lambda i, g: (i, 0, g)),
                  pl.BlockSpec((1, s, LANES), lambda i, g: (i, 0, g)),
                  pl.BlockSpec((R_B, n_e, BLK, BLK), lambda i, g: (g, 0, 0, 0))],
        out_specs=pl.BlockSpec((1, s, gw), lambda i, g: (i, 0, g)),
        out_shape=jax.ShapeDtypeStruct((b, s, D_B), BF16),
        scratch_shapes=[pltpu.VMEM((s, LANES), BF16)] * 3
        + [pltpu.VMEM((R_B, BLK, LANES), F32)] * 2 + [pltpu.VMEM((R_B // 2, BLK, LANES), F32)],
        compiler_params=_cparams(("parallel", "parallel")),
        name="nsa_selected",
    )(qb, selb, ks_dup, vs_dup, table)


def _combine_kernel(x_ref, o1, o2, o3, l1, l2, l3, za, oc, osl, ow, gb, zb, wo, ex, gate, lng, lnb, out_ref, *, alpha):
    f = lambda r: r[0].astype(F32)
    a1, a2, a3 = l1[0], l2[0], l3[0]
    mx = jnp.maximum(jnp.maximum(a1, a2), a3)
    e1, e2, e3 = jnp.exp2(a1 - mx), jnp.exp2(a2 - mx), jnp.exp2(a3 - mx)
    oa = (e1 * f(o1) + e2 * f(o2) + e3 * f(o3)) * (1.0 / (e1 + e2 + e3))
    z = f(za)
    mix_a = oa * (z * jax.nn.sigmoid(z))
    g = jax.nn.sigmoid(gb[0])
    g_hi = g.astype(BF16)
    g_lo = (g - g_hi.astype(F32)).astype(BF16)
    gx = (jnp.dot(g_hi, ex[...], preferred_element_type=F32)
          + jnp.dot(g_lo, ex[...], preferred_element_type=F32))
    ob = gx[:, :D_B] * f(oc) + gx[:, D_B:2 * D_B] * f(osl) + gx[:, 2 * D_B:] * f(ow)
    z = f(zb)
    mix_b = ob * (z * jax.nn.sigmoid(z))
    y = (jnp.dot(mix_a.astype(BF16), wo[:D_A], preferred_element_type=F32)
         + jnp.dot(mix_b.astype(BF16), wo[D_A:], preferred_element_type=F32))
    r = alpha * x_ref[0] + (1.0 + gate[0]) * y
    mu = jnp.mean(r, axis=-1, keepdims=True)
    rc = r - mu
    var = jnp.mean(rc * rc, axis=-1, keepdims=True)
    out_ref[0] = rc * lax.rsqrt(var + LN_EPS) * lng[...] + lnb[...]


def _combine(x, oas, lses, za, o_cmp, o_slc, o_win, gb, zb, w_out, expand, gate, ln_g, ln_b, alpha, tm=256):
    b, s, d = x.shape
    bs = lambda n: pl.BlockSpec((1, tm, n), lambda i, j: (i, j, 0))
    full = lambda a: pl.BlockSpec(a.shape, lambda i, j: (0,) * a.ndim)
    return pl.pallas_call(
        functools.partial(_combine_kernel, alpha=alpha),
        grid=(b, s // tm),
        in_specs=[bs(d)] + [bs(D_A)] * 7 + [bs(D_B)] * 3 + [bs(LANES), bs(D_B), full(w_out), full(expand),
                  pl.BlockSpec((1, 1, d), lambda i, j: (i, 0, 0)), full(ln_g), full(ln_b)],
        out_specs=bs(d),
        out_shape=jax.ShapeDtypeStruct((b, s, d), F32),
        compiler_params=_cparams(("parallel", "parallel")),
        name="merge_out_proj_ln",
    )(x, *oas, *lses, za, o_cmp, o_slc, o_win, gb, zb, w_out, expand, gate, ln_g, ln_b)


def _pack_w_in(w_in):
    n_gb = 3 * H_B
    main = w_in[..., :_C_ZB]
    gbw = w_in[..., _C_ZB:_C_ZB + n_gb]
    zbw = w_in[..., _C_ZB + n_gb:]
    pad = jnp.zeros(w_in.shape[:-1] + (LANES - n_gb,), w_in.dtype)
    return jnp.concatenate([main, zbw, gbw, pad], axis=-1).astype(BF16)


def _gate_expand():
    e = np.zeros((LANES, 3 * D_B), np.float32)
    for i in range(3):
        for h in range(H_B):
            e[i * H_B + h, i * D_B + h * HEAD_DIM:i * D_B + (h + 1) * HEAD_DIM] = 1.0
    return jnp.asarray(e, BF16)


def _overlap_t(n_col, n_slc):
    cs = np.arange(n_col)[None, :] * CMP_STRIDE
    ss = np.arange(n_slc)[:, None] * SLC_BLOCK
    return jnp.asarray(((cs < ss + SLC_BLOCK) & (cs + CMP_LEN > ss)).astype(np.float32), BF16)


def _layer(x, shift, scl, gate, w_in_p, w_out_b, pe_b, w1_b, w2d_b, ln_g, ln_b, tables, consts, alpha):
    b, s, d = x.shape
    (qa, ka, va, za, qb, kc_in, vc_in, ks, vs, kw, vw, zb, gb) = _project(x, shift, scl, w_in_p)

    oas, lses = [], []
    n_pair = D_A // LANES
    for (window, dil), tbl in zip(DILATED, tables["dilated"]):
        view = lambda t: t.reshape(b, s // dil, dil * t.shape[-1])
        o, lse = _band_attention(view(qa), view(ka), view(va), tbl, n_prev=-(-(window // dil) // BLK),
                                 kv_of_q=lambda j: j, tbl_of_q=lambda j: j % n_pair, with_lse=True)
        oas.append(o.reshape(b, s, D_A))
        lses.append(lse.reshape(b, s, D_A))

    n_chunk = s // CMP_STRIDE
    flat = lambda t: t.reshape(b, G_B, n_chunk, CMP_STRIDE * HEAD_DIM)
    kcd, vcd = _compress(flat(kc_in), flat(vc_in), w1_b, w2d_b, pe_b)
    o_cmp, selb = _cmp_attention(qb, kcd, vcd, tables["cmp"], consts["overlap_t"])
    o_slc = _sel_attention(qb, selb, ks, vs, tables["sel"])
    (o_win,) = _band_attention(qb, kw, vw, tables["win"], n_prev=-(-(WIN - 1) // BLK),
                               kv_of_q=lambda j: j // (R_B // 2), tbl_of_q=lambda j: j, with_lse=False)

    return _combine(x, oas, lses, za, o_cmp, o_slc, o_win, gb, zb, w_out_b, consts["expand"], gate,
                    ln_g, ln_b, alpha)


def kernel(x, c, w_in, w_out, cmp_pe, cmp_w1, cmp_w2, w_ada, b_ada, ln_g, ln_b, rel_bias):
    b, s, d = x.shape
    depth = w_in.shape[0]
    alpha = (2 * depth) ** 0.25
    n_pair = D_A // LANES

    bb = rel_bias.astype(F32) * LOG2E
    pair = lambda t: t.reshape(n_pair, 2, *t.shape[1:])
    tables = {
        "dilated": [pair(_band_table(bb[:, :H_A], -(-(w // dl) // BLK), w // dl, dl)) for w, dl in DILATED],
        "win": pair(_band_table(bb[:, H_A:], -(-(WIN - 1) // BLK), WIN - 1, 1)),
        "cmp": _cmp_table(bb[:, H_A:], s),
        "sel": _diag_table(bb[:, H_A:], 13),
    }
    consts = {"expand": _gate_expand(), "overlap_t": _overlap_t(s // CMP_STRIDE, s // SLC_BLOCK)}

    mod = _ada_mod(c, w_ada, b_ada)
    w_in_p = _pack_w_in(w_in)
    w_out_b = w_out.astype(BF16)
    w1_b = cmp_w1.astype(BF16)
    w2d_b = jnp.concatenate([cmp_w2, cmp_w2], axis=-1).astype(BF16)
    pe_b = jnp.broadcast_to(cmp_pe.reshape(depth, 2, 1, CMP_LEN * HEAD_DIM), (depth, 2, 8, CMP_LEN * HEAD_DIM)).astype(BF16)

    for l in range(depth):
        m = mod[l, :b]
        shift, scl, gate = (m[:, i * d:(i + 1) * d].reshape(b, 1, d) for i in range(3))
        x = _layer(x, shift, scl, gate, w_in_p[l], w_out_b[l], pe_b[l], w1_b[l], w2d_b[l],
                   ln_g[l].reshape(1, d), ln_b[l].reshape(1, d), tables, consts, alpha)
    return x
```

```python
import functools
import math

import numpy as np
import jax
import jax.numpy as jnp
from jax import lax
from jax.experimental import pallas as pl
from jax.experimental.pallas import tpu as pltpu

F32 = jnp.float32
BF16 = jnp.bfloat16

D_MODEL = 1024
HEAD_DIM = 64
H_A = 8
H_B = 8
G_B = 2
R_B = H_B // G_B
D_A = H_A * HEAD_DIM
D_B = H_B * HEAD_DIM
DILATED = ((128, 1), (512, 4), (2048, 16))
CMP_LEN = 32
CMP_STRIDE = 16
CMP_HIDDEN = 256
SLC_BLOCK = 64
SLC_TOPK = 16
SLC_LOCAL = 2
WIN = 512
N_BUCKETS = 32
MAX_DIST = 2048
LN_EPS = 1e-5
NEG = -1e30
LOG2E = 1.4426950408889634

LANES = 128
BLK = 128
VMEM_LIMIT = 56 * 1024 * 1024

_C_QA, _C_KA, _C_VA, _C_ZA, _C_QB = 0, 512, 1024, 1536, 2048
_C_KC, _C_VC, _C_KS, _C_VS, _C_KW, _C_VW = 2560, 2688, 2816, 2944, 3072, 3200
_C_ZB, _C_GB, _C_END = 3328, 3840, 3968


def _cparams(sem):
    return pltpu.CompilerParams(dimension_semantics=sem, vmem_limit_bytes=VMEM_LIMIT)


def _lo_mask():
    return lax.broadcasted_iota(jnp.int32, (1, LANES), 1) < HEAD_DIM


def _dot_nt(a, b):
    return lax.dot_general(a, b, (((1,), (1,)), ((), ())), preferred_element_type=F32)


def _bucket_np(dist):
    d = np.maximum(dist, 0)
    max_exact = N_BUCKETS // 2
    large = max_exact + (np.log(np.maximum(d, 1).astype(np.float32) / np.float32(max_exact))
                         / np.float32(math.log(MAX_DIST / max_exact))
                         * np.float32(N_BUCKETS - max_exact)).astype(np.int32)
    return np.where(d < max_exact, d, np.minimum(large, N_BUCKETS - 1)).astype(np.int32)


def _masked_bias(bias_by_bucket, dist, valid):
    t = jnp.moveaxis(bias_by_bucket[_bucket_np(dist)], -1, 0)
    return jnp.where(jnp.asarray(valid)[None], t, NEG)


def _toeplitz(h, rows, cols):
    ext = jnp.concatenate([h[..., rows - 1:], h[..., :1], h[..., :rows - 1]], axis=-1)
    lead = h.shape[:-1]
    flat = jnp.tile(ext, (1,) * len(lead) + (rows,))[..., :rows * (rows - 1 + cols)]
    return flat.reshape(*lead, rows, rows - 1 + cols)[..., :cols]


def _band_table(bias_by_bucket, n_prev, window, scale):
    w = (n_prev + 1) * BLK
    dist = n_prev * BLK - (np.arange(BLK - 1 + w) - (BLK - 1))
    return _toeplitz(_masked_bias(bias_by_bucket, dist * scale, (dist >= 0) & (dist <= window)), BLK, w)


def _diag_table(bias_by_bucket, n_diag):
    n_e = n_diag + 2
    dist = n_diag * BLK - (np.arange(BLK - 1 + n_e * BLK) - (BLK - 1))
    big = _toeplitz(_masked_bias(bias_by_bucket, dist, dist >= 0), BLK, n_e * BLK)
    return jnp.moveaxis(big.reshape(-1, BLK, n_e, BLK), 2, 1)[:, ::-1]


def _cmp_table(bias_by_bucket, seq):
    n_col = seq // CMP_STRIDE
    t = np.arange(2 * n_col - 1) - (n_col - 1)
    dist = -CMP_STRIDE * t[None, :] + np.arange(CMP_STRIDE)[:, None] - (CMP_LEN - 1)
    per_r = _toeplitz(_masked_bias(bias_by_bucket, dist, dist >= 0), n_col, n_col)
    return jnp.moveaxis(per_r, 1, 2).reshape(-1, seq, n_col)


def _ada_kernel(c_ref, w_ref, b_ref, o_ref):
    o_ref[0] = jnp.dot(c_ref[...], w_ref[0].astype(BF16), preferred_element_type=F32) + b_ref[0]


def _ada_mod(c, w_ada, b_ada):
    depth, d, d3 = w_ada.shape
    cp = jnp.zeros((8, d), BF16).at[:c.shape[0]].set(c.astype(BF16))
    nj = d3 // d
    return pl.pallas_call(
        _ada_kernel,
        grid=(depth, nj),
        in_specs=[pl.BlockSpec((8, d), lambda l, j: (0, 0)),
                  pl.BlockSpec((1, d, d), lambda l, j: (l, 0, j)),
                  pl.BlockSpec((1, 1, d), lambda l, j: (l, 0, j))],
        out_specs=pl.BlockSpec((1, 8, d), lambda l, j: (l, 0, j)),
        out_shape=jax.ShapeDtypeStruct((depth, 8, d3), F32),
        compiler_params=_cparams(("parallel", "parallel")),
        name="ada_mod",
    )(cp, w_ada, b_ada.reshape(depth, 1, d3))


def _proj_kernel(x_ref, sh_ref, sc_ref, w_ref, qa, ka, va, za, qb, kc, vc, ks, vs, kw, vw, zb, gb, h_ref):
    h_ref[...] = (x_ref[0] * (1.0 + sc_ref[0]) + sh_ref[0]).astype(BF16)

    def mm(c0, n):
        return jnp.dot(h_ref[...], w_ref[:, c0:c0 + n], preferred_element_type=F32)

    qs = HEAD_DIM ** -0.5 * LOG2E
    qa[0] = (mm(_C_QA, D_A) * qs).astype(BF16)
    ka[0] = mm(_C_KA, D_A).astype(BF16)
    va[0] = mm(_C_VA, D_A).astype(BF16)
    za[0] = mm(_C_ZA, D_A).astype(BF16)
    qb[0] = (mm(_C_QB, D_B) * qs).astype(BF16)
    lo = _lo_mask()
    for ref, c0 in ((kc, _C_KC), (vc, _C_VC)):
        a = mm(c0, LANES)
        r = pltpu.roll(a, HEAD_DIM, 1)
        ref[0, 0] = a[:, :HEAD_DIM].astype(BF16)
        ref[0, 1] = r[:, :HEAD_DIM].astype(BF16)
    for ref, c0 in ((ks, _C_KS), (vs, _C_VS), (kw, _C_KW), (vw, _C_VW)):
        a = mm(c0, LANES)
        r = pltpu.roll(a, HEAD_DIM, 1)
        ref[0, :, :LANES] = jnp.where(lo, a, r).astype(BF16)
        ref[0, :, LANES:] = jnp.where(lo, r, a).astype(BF16)
    zb[0] = mm(_C_ZB, D_B).astype(BF16)
    gb[0] = mm(_C_GB, LANES)


def _project(x, shift, scl, w, tm=512):
    b, s, d = x.shape
    bs = lambda n: pl.BlockSpec((1, tm, n), lambda i, j: (i, j, 0))
    gs = pl.BlockSpec((1, G_B, tm, HEAD_DIM), lambda i, j: (i, 0, j, 0))
    sd = lambda n, dt=BF16: jax.ShapeDtypeStruct((b, s, n), dt)
    gd = jax.ShapeDtypeStruct((b, G_B, s, HEAD_DIM), BF16)
    mod = pl.BlockSpec((1, 1, d), lambda i, j: (i, 0, 0))
    return pl.pallas_call(
        _proj_kernel,
        grid=(b, s // tm),
        in_specs=[pl.BlockSpec((1, tm, d), lambda i, j: (i, j, 0)), mod, mod,
                  pl.BlockSpec((d, _C_END), lambda i, j: (0, 0))],
        out_specs=[bs(D_A), bs(D_A), bs(D_A), bs(D_A), bs(D_B), gs, gs,
                   bs(2 * LANES), bs(2 * LANES), bs(2 * LANES), bs(2 * LANES), bs(D_B), bs(LANES)],
        out_shape=[sd(D_A), sd(D_A), sd(D_A), sd(D_A), sd(D_B), gd, gd,
                   sd(2 * LANES), sd(2 * LANES), sd(2 * LANES), sd(2 * LANES), sd(D_B), sd(LANES, F32)],
        scratch_shapes=[pltpu.VMEM((tm, d), BF16)],
        compiler_params=_cparams(("parallel", "parallel")),
        name="in_proj",
    )(x, shift, scl, w)


def _dilated_kernel(q_ref, k_ref, v_ref, t1_ref, t2_ref, t3_ref, o_ref,
                    x32, qlo, qhi, kp, vp, o_r, l_r, o_t, l_t, m_acc, n_acc, d_acc, *, seq, unroll):
    lo = _lo_mask()
    nblk = seq // BLK
    chunk = 512

    for pi, ((_, dil), t_ref) in enumerate(zip(DILATED, (t1_ref, t2_ref, t3_ref))):
        rl = seq // dil
        nb = rl // BLK
        nb_shift = int(math.log2(nb))

        def regroup(src_ref, dsts, pad):
            x32[...] = src_ref[0].astype(F32)
            for r in range(dil):
                rows = (x32[pl.ds(r, rl, stride=dil), :] if dil > 1 else x32[...]).astype(BF16)
                base = r * (rl + pad) + pad
                for dst, keep in dsts:
                    if pad:
                        dst[base - pad:base, :] = jnp.zeros((pad, LANES), BF16)
                    if keep is None:
                        dst[base:base + rl, :] = rows
                    else:
                        dst[base:base + rl, :] = jnp.where(keep, rows, jnp.zeros_like(rows))

        regroup(q_ref, ((qlo, lo), (qhi, ~lo)), 0)
        regroup(k_ref, ((kp, None),), BLK)
        regroup(v_ref, ((vp, None),), BLK)

        def blocks(it, carry):
            for u in range(unroll):
                g = it * unroll + u
                r = g >> nb_shift
                first = jnp.where((g & (nb - 1)) == 0, 1, 0)
                q0 = pl.multiple_of(g * BLK, BLK)
                k0 = pl.multiple_of((g + r) * BLK, BLK)
                qs = jnp.concatenate([qlo[pl.ds(q0, BLK), :], qhi[pl.ds(q0, BLK), :]], axis=0)
                s = _dot_nt(qs, kp[pl.ds(k0, 2 * BLK), :]) + t_ref[0, first]
                m = jnp.max(s, axis=-1, keepdims=True)
                p = jnp.exp2(s - m)
                l = jnp.sum(p, axis=-1, keepdims=True)
                pv = jnp.dot(p.astype(BF16), vp[pl.ds(k0, 2 * BLK), :], preferred_element_type=F32) * (1.0 / l)
                lse = m + jnp.log2(l)
                o_r[pl.ds(q0, BLK), :] = jnp.where(lo, pv[:BLK], pv[BLK:])
                l_r[pl.ds(q0, BLK), :] = jnp.where(lo, lse[:BLK], lse[BLK:])
            return carry

        lax.fori_loop(0, nblk // unroll, blocks, 0)

        if dil > 1:
            for r in range(dil):
                o_t[pl.ds(r, rl, stride=dil), :] = o_r[r * rl:(r + 1) * rl, :]
                l_t[pl.ds(r, rl, stride=dil), :] = l_r[r * rl:(r + 1) * rl, :]
            o_src, l_src = o_t, l_t
        else:
            o_src, l_src = o_r, l_r

        def fold(c, carry, pi=pi, o_src=o_src, l_src=l_src):
            rows = pl.ds(pl.multiple_of(c * chunk, chunk), chunk)
            o_new, l_new = o_src[rows, :], l_src[rows, :]
            if pi == 0:
                m_acc[rows, :] = l_new
                n_acc[rows, :] = o_new
                d_acc[rows, :] = jnp.ones_like(l_new)
            else:
                m_old = m_acc[rows, :]
                m_new = jnp.maximum(m_old, l_new)
                a, b_ = jnp.exp2(m_old - m_new), jnp.exp2(l_new - m_new)
                n_new = a * n_acc[rows, :] + b_ * o_new
                d_new = a * d_acc[rows, :] + b_
                if pi == len(DILATED) - 1:
                    o_ref[0, rows, :] = (n_new * (1.0 / d_new)).astype(o_ref.dtype)
                else:
                    m_acc[rows, :] = m_new
                    n_acc[rows, :] = n_new
                    d_acc[rows, :] = d_new
            return carry

        lax.fori_loop(0, seq // chunk, fold, 0)


def _dilated_attention(qa, ka, va, tables, unroll=8):
    b, s, c = qa.shape
    max_dil = max(d for _, d in DILATED)
    xs = pl.BlockSpec((1, s, LANES), lambda i, j: (i, 0, j))
    ts = pl.BlockSpec((1, 2, 2 * BLK, 2 * BLK), lambda i, j: (j, 0, 0, 0))
    big = lambda rows, dt: pltpu.VMEM((rows, LANES), dt)
    return pl.pallas_call(
        functools.partial(_dilated_kernel, seq=s, unroll=unroll),
        grid=(b, c // LANES),
        in_specs=[xs, xs, xs, ts, ts, ts],
        out_specs=xs,
        out_shape=jax.ShapeDtypeStruct((b, s, c), BF16),
        scratch_shapes=[big(s, F32), big(s, BF16), big(s, BF16)]
        + [big(s + max_dil * BLK, BF16)] * 2 + [big(s, F32)] * 7,
        compiler_params=_cparams(("parallel", "parallel")),
        name="dilated_attn",
    )(qa, ka, va, *tables)


def _win_kernel(q_ref, k_ref, v_ref, t_ref, o_ref, qst, kaug, vp, *, seq, n_prev, unroll):
    lo = _lo_mask()
    lane = lax.broadcasted_iota(jnp.int32, (1, LANES), 1)
    flag = lane == HEAD_DIM
    pad = n_prev * BLK
    span = (n_prev + 1) * BLK
    negflag = jnp.where(flag, NEG, 0.0)
    for pr in range(R_B // 2):
        q = q_ref[0, :, pr * LANES:(pr + 1) * LANES].astype(F32)
        qst[2 * pr] = jnp.where(lo, q, negflag).astype(BF16)
        qst[2 * pr + 1] = jnp.where(lo, pltpu.roll(q, HEAD_DIM, 1), negflag).astype(BF16)
    k = k_ref[0]
    kaug[:pad, :] = jnp.broadcast_to(jnp.where(flag, 1.0, 0.0), (pad, LANES)).astype(BF16)
    kaug[pad:, :] = jnp.where(lo, k, jnp.zeros_like(k))
    vp[:pad, :] = jnp.zeros((pad, LANES), BF16)
    vp[pad:, :] = v_ref[0]

    def blocks(it, carry):
        for u in range(unroll):
            n = it * unroll + u
            r0 = pl.multiple_of(n * BLK, BLK)
            qs = jnp.concatenate([qst[h, pl.ds(r0, BLK), :] for h in range(R_B)], axis=0)
            s = _dot_nt(qs, kaug[pl.ds(r0, span), :]) + t_ref[0]
            m = jnp.max(s, axis=-1, keepdims=True)
            p = jnp.exp2(s - m)
            l = jnp.sum(p, axis=-1, keepdims=True)
            pv = jnp.dot(p.astype(BF16), vp[pl.ds(r0, span), :], preferred_element_type=F32) * (1.0 / l)
            for pr in range(R_B // 2):
                o = jnp.where(lo, pv[2 * pr * BLK:(2 * pr + 1) * BLK], pv[(2 * pr + 1) * BLK:(2 * pr + 2) * BLK])
                o_ref[0, pl.ds(r0, BLK), pr * LANES:(pr + 1) * LANES] = o.astype(o_ref.dtype)
        return carry

    lax.fori_loop(0, seq // BLK // unroll, blocks, 0)


def _win_attention(qb, kw_dup, vw_dup, table, n_prev, unroll=4):
    b, s, _ = qb.shape
    gw = R_B * HEAD_DIM
    w = table.shape[-1]
    return pl.pallas_call(
        functools.partial(_win_kernel, seq=s, n_prev=n_prev, unroll=unroll),
        grid=(b, G_B),
        in_specs=[pl.BlockSpec((1, s, gw), lambda i, g: (i, 0, g)),
                  pl.BlockSpec((1, s, LANES), lambda i, g: (i, 0, g)),
                  pl.BlockSpec((1, s, LANES), lambda i, g: (i, 0, g)),
                  pl.BlockSpec((1, R_B * BLK, w), lambda i, g: (g, 0, 0))],
        out_specs=pl.BlockSpec((1, s, gw), lambda i, g: (i, 0, g)),
        out_shape=jax.ShapeDtypeStruct((b, s, D_B), BF16),
        scratch_shapes=[pltpu.VMEM((R_B, s, LANES), BF16)] + [pltpu.VMEM((s + n_prev * BLK, LANES), BF16)] * 2,
        compiler_params=_cparams(("parallel", "parallel")),
        name="nsa_window",
    )(qb, kw_dup, vw_dup, table)


def _compress_kernel(kin, vin, w1, w2, pe, kout, vout):
    n_chunk = kin.shape[2]
    half = w1.shape[1] // 2
    rows = lax.broadcasted_iota(jnp.int32, (n_chunk, 1), 0)
    for t, (xin, out) in enumerate(((kin, kout), (vin, vout))):
        x = xin[0, 0]
        a = jnp.dot(x, w1[t, :half], preferred_element_type=F32)
        b = jnp.dot(x, w1[t, half:], preferred_element_type=F32)
        c = jnp.dot(pe[t], w1[t], preferred_element_type=F32)[0:1]
        hid = a + pltpu.roll(b, n_chunk - 1, 0) + c
        act = hid * jax.nn.sigmoid(hid)
        o = jnp.dot(act.astype(BF16), w2[t], preferred_element_type=F32)
        out[0, 0] = jnp.where(rows < n_chunk - 1, o, 0.0).astype(BF16)


def _compress(kc_in, vc_in, w1, w2d, pe):
    b, g, n_chunk, f = kc_in.shape
    xs = pl.BlockSpec((1, 1, n_chunk, f), lambda i, j: (i, j, 0, 0))
    os_ = pl.BlockSpec((1, 1, n_chunk, LANES), lambda i, j: (i, j, 0, 0))
    full = lambda a: pl.BlockSpec(a.shape, lambda i, j: (0,) * a.ndim)
    od = jax.ShapeDtypeStruct((b, g, n_chunk, LANES), BF16)
    return pl.pallas_call(
        _compress_kernel,
        grid=(b, g),
        in_specs=[xs, xs, full(w1), full(w2d), full(pe)],
        out_specs=[os_, os_],
        out_shape=[od, od],
        compiler_params=_cparams(("parallel", "parallel")),
        name="nsa_compress",
    )(kc_in, vc_in, w1, w2d, pe)


def _cmp_kernel(q_ref, kc_ref, vc_ref, t_ref, ov_ref, o_ref, sb_ref, *, tq):
    i = pl.program_id(2)
    lo = _lo_mask()
    kcd = kc_ref[0, 0]
    vcd = vc_ref[0, 0]
    zero = jnp.zeros_like(vcd)
    vlo = jnp.where(lo, vcd, zero)
    vhi = jnp.where(lo, zero, vcd)
    psum = None
    for pr in range(R_B // 2):
        q = q_ref[0, :, pr * LANES:(pr + 1) * LANES]
        qz = jnp.zeros_like(q)
        acc = None
        for hh, (qm, vm) in enumerate(((jnp.where(lo, q, qz), vlo), (jnp.where(lo, qz, q), vhi))):
            s = _dot_nt(qm, kcd) + t_ref[2 * pr + hh]
            m = jnp.max(s, axis=-1, keepdims=True)
            e = jnp.where(s > 0.5 * NEG, jnp.exp2(s - m), 0.0)
            l = jnp.sum(e, axis=-1, keepdims=True)
            p = e * (1.0 / jnp.where(l > 0.0, l, 1.0))
            psum = p if psum is None else psum + p
            pv = jnp.dot(p.astype(BF16), vm, preferred_element_type=F32)
            acc = pv if acc is None else acc + pv
        o_ref[0, :, pr * LANES:(pr + 1) * LANES] = acc.astype(o_ref.dtype)

    p_hi = psum.astype(BF16)
    p_lo = (psum - p_hi.astype(F32)).astype(BF16)
    ov = ov_ref[...]
    score = (_dot_nt(ov, p_hi) + _dot_nt(ov, p_lo))
    n_slc = score.shape[0]
    blk = lax.broadcasted_iota(jnp.int32, (n_slc, 1), 0)
    pos = i * tq + lax.broadcasted_iota(jnp.int32, (1, tq), 1)
    cur = pos >> int(math.log2(SLC_BLOCK))
    forced = (blk == 0) | ((cur - blk >= 0) & (cur - blk < SLC_LOCAL))
    score = jnp.where(forced, 1e9, jnp.where(blk > cur, -1e9, score))
    sub = lax.broadcasted_iota(jnp.int32, (8, 1), 0)
    sel_rows = []
    for g8 in range(n_slc // 8):
        sg = score[g8 * 8:(g8 + 1) * 8]
        rank = jnp.zeros(sg.shape, F32)
        for mp in range(n_slc):
            row = score[mp:mp + 1]
            if mp < g8 * 8:
                ahead = jnp.where(row >= sg, 1.0, 0.0)
            elif mp >= (g8 + 1) * 8:
                ahead = jnp.where(row > sg, 1.0, 0.0)
            else:
                ahead = jnp.where(sub > mp - g8 * 8, jnp.where(row >= sg, 1.0, 0.0), jnp.where(row > sg, 1.0, 0.0))
            rank = rank + ahead
        sel_rows.append(jnp.where(rank < float(min(SLC_TOPK, n_slc)), 0.0, NEG))
    pad = LANES // n_slc
    selb = jnp.concatenate(sel_rows * pad, axis=0)
    sb_ref[0, 0] = selb.T.astype(BF16)


def _cmp_attention(qb, kcd, vcd, table, ov_t, tq=256):
    b, s, _ = qb.shape
    n_col = kcd.shape[2]
    gw = R_B * HEAD_DIM
    return pl.pallas_call(
        functools.partial(_cmp_kernel, tq=tq),
        grid=(b, G_B, s // tq),
        in_specs=[pl.BlockSpec((1, tq, gw), lambda i, g, j: (i, j, g)),
                  pl.BlockSpec((1, 1, n_col, LANES), lambda i, g, j: (i, g, 0, 0)),
                  pl.BlockSpec((1, 1, n_col, LANES), lambda i, g, j: (i, g, 0, 0)),
                  pl.BlockSpec((R_B, tq, n_col), lambda i, g, j: (g, j, 0)),
                  pl.BlockSpec(ov_t.shape, lambda i, g, j: (0, 0))],
        out_specs=[pl.BlockSpec((1, tq, gw), lambda i, g, j: (i, j, g)),
                   pl.BlockSpec((1, 1, tq, LANES), lambda i, g, j: (i, g, j, 0))],
        out_shape=[jax.ShapeDtypeStruct((b, s, D_B), BF16),
                   jax.ShapeDtypeStruct((b, G_B, s, LANES), BF16)],
        compiler_params=_cparams(("parallel", "parallel", "parallel")),
        name="nsa_cmp_topk",
    )(qb, kcd, vcd, table, ov_t)


def _sel_kernel(q_ref, sb_ref, k_ref, v_ref, t_ref, o_ref, kaug, m_ref, l_ref, acc_ref, *, nblk):
    lo = _lo_mask()
    s_len = nblk * BLK
    n_diag = t_ref.shape[1] - 1
    rowblk = lax.broadcasted_iota(jnp.int32, (s_len, LANES), 0) >> int(math.log2(SLC_BLOCK))
    lane = lax.broadcasted_iota(jnp.int32, (s_len, LANES), 1)
    onehot = jnp.where(lane - HEAD_DIM == rowblk, 1.0, 0.0).astype(BF16)
    kaug[...] = jnp.where(lo, k_ref[0], onehot)
    vp = v_ref.at[0]

    def qtile(n, carry):
        r0 = pl.multiple_of(n * BLK, BLK)
        sbf = sb_ref[0, 0, pl.ds(r0, BLK), :].astype(F32)
        rows = []
        for pr in range(R_B // 2):
            q = q_ref[0, pl.ds(r0, BLK), pr * LANES:(pr + 1) * LANES].astype(F32)
            rows.append(jnp.where(lo, q, sbf))
            rows.append(jnp.where(lo, pltpu.roll(q, HEAD_DIM, 1), sbf))
        qst = jnp.concatenate(rows, axis=0).astype(BF16)
        m_ref[...] = jnp.full(m_ref.shape, NEG, F32)
        l_ref[...] = jnp.zeros(l_ref.shape, F32)
        acc_ref[...] = jnp.zeros(acc_ref.shape, F32)

        def ktile(jt, c):
            k0 = pl.multiple_of(jt * 2 * BLK, 2 * BLK)
            s = _dot_nt(qst, kaug[pl.ds(k0, 2 * BLK), :])
            e0 = jnp.minimum(n - 2 * jt + 1, n_diag)
            e1 = jnp.minimum(n - 2 * jt, n_diag)
            ps = []
            for h in range(R_B):
                sh = s[h * BLK:(h + 1) * BLK] + jnp.concatenate([t_ref[h, e0], t_ref[h, e1]], axis=1)
                m_prev = m_ref[h]
                m_new = jnp.maximum(m_prev, jnp.max(sh, axis=-1, keepdims=True))
                alpha = jnp.exp2(m_prev - m_new)
                p = jnp.exp2(sh - jnp.concatenate([m_new, m_new], axis=1))
                l_ref[h] = alpha * l_ref[h] + jnp.sum(p, axis=-1, keepdims=True)
                m_ref[h] = m_new
                acc_ref[h] = alpha * acc_ref[h]
                ps.append(p.astype(BF16))
            pv = jnp.dot(jnp.concatenate(ps, axis=0), vp[pl.ds(k0, 2 * BLK), :], preferred_element_type=F32)
            for h in range(R_B):
                acc_ref[h] += pv[h * BLK:(h + 1) * BLK]
            return c

        lax.fori_loop(0, (n >> 1) + 1, ktile, 0)
        for pr in range(R_B // 2):
            o = jnp.where(lo, acc_ref[2 * pr] * (1.0 / l_ref[2 * pr]), acc_ref[2 * pr + 1] * (1.0 / l_ref[2 * pr + 1]))
            o_ref[0, pl.ds(r0, BLK), pr * LANES:(pr + 1) * LANES] = o.astype(o_ref.dtype)
        return carry

    lax.fori_loop(0, nblk, qtile, 0)


def _sel_attention(qb, selb, ks_dup, vs_dup, table):
    b, s, _ = qb.shape
    gw = R_B * HEAD_DIM
    n_e = table.shape[1]
    return pl.pallas_call(
        functools.partial(_sel_kernel, nblk=s // BLK),
        grid=(b, G_B),
        in_specs=[pl.BlockSpec((1, s, gw), lambda i, g: (i, 0, g)),
                  pl.BlockSpec((1, 1, s, LANES), lambda i, g: (i, g, 0, 0)),
                  pl.BlockSpec((1, s, LANES), lambda i, g: (i, 0, g)),
                  pl.BlockSpec((1, s, LANES), lambda i, g: (i, 0, g)),
                  pl.BlockSpec((R_B, n_e, BLK, BLK), lambda i, g: (g, 0, 0, 0))],
        out_specs=pl.BlockSpec((1, s, gw), lambda i, g: (i, 0, g)),
        out_shape=jax.ShapeDtypeStruct((b, s, D_B), BF16),
        scratch_shapes=[pltpu.VMEM((s, LANES), BF16)] + [pltpu.VMEM((R_B, BLK, LANES), F32)] * 3,
        compiler_params=_cparams(("parallel", "parallel")),
        name="nsa_selected",
    )(qb, selb, ks_dup, vs_dup, table)


def _combine_kernel(x_ref, oa, za, oc, osl, ow, gb, zb, wo, ex, gate, lng, lnb, out_ref, *, alpha):
    f = lambda r: r[0].astype(F32)
    z = f(za)
    mix_a = f(oa) * (z * jax.nn.sigmoid(z))
    g = jax.nn.sigmoid(gb[0])
    g_hi = g.astype(BF16)
    g_lo = (g - g_hi.astype(F32)).astype(BF16)
    gx = (jnp.dot(g_hi, ex[...], preferred_element_type=F32)
          + jnp.dot(g_lo, ex[...], preferred_element_type=F32))
    ob = gx[:, :D_B] * f(oc) + gx[:, D_B:2 * D_B] * f(osl) + gx[:, 2 * D_B:] * f(ow)
    z = f(zb)
    mix_b = ob * (z * jax.nn.sigmoid(z))
    y = (jnp.dot(mix_a.astype(BF16), wo[:D_A], preferred_element_type=F32)
         + jnp.dot(mix_b.astype(BF16), wo[D_A:], preferred_element_type=F32))
    r = alpha * x_ref[0] + (1.0 + gate[0]) * y
    mu = jnp.mean(r, axis=-1, keepdims=True)
    rc = r - mu
    var = jnp.mean(rc * rc, axis=-1, keepdims=True)
    out_ref[0] = rc * lax.rsqrt(var + LN_EPS) * lng[...] + lnb[...]


def _combine(x, o_a, za, o_cmp, o_slc, o_win, gb, zb, w_out, expand, gate, ln_g, ln_b, alpha, tm=512):
    b, s, d = x.shape
    bs = lambda n: pl.BlockSpec((1, tm, n), lambda i, j: (i, j, 0))
    full = lambda a: pl.BlockSpec(a.shape, lambda i, j: (0,) * a.ndim)
    return pl.pallas_call(
        functools.partial(_combine_kernel, alpha=alpha),
        grid=(b, s // tm),
        in_specs=[bs(d)] + [bs(D_A)] * 2 + [bs(D_B)] * 3 + [bs(LANES), bs(D_B), full(w_out), full(expand),
                  pl.BlockSpec((1, 1, d), lambda i, j: (i, 0, 0)), full(ln_g), full(ln_b)],
        out_specs=bs(d),
        out_shape=jax.ShapeDtypeStruct((b, s, d), F32),
        compiler_params=_cparams(("parallel", "parallel")),
        name="merge_out_proj_ln",
    )(x, o_a, za, o_cmp, o_slc, o_win, gb, zb, w_out, expand, gate, ln_g, ln_b)


def _pack_w_in(w_in):
    n_gb = 3 * H_B
    main = w_in[..., :_C_ZB]
    gbw = w_in[..., _C_ZB:_C_ZB + n_gb]
    zbw = w_in[..., _C_ZB + n_gb:]
    pad = jnp.zeros(w_in.shape[:-1] + (LANES - n_gb,), w_in.dtype)
    return jnp.concatenate([main, zbw, gbw, pad], axis=-1).astype(BF16)


def _gate_expand():
    e = np.zeros((LANES, 3 * D_B), np.float32)
    for i in range(3):
        for h in range(H_B):
            e[i * H_B + h, i * D_B + h * HEAD_DIM:i * D_B + (h + 1) * HEAD_DIM] = 1.0
    return jnp.asarray(e, BF16)


def _overlap_t(n_col, n_slc):
    cs = np.arange(n_col)[None, :] * CMP_STRIDE
    ss = np.arange(n_slc)[:, None] * SLC_BLOCK
    return jnp.asarray(((cs < ss + SLC_BLOCK) & (cs + CMP_LEN > ss)).astype(np.float32), BF16)


def _layer(x, shift, scl, gate, w_in_p, w_out_b, pe_b, w1_b, w2d_b, ln_g, ln_b, tables, consts, alpha):
    b, s, d = x.shape
    (qa, ka, va, za, qb, kc_in, vc_in, ks, vs, kw, vw, zb, gb) = _project(x, shift, scl, w_in_p)

    o_a = _dilated_attention(qa, ka, va, tables["dilated"])

    n_chunk = s // CMP_STRIDE
    flat = lambda t: t.reshape(b, G_B, n_chunk, CMP_STRIDE * HEAD_DIM)
    kcd, vcd = _compress(flat(kc_in), flat(vc_in), w1_b, w2d_b, pe_b)
    o_cmp, selb = _cmp_attention(qb, kcd, vcd, tables["cmp"], consts["overlap_t"])
    o_slc = _sel_attention(qb, selb, ks, vs, tables["sel"])
    o_win = _win_attention(qb, kw, vw, tables["win"], n_prev=-(-(WIN - 1) // BLK))

    return _combine(x, o_a, za, o_cmp, o_slc, o_win, gb, zb, w_out_b, consts["expand"], gate,
                    ln_g, ln_b, alpha)


def kernel(x, c, w_in, w_out, cmp_pe, cmp_w1, cmp_w2, w_ada, b_ada, ln_g, ln_b, rel_bias):
    b, s, d = x.shape
    depth = w_in.shape[0]
    alpha = (2 * depth) ** 0.25
    n_pair = D_A // LANES

    bb = rel_bias.astype(F32) * LOG2E
    def dilated_table(window, dil):
        t = _band_table(bb[:, :H_A], 1, window // dil, dil).reshape(n_pair, 2 * BLK, 2 * BLK)
        first = t.at[..., :BLK].set(NEG)
        return jnp.stack([t, first], axis=1)

    n_prev_win = -(-(WIN - 1) // BLK)
    tables = {
        "dilated": [dilated_table(w, dl) for w, dl in DILATED],
        "win": _band_table(bb[:, H_A:], n_prev_win, WIN - 1, 1).reshape(G_B, R_B * BLK, (n_prev_win + 1) * BLK),
        "cmp": _cmp_table(bb[:, H_A:], s),
        "sel": _diag_table(bb[:, H_A:], 13),
    }
    consts = {"expand": _gate_expand(), "overlap_t": _overlap_t(s // CMP_STRIDE, s // SLC_BLOCK)}

    mod = _ada_mod(c, w_ada, b_ada)
    w_in_p = _pack_w_in(w_in)
    w_out_b = w_out.astype(BF16)
    w1_b = cmp_w1.astype(BF16)
    w2d_b = jnp.concatenate([cmp_w2, cmp_w2], axis=-1).astype(BF16)
    pe_b = jnp.broadcast_to(cmp_pe.reshape(depth, 2, 1, CMP_LEN * HEAD_DIM), (depth, 2, 8, CMP_LEN * HEAD_DIM)).astype(BF16)

    for l in range(depth):
        m = mod[l, :b]
        shift, scl, gate = (m[:, i * d:(i + 1) * d].reshape(b, 1, d) for i in range(3))
        x = _layer(x, shift, scl, gate, w_in_p[l], w_out_b[l], pe_b[l], w1_b[l], w2d_b[l],
                   ln_g[l].reshape(1, d), ln_b[l].reshape(1, d), tables, consts, alpha)
    return x
```

```python
import functools
import math

import numpy as np
import jax
import jax.numpy as jnp
from jax import lax
from jax.experimental import pallas as pl
from jax.experimental.pallas import tpu as pltpu

F32 = jnp.float32
BF16 = jnp.bfloat16

D_MODEL = 1024
HEAD_DIM = 64
H_A = 8
H_B = 8
G_B = 2
R_B = H_B // G_B
D_A = H_A * HEAD_DIM
D_B = H_B * HEAD_DIM
DILATED = ((128, 1), (512, 4), (2048, 16))
CMP_LEN = 32
CMP_STRIDE = 16
CMP_HIDDEN = 256
SLC_BLOCK = 64
SLC_TOPK = 16
SLC_LOCAL = 2
WIN = 512
N_BUCKETS = 32
MAX_DIST = 2048
LN_EPS = 1e-5
NEG = -1e30
LOG2E = 1.4426950408889634

LANES = 128
BLK = 128
VMEM_LIMIT = 56 * 1024 * 1024

_C_QA, _C_KA, _C_VA, _C_ZA, _C_QB = 0, 512, 1024, 1536, 2048
_C_KC, _C_VC, _C_KS, _C_VS, _C_KW, _C_VW = 2560, 2688, 2816, 2944, 3072, 3200
_C_ZB, _C_GB, _C_END = 3328, 3840, 3968


def _cparams(sem):
    return pltpu.CompilerParams(dimension_semantics=sem, vmem_limit_bytes=VMEM_LIMIT)


def _lo_mask():
    return lax.broadcasted_iota(jnp.int32, (1, LANES), 1) < HEAD_DIM


def _dot_nt(a, b):
    return lax.dot_general(a, b, (((1,), (1,)), ((), ())), preferred_element_type=F32)


def _bucket_np(dist):
    d = np.maximum(dist, 0)
    max_exact = N_BUCKETS // 2
    large = max_exact + (np.log(np.maximum(d, 1).astype(np.float32) / np.float32(max_exact))
                         / np.float32(math.log(MAX_DIST / max_exact))
                         * np.float32(N_BUCKETS - max_exact)).astype(np.int32)
    return np.where(d < max_exact, d, np.minimum(large, N_BUCKETS - 1)).astype(np.int32)


def _masked_bias(bias_by_bucket, dist, valid):
    t = jnp.moveaxis(bias_by_bucket[_bucket_np(dist)], -1, 0)
    return jnp.where(jnp.asarray(valid)[None], t, NEG)


def _toeplitz(h, rows, cols):
    ext = jnp.concatenate([h[..., rows - 1:], h[..., :1], h[..., :rows - 1]], axis=-1)
    lead = h.shape[:-1]
    flat = jnp.tile(ext, (1,) * len(lead) + (rows,))[..., :rows * (rows - 1 + cols)]
    return flat.reshape(*lead, rows, rows - 1 + cols)[..., :cols]


def _band_table(bias_by_bucket, n_prev, window, scale):
    w = (n_prev + 1) * BLK
    dist = n_prev * BLK - (np.arange(BLK - 1 + w) - (BLK - 1))
    return _toeplitz(_masked_bias(bias_by_bucket, dist * scale, (dist >= 0) & (dist <= window)), BLK, w)


def _diag_table(bias_by_bucket, n_diag):
    n_e = n_diag + 2
    dist = n_diag * BLK - (np.arange(BLK - 1 + n_e * BLK) - (BLK - 1))
    big = _toeplitz(_masked_bias(bias_by_bucket, dist, dist >= 0), BLK, n_e * BLK)
    return jnp.moveaxis(big.reshape(-1, BLK, n_e, BLK), 2, 1)[:, ::-1]


def _cmp_table(bias_by_bucket, seq):
    n_col = seq // CMP_STRIDE
    t = np.arange(2 * n_col - 1) - (n_col - 1)
    dist = -CMP_STRIDE * t[None, :] + np.arange(CMP_STRIDE)[:, None] - (CMP_LEN - 1)
    per_r = _toeplitz(_masked_bias(bias_by_bucket, dist, dist >= 0), n_col, n_col)
    return jnp.moveaxis(per_r, 1, 2).reshape(-1, seq, n_col)


def _ada_kernel(c_ref, w_ref, b_ref, o_ref):
    o_ref[0] = jnp.dot(c_ref[...], w_ref[0].astype(BF16), preferred_element_type=F32) + b_ref[0]


def _ada_mod(c, w_ada, b_ada):
    depth, d, d3 = w_ada.shape
    cp = jnp.zeros((8, d), BF16).at[:c.shape[0]].set(c.astype(BF16))
    nj = d3 // d
    return pl.pallas_call(
        _ada_kernel,
        grid=(depth, nj),
        in_specs=[pl.BlockSpec((8, d), lambda l, j: (0, 0)),
                  pl.BlockSpec((1, d, d), lambda l, j: (l, 0, j)),
                  pl.BlockSpec((1, 1, d), lambda l, j: (l, 0, j))],
        out_specs=pl.BlockSpec((1, 8, d), lambda l, j: (l, 0, j)),
        out_shape=jax.ShapeDtypeStruct((depth, 8, d3), F32),
        compiler_params=_cparams(("parallel", "parallel")),
        name="ada_mod",
    )(cp, w_ada, b_ada.reshape(depth, 1, d3))


def _proj_kernel(x_ref, sh_ref, sc_ref, w_ref, qa, ka, va, za, qb, kc, vc, ks, vs, kw, vw, zb, gb, h_ref):
    h_ref[...] = (x_ref[0] * (1.0 + sc_ref[0]) + sh_ref[0]).astype(BF16)

    def mm(c0, n):
        return jnp.dot(h_ref[...], w_ref[:, c0:c0 + n], preferred_element_type=F32)

    qs = HEAD_DIM ** -0.5 * LOG2E
    qa[0] = (mm(_C_QA, D_A) * qs).astype(BF16)
    ka[0] = mm(_C_KA, D_A).astype(BF16)
    va[0] = mm(_C_VA, D_A).astype(BF16)
    za[0] = mm(_C_ZA, D_A).astype(BF16)
    qb[0] = (mm(_C_QB, D_B) * qs).astype(BF16)
    lo = _lo_mask()
    for ref, c0 in ((kc, _C_KC), (vc, _C_VC)):
        a = mm(c0, LANES)
        r = pltpu.roll(a, HEAD_DIM, 1)
        ref[0, 0] = a[:, :HEAD_DIM].astype(BF16)
        ref[0, 1] = r[:, :HEAD_DIM].astype(BF16)
    for ref, c0 in ((ks, _C_KS), (vs, _C_VS), (kw, _C_KW), (vw, _C_VW)):
        a = mm(c0, LANES)
        r = pltpu.roll(a, HEAD_DIM, 1)
        ref[0, :, :LANES] = jnp.where(lo, a, r).astype(BF16)
        ref[0, :, LANES:] = jnp.where(lo, r, a).astype(BF16)
    zb[0] = mm(_C_ZB, D_B).astype(BF16)
    gb[0] = mm(_C_GB, LANES)


def _project(x, shift, scl, w, tm=512):
    b, s, d = x.shape
    bs = lambda n: pl.BlockSpec((1, tm, n), lambda i, j: (i, j, 0))
    gs = pl.BlockSpec((1, G_B, tm, HEAD_DIM), lambda i, j: (i, 0, j, 0))
    sd = lambda n, dt=BF16: jax.ShapeDtypeStruct((b, s, n), dt)
    gd = jax.ShapeDtypeStruct((b, G_B, s, HEAD_DIM), BF16)
    mod = pl.BlockSpec((1, 1, d), lambda i, j: (i, 0, 0))
    return pl.pallas_call(
        _proj_kernel,
        grid=(b, s // tm),
        in_specs=[pl.BlockSpec((1, tm, d), lambda i, j: (i, j, 0)), mod, mod,
                  pl.BlockSpec((d, _C_END), lambda i, j: (0, 0))],
        out_specs=[bs(D_A), bs(D_A), bs(D_A), bs(D_A), bs(D_B), gs, gs,
                   bs(2 * LANES), bs(2 * LANES), bs(2 * LANES), bs(2 * LANES), bs(D_B), bs(LANES)],
        out_shape=[sd(D_A), sd(D_A), sd(D_A), sd(D_A), sd(D_B), gd, gd,
                   sd(2 * LANES), sd(2 * LANES), sd(2 * LANES), sd(2 * LANES), sd(D_B), sd(LANES, F32)],
        scratch_shapes=[pltpu.VMEM((tm, d), BF16)],
        compiler_params=_cparams(("parallel", "parallel")),
        name="in_proj",
    )(x, shift, scl, w)


def _dilated_kernel(q_ref, k_ref, v_ref, t1_ref, t2_ref, t3_ref, o_ref,
                    x32, qlo, qhi, kp, vp, o_r, l_r, o_t, l_t, m_acc, n_acc, d_acc, *, seq, unroll):
    lo = _lo_mask()
    nblk = seq // BLK
    chunk = 512

    for pi, ((_, dil), t_ref) in enumerate(zip(DILATED, (t1_ref, t2_ref, t3_ref))):
        rl = seq // dil
        nb = rl // BLK
        nb_shift = int(math.log2(nb))

        def regroup(src_ref, dsts, pad):
            x32[...] = src_ref[0].astype(F32)
            for r in range(dil):
                rows = (x32[pl.ds(r, rl, stride=dil), :] if dil > 1 else x32[...]).astype(BF16)
                base = r * (rl + pad) + pad
                for dst, keep in dsts:
                    if pad:
                        dst[base - pad:base, :] = jnp.zeros((pad, LANES), BF16)
                    if keep is None:
                        dst[base:base + rl, :] = rows
                    else:
                        dst[base:base + rl, :] = jnp.where(keep, rows, jnp.zeros_like(rows))

        regroup(q_ref, ((qlo, lo), (qhi, ~lo)), 0)
        regroup(k_ref, ((kp, None),), BLK)
        regroup(v_ref, ((vp, None),), BLK)

        def blocks(it, carry):
            for u in range(unroll):
                g = it * unroll + u
                r = g >> nb_shift
                first = jnp.where((g & (nb - 1)) == 0, 1, 0)
                q0 = pl.multiple_of(g * BLK, BLK)
                k0 = pl.multiple_of((g + r) * BLK, BLK)
                qs = jnp.concatenate([qlo[pl.ds(q0, BLK), :], qhi[pl.ds(q0, BLK), :]], axis=0)
                s = _dot_nt(qs, kp[pl.ds(k0, 2 * BLK), :]) + t_ref[0, first]
                m = jnp.max(s, axis=-1, keepdims=True)
                p = jnp.exp2(s - m)
                l = jnp.sum(p, axis=-1, keepdims=True)
                pv = jnp.dot(p.astype(BF16), vp[pl.ds(k0, 2 * BLK), :], preferred_element_type=F32) * (1.0 / l)
                lse = m + jnp.log2(l)
                o_r[pl.ds(q0, BLK), :] = jnp.where(lo, pv[:BLK], pv[BLK:])
                l_r[pl.ds(q0, BLK), :] = jnp.where(lo, lse[:BLK], lse[BLK:])
            return carry

        lax.fori_loop(0, nblk // unroll, blocks, 0)

        if dil > 1:
            for r in range(dil):
                o_t[pl.ds(r, rl, stride=dil), :] = o_r[r * rl:(r + 1) * rl, :]
                l_t[pl.ds(r, rl, stride=dil), :] = l_r[r * rl:(r + 1) * rl, :]
            o_src, l_src = o_t, l_t
        else:
            o_src, l_src = o_r, l_r

        def fold(c, carry, pi=pi, o_src=o_src, l_src=l_src):
            rows = pl.ds(pl.multiple_of(c * chunk, chunk), chunk)
            o_new, l_new = o_src[rows, :], l_src[rows, :]
            if pi == 0:
                m_acc[rows, :] = l_new
                n_acc[rows, :] = o_new
                d_acc[rows, :] = jnp.ones_like(l_new)
            else:
                m_old = m_acc[rows, :]
                m_new = jnp.maximum(m_old, l_new)
                a, b_ = jnp.exp2(m_old - m_new), jnp.exp2(l_new - m_new)
                n_new = a * n_acc[rows, :] + b_ * o_new
                d_new = a * d_acc[rows, :] + b_
                if pi == len(DILATED) - 1:
                    o_ref[0, rows, :] = (n_new * (1.0 / d_new)).astype(o_ref.dtype)
                else:
                    m_acc[rows, :] = m_new
                    n_acc[rows, :] = n_new
                    d_acc[rows, :] = d_new
            return carry

        lax.fori_loop(0, seq // chunk, fold, 0)


def _dilated_attention(qa, ka, va, tables, unroll=8):
    b, s, c = qa.shape
    max_dil = max(d for _, d in DILATED)
    xs = pl.BlockSpec((1, s, LANES), lambda i, j: (i, 0, j))
    ts = pl.BlockSpec((1, 2, 2 * BLK, 2 * BLK), lambda i, j: (j, 0, 0, 0))
    big = lambda rows, dt: pltpu.VMEM((rows, LANES), dt)
    return pl.pallas_call(
        functools.partial(_dilated_kernel, seq=s, unroll=unroll),
        grid=(b, c // LANES),
        in_specs=[xs, xs, xs, ts, ts, ts],
        out_specs=xs,
        out_shape=jax.ShapeDtypeStruct((b, s, c), BF16),
        scratch_shapes=[big(s, F32), big(s, BF16), big(s, BF16)]
        + [big(s + max_dil * BLK, BF16)] * 2 + [big(s, F32)] * 7,
        compiler_params=_cparams(("parallel", "parallel")),
        name="dilated_attn",
    )(qa, ka, va, *tables)


def _win_kernel(q_ref, k_ref, v_ref, t_ref, o_ref, qst, kaug, vp, *, seq, n_prev, unroll):
    lo = _lo_mask()
    lane = lax.broadcasted_iota(jnp.int32, (1, LANES), 1)
    flag = lane == HEAD_DIM
    pad = n_prev * BLK
    span = (n_prev + 1) * BLK
    negflag = jnp.where(flag, NEG, 0.0)
    for pr in range(R_B // 2):
        q = q_ref[0, :, pr * LANES:(pr + 1) * LANES].astype(F32)
        qst[2 * pr] = jnp.where(lo, q, negflag).astype(BF16)
        qst[2 * pr + 1] = jnp.where(lo, pltpu.roll(q, HEAD_DIM, 1), negflag).astype(BF16)
    k = k_ref[0]
    kaug[:pad, :] = jnp.broadcast_to(jnp.where(flag, 1.0, 0.0), (pad, LANES)).astype(BF16)
    kaug[pad:, :] = jnp.where(lo, k, jnp.zeros_like(k))
    vp[:pad, :] = jnp.zeros((pad, LANES), BF16)
    vp[pad:, :] = v_ref[0]

    def blocks(it, carry):
        for u in range(unroll):
            n = it * unroll + u
            r0 = pl.multiple_of(n * BLK, BLK)
            qs = jnp.concatenate([qst[h, pl.ds(r0, BLK), :] for h in range(R_B)], axis=0)
            s = _dot_nt(qs, kaug[pl.ds(r0, span), :]) + t_ref[0]
            m = jnp.max(s, axis=-1, keepdims=True)
            p = jnp.exp2(s - m)
            l = jnp.sum(p, axis=-1, keepdims=True)
            pv = jnp.dot(p.astype(BF16), vp[pl.ds(r0, span), :], preferred_element_type=F32) * (1.0 / l)
            for pr in range(R_B // 2):
                o = jnp.where(lo, pv[2 * pr * BLK:(2 * pr + 1) * BLK], pv[(2 * pr + 1) * BLK:(2 * pr + 2) * BLK])
                o_ref[0, pl.ds(r0, BLK), pr * LANES:(pr + 1) * LANES] = o.astype(o_ref.dtype)
        return carry

    lax.fori_loop(0, seq // BLK // unroll, blocks, 0)


def _win_attention(qb, kw_dup, vw_dup, table, n_prev, unroll=4):
    b, s, _ = qb.shape
    gw = R_B * HEAD_DIM
    w = table.shape[-1]
    return pl.pallas_call(
        functools.partial(_win_kernel, seq=s, n_prev=n_prev, unroll=unroll),
        grid=(b, G_B),
        in_specs=[pl.BlockSpec((1, s, gw), lambda i, g: (i, 0, g)),
                  pl.BlockSpec((1, s, LANES), lambda i, g: (i, 0, g)),
                  pl.BlockSpec((1, s, LANES), lambda i, g: (i, 0, g)),
                  pl.BlockSpec((1, R_B * BLK, w), lambda i, g: (g, 0, 0))],
        out_specs=pl.BlockSpec((1, s, gw), lambda i, g: (i, 0, g)),
        out_shape=jax.ShapeDtypeStruct((b, s, D_B), BF16),
        scratch_shapes=[pltpu.VMEM((R_B, s, LANES), BF16)] + [pltpu.VMEM((s + n_prev * BLK, LANES), BF16)] * 2,
        compiler_params=_cparams(("parallel", "parallel")),
        name="nsa_window",
    )(qb, kw_dup, vw_dup, table)


def _compress_kernel(kin, vin, w1, w2, pe, kout, vout):
    n_chunk = kin.shape[2]
    half = w1.shape[1] // 2
    rows = lax.broadcasted_iota(jnp.int32, (n_chunk, 1), 0)
    for t, (xin, out) in enumerate(((kin, kout), (vin, vout))):
        x = xin[0, 0]
        a = jnp.dot(x, w1[t, :half], preferred_element_type=F32)
        b = jnp.dot(x, w1[t, half:], preferred_element_type=F32)
        c = jnp.dot(pe[t], w1[t], preferred_element_type=F32)[0:1]
        hid = a + pltpu.roll(b, n_chunk - 1, 0) + c
        act = hid * jax.nn.sigmoid(hid)
        o = jnp.dot(act.astype(BF16), w2[t], preferred_element_type=F32)
        out[0, 0] = jnp.where(rows < n_chunk - 1, o, 0.0).astype(BF16)


def _compress(kc_in, vc_in, w1, w2d, pe):
    b, g, n_chunk, f = kc_in.shape
    xs = pl.BlockSpec((1, 1, n_chunk, f), lambda i, j: (i, j, 0, 0))
    os_ = pl.BlockSpec((1, 1, n_chunk, LANES), lambda i, j: (i, j, 0, 0))
    full = lambda a: pl.BlockSpec(a.shape, lambda i, j: (0,) * a.ndim)
    od = jax.ShapeDtypeStruct((b, g, n_chunk, LANES), BF16)
    return pl.pallas_call(
        _compress_kernel,
        grid=(b, g),
        in_specs=[xs, xs, full(w1), full(w2d), full(pe)],
        out_specs=[os_, os_],
        out_shape=[od, od],
        compiler_params=_cparams(("parallel", "parallel")),
        name="nsa_compress",
    )(kc_in, vc_in, w1, w2d, pe)


def _cmp_kernel(q_ref, kc_ref, vc_ref, t_ref, ov_ref, o_ref, sb_ref, *, tq):
    i = pl.program_id(2)
    lo = _lo_mask()
    kcd = kc_ref[0, 0]
    vcd = vc_ref[0, 0]
    zero = jnp.zeros_like(vcd)
    vlo = jnp.where(lo, vcd, zero)
    vhi = jnp.where(lo, zero, vcd)
    psum = None
    for pr in range(R_B // 2):
        q = q_ref[0, :, pr * LANES:(pr + 1) * LANES]
        qz = jnp.zeros_like(q)
        acc = None
        for hh, (qm, vm) in enumerate(((jnp.where(lo, q, qz), vlo), (jnp.where(lo, qz, q), vhi))):
            s = _dot_nt(qm, kcd) + t_ref[2 * pr + hh]
            m = jnp.max(s, axis=-1, keepdims=True)
            e = jnp.where(s > 0.5 * NEG, jnp.exp2(s - m), 0.0)
            l = jnp.sum(e, axis=-1, keepdims=True)
            p = e * (1.0 / jnp.where(l > 0.0, l, 1.0))
            psum = p if psum is None else psum + p
            pv = jnp.dot(p.astype(BF16), vm, preferred_element_type=F32)
            acc = pv if acc is None else acc + pv
        o_ref[0, :, pr * LANES:(pr + 1) * LANES] = acc.astype(o_ref.dtype)

    p_hi = psum.astype(BF16)
    p_lo = (psum - p_hi.astype(F32)).astype(BF16)
    ov = ov_ref[...]
    score = (_dot_nt(ov, p_hi) + _dot_nt(ov, p_lo))
    n_slc = score.shape[0]
    blk = lax.broadcasted_iota(jnp.int32, (n_slc, 1), 0)
    pos = i * tq + lax.broadcasted_iota(jnp.int32, (1, tq), 1)
    cur = pos >> int(math.log2(SLC_BLOCK))
    forced = (blk == 0) | ((cur - blk >= 0) & (cur - blk < SLC_LOCAL))
    score = jnp.where(forced, 1e9, jnp.where(blk > cur, -1e9, score))
    sub = lax.broadcasted_iota(jnp.int32, (8, 1), 0)
    sel_rows = []
    for g8 in range(n_slc // 8):
        sg = score[g8 * 8:(g8 + 1) * 8]
        rank = jnp.zeros(sg.shape, F32)
        for mp in range(n_slc):
            row = score[mp:mp + 1]
            if mp < g8 * 8:
                ahead = jnp.where(row >= sg, 1.0, 0.0)
            elif mp >= (g8 + 1) * 8:
                ahead = jnp.where(row > sg, 1.0, 0.0)
            else:
                ahead = jnp.where(sub > mp - g8 * 8, jnp.where(row >= sg, 1.0, 0.0), jnp.where(row > sg, 1.0, 0.0))
            rank = rank + ahead
        sel_rows.append(jnp.where(rank < float(min(SLC_TOPK, n_slc)), 0.0, NEG))
    pad = LANES // n_slc
    selb = jnp.concatenate(sel_rows * pad, axis=0)
    sb_ref[0, 0] = selb.T.astype(BF16)


def _cmp_attention(qb, kcd, vcd, table, ov_t, tq=256):
    b, s, _ = qb.shape
    n_col = kcd.shape[2]
    gw = R_B * HEAD_DIM
    return pl.pallas_call(
        functools.partial(_cmp_kernel, tq=tq),
        grid=(b, G_B, s // tq),
        in_specs=[pl.BlockSpec((1, tq, gw), lambda i, g, j: (i, j, g)),
                  pl.BlockSpec((1, 1, n_col, LANES), lambda i, g, j: (i, g, 0, 0)),
                  pl.BlockSpec((1, 1, n_col, LANES), lambda i, g, j: (i, g, 0, 0)),
                  pl.BlockSpec((R_B, tq, n_col), lambda i, g, j: (g, j, 0)),
                  pl.BlockSpec(ov_t.shape, lambda i, g, j: (0, 0))],
        out_specs=[pl.BlockSpec((1, tq, gw), lambda i, g, j: (i, j, g)),
                   pl.BlockSpec((1, 1, tq, LANES), lambda i, g, j: (i, g, j, 0))],
        out_shape=[jax.ShapeDtypeStruct((b, s, D_B), BF16),
                   jax.ShapeDtypeStruct((b, G_B, s, LANES), BF16)],
        compiler_params=_cparams(("parallel", "parallel", "parallel")),
        name="nsa_cmp_topk",
    )(qb, kcd, vcd, table, ov_t)


def _sel_kernel(q_ref, sb_ref, k_ref, v_ref, t_ref, o_ref, kaug, m_ref, l_ref, acc_ref, *, nblk):
    lo = _lo_mask()
    s_len = nblk * BLK
    n_diag = t_ref.shape[1] - 1
    rowblk = lax.broadcasted_iota(jnp.int32, (s_len, LANES), 0) >> int(math.log2(SLC_BLOCK))
    lane = lax.broadcasted_iota(jnp.int32, (s_len, LANES), 1)
    onehot = jnp.where(lane - HEAD_DIM == rowblk, 1.0, 0.0).astype(BF16)
    kaug[...] = jnp.where(lo, k_ref[0], onehot)
    m_ref[...] = jnp.full(m_ref.shape, NEG, F32)
    l_ref[...] = jnp.zeros(l_ref.shape, F32)
    acc_ref[...] = jnp.zeros(acc_ref.shape, F32)
    kb = 4
    qb = 2
    sub = kb // qb
    kt_rows, qt_rows = kb * BLK, qb * BLK

    def ktile(t, carry):
        k0 = pl.multiple_of(t * kt_rows, kt_rows)
        kt = kaug[pl.ds(k0, kt_rows), :]
        vt = v_ref[0, pl.ds(k0, kt_rows), :]

        def qstep(mp, c):
            for u in range(sub):
                r0 = pl.multiple_of((sub * mp + u) * qt_rows, qt_rows)
                sbf = sb_ref[0, 0, pl.ds(r0, qt_rows), :].astype(F32)
                rows = []
                for pr in range(R_B // 2):
                    q = q_ref[0, pl.ds(r0, qt_rows), pr * LANES:(pr + 1) * LANES].astype(F32)
                    rows.append(jnp.where(lo, q, sbf))
                    rows.append(jnp.where(lo, pltpu.roll(q, HEAD_DIM, 1), sbf))
                qst = jnp.concatenate(rows, axis=0).astype(BF16)
                s = _dot_nt(qst, kt)
                ps, alphas = [], []
                for h in range(R_B):
                    for a in range(qb):
                        n = qb * (sub * mp + u) + a
                        bias = jnp.concatenate(
                            [t_ref[h, jnp.clip(n - kb * t - c + 1, 0, n_diag)] for c in range(kb)], axis=1)
                        lo_r = h * qt_rows + a * BLK
                        st = pl.ds(r0 + a * BLK, BLK)
                        sh = s[lo_r:lo_r + BLK] + bias
                        m_prev = m_ref[h, st, :]
                        m_new = jnp.maximum(m_prev, jnp.max(sh, axis=-1, keepdims=True))
                        alpha = jnp.exp2(m_prev - m_new)
                        p = jnp.exp2(sh - jnp.concatenate([m_new] * kb, axis=1))
                        psum = p[:, :LANES]
                        for c in range(1, kb):
                            psum = psum + p[:, c * LANES:(c + 1) * LANES]
                        l_ref[h, st, :] = alpha * l_ref[h, st, :] + psum
                        m_ref[h, st, :] = m_new
                        ps.append(p.astype(BF16))
                        alphas.append(alpha)
                pv = jnp.dot(jnp.concatenate(ps, axis=0), vt, preferred_element_type=F32)
                for h in range(R_B):
                    for a in range(qb):
                        lo_r = h * qt_rows + a * BLK
                        st = pl.ds(r0 + a * BLK, BLK)
                        acc_ref[h, st, :] = alphas[qb * h + a] * acc_ref[h, st, :] + pv[lo_r:lo_r + BLK]
            return c

        lax.fori_loop(t, nblk // kb, qstep, 0)
        return carry

    lax.fori_loop(0, nblk // kb, ktile, 0)

    def finish(n, carry):
        st = pl.ds(pl.multiple_of(n * BLK, BLK), BLK)
        for pr in range(R_B // 2):
            inv = [1.0 / jnp.sum(l_ref[2 * pr + hh, st, :], axis=-1, keepdims=True) for hh in range(2)]
            o = jnp.where(lo, acc_ref[2 * pr, st, :] * inv[0], acc_ref[2 * pr + 1, st, :] * inv[1])
            o_ref[0, st, pr * LANES:(pr + 1) * LANES] = o.astype(o_ref.dtype)
        return carry

    lax.fori_loop(0, nblk, finish, 0)


def _sel_attention(qb, selb, ks_dup, vs_dup, table):
    b, s, _ = qb.shape
    gw = R_B * HEAD_DIM
    n_e = table.shape[1]
    one = pl.Buffered(1)
    return pl.pallas_call(
        functools.partial(_sel_kernel, nblk=s // BLK),
        grid=(b, G_B),
        in_specs=[pl.BlockSpec((1, s, gw), lambda i, g: (i, 0, g), pipeline_mode=one),
                  pl.BlockSpec((1, 1, s, LANES), lambda i, g: (i, g, 0, 0), pipeline_mode=one),
                  pl.BlockSpec((1, s, LANES), lambda i, g: (i, 0, g), pipeline_mode=one),
                  pl.BlockSpec((1, s, LANES), lambda i, g: (i, 0, g), pipeline_mode=one),
                  pl.BlockSpec((R_B, n_e, BLK, BLK), lambda i, g: (g, 0, 0, 0), pipeline_mode=one)],
        out_specs=pl.BlockSpec((1, s, gw), lambda i, g: (i, 0, g)),
        out_shape=jax.ShapeDtypeStruct((b, s, D_B), BF16),
        scratch_shapes=[pltpu.VMEM((s, LANES), BF16)] + [pltpu.VMEM((R_B, s, LANES), F32)] * 3,
        compiler_params=_cparams(("parallel", "parallel")),
        name="nsa_selected",
    )(qb, selb, ks_dup, vs_dup, table)


def _combine_kernel(x_ref, oa, za, oc, osl, ow, gb, zb, wo, ex, gate, lng, lnb, out_ref, *, alpha):
    f = lambda r: r[0].astype(F32)
    z = f(za)
    mix_a = f(oa) * (z * jax.nn.sigmoid(z))
    g = jax.nn.sigmoid(gb[0])
    g_hi = g.astype(BF16)
    g_lo = (g - g_hi.astype(F32)).astype(BF16)
    gx = (jnp.dot(g_hi, ex[...], preferred_element_type=F32)
          + jnp.dot(g_lo, ex[...], preferred_element_type=F32))
    ob = gx[:, :D_B] * f(oc) + gx[:, D_B:2 * D_B] * f(osl) + gx[:, 2 * D_B:] * f(ow)
    z = f(zb)
    mix_b = ob * (z * jax.nn.sigmoid(z))
    y = (jnp.dot(mix_a.astype(BF16), wo[:D_A], preferred_element_type=F32)
         + jnp.dot(mix_b.astype(BF16), wo[D_A:], preferred_element_type=F32))
    r = alpha * x_ref[0] + (1.0 + gate[0]) * y
    mu = jnp.mean(r, axis=-1, keepdims=True)
    rc = r - mu
    var = jnp.mean(rc * rc, axis=-1, keepdims=True)
    out_ref[0] = rc * lax.rsqrt(var + LN_EPS) * lng[...] + lnb[...]


def _combine(x, o_a, za, o_cmp, o_slc, o_win, gb, zb, w_out, expand, gate, ln_g, ln_b, alpha, tm=512):
    b, s, d = x.shape
    bs = lambda n: pl.BlockSpec((1, tm, n), lambda i, j: (i, j, 0))
    full = lambda a: pl.BlockSpec(a.shape, lambda i, j: (0,) * a.ndim)
    return pl.pallas_call(
        functools.partial(_combine_kernel, alpha=alpha),
        grid=(b, s // tm),
        in_specs=[bs(d)] + [bs(D_A)] * 2 + [bs(D_B)] * 3 + [bs(LANES), bs(D_B), full(w_out), full(expand),
                  pl.BlockSpec((1, 1, d), lambda i, j: (i, 0, 0)), full(ln_g), full(ln_b)],
        out_specs=bs(d),
        out_shape=jax.ShapeDtypeStruct((b, s, d), F32),
        compiler_params=_cparams(("parallel", "parallel")),
        name="merge_out_proj_ln",
    )(x, o_a, za, o_cmp, o_slc, o_win, gb, zb, w_out, expand, gate, ln_g, ln_b)


def _pack_w_in(w_in):
    n_gb = 3 * H_B
    main = w_in[..., :_C_ZB]
    gbw = w_in[..., _C_ZB:_C_ZB + n_gb]
    zbw = w_in[..., _C_ZB + n_gb:]
    pad = jnp.zeros(w_in.shape[:-1] + (LANES - n_gb,), w_in.dtype)
    return jnp.concatenate([main, zbw, gbw, pad], axis=-1).astype(BF16)


def _gate_expand():
    e = np.zeros((LANES, 3 * D_B), np.float32)
    for i in range(3):
        for h in range(H_B):
            e[i * H_B + h, i * D_B + h * HEAD_DIM:i * D_B + (h + 1) * HEAD_DIM] = 1.0
    return jnp.asarray(e, BF16)


def _overlap_t(n_col, n_slc):
    cs = np.arange(n_col)[None, :] * CMP_STRIDE
    ss = np.arange(n_slc)[:, None] * SLC_BLOCK
    return jnp.asarray(((cs < ss + SLC_BLOCK) & (cs + CMP_LEN > ss)).astype(np.float32), BF16)


def _layer(x, shift, scl, gate, w_in_p, w_out_b, pe_b, w1_b, w2d_b, ln_g, ln_b, tables, consts, alpha):
    b, s, d = x.shape
    (qa, ka, va, za, qb, kc_in, vc_in, ks, vs, kw, vw, zb, gb) = _project(x, shift, scl, w_in_p)

    o_a = _dilated_attention(qa, ka, va, tables["dilated"])

    n_chunk = s // CMP_STRIDE
    flat = lambda t: t.reshape(b, G_B, n_chunk, CMP_STRIDE * HEAD_DIM)
    kcd, vcd = _compress(flat(kc_in), flat(vc_in), w1_b, w2d_b, pe_b)
    o_cmp, selb = _cmp_attention(qb, kcd, vcd, tables["cmp"], consts["overlap_t"])
    o_slc = _sel_attention(qb, selb, ks, vs, tables["sel"])
    o_win = _win_attention(qb, kw, vw, tables["win"], n_prev=-(-(WIN - 1) // BLK))

    return _combine(x, o_a, za, o_cmp, o_slc, o_win, gb, zb, w_out_b, consts["expand"], gate,
                    ln_g, ln_b, alpha)


def kernel(x, c, w_in, w_out, cmp_pe, cmp_w1, cmp_w2, w_ada, b_ada, ln_g, ln_b, rel_bias):
    b, s, d = x.shape
    depth = w_in.shape[0]
    alpha = (2 * depth) ** 0.25
    n_pair = D_A // LANES

    bb = rel_bias.astype(F32) * LOG2E
    def dilated_table(window, dil):
        t = _band_table(bb[:, :H_A], 1, window // dil, dil).reshape(n_pair, 2 * BLK, 2 * BLK)
        first = t.at[..., :BLK].set(NEG)
        return jnp.stack([t, first], axis=1)

    n_prev_win = -(-(WIN - 1) // BLK)
    tables = {
        "dilated": [dilated_table(w, dl) for w, dl in DILATED],
        "win": _band_table(bb[:, H_A:], n_prev_win, WIN - 1, 1).reshape(G_B, R_B * BLK, (n_prev_win + 1) * BLK),
        "cmp": _cmp_table(bb[:, H_A:], s),
        "sel": _diag_table(bb[:, H_A:], 13),
    }
    consts = {"expand": _gate_expand(), "overlap_t": _overlap_t(s // CMP_STRIDE, s // SLC_BLOCK)}

    mod = _ada_mod(c, w_ada, b_ada)
    w_in_p = _pack_w_in(w_in)
    w_out_b = w_out.astype(BF16)
    w1_b = cmp_w1.astype(BF16)
    w2d_b = jnp.concatenate([cmp_w2, cmp_w2], axis=-1).astype(BF16)
    pe_b = jnp.broadcast_to(cmp_pe.reshape(depth, 2, 1, CMP_LEN * HEAD_DIM), (depth, 2, 8, CMP_LEN * HEAD_DIM)).astype(BF16)

    for l in range(depth):
        m = mod[l, :b]
        shift, scl, gate = (m[:, i * d:(i + 1) * d].reshape(b, 1, d) for i in range(3))
        x = _layer(x, shift, scl, gate, w_in_p[l], w_out_b[l], pe_b[l], w1_b[l], w2d_b[l],
                   ln_g[l].reshape(1, d), ln_b[l].reshape(1, d), tables, consts, alpha)
    return x
```

```python
import functools
import math

import numpy as np
import jax
import jax.numpy as jnp
from jax import lax
from jax.experimental import pallas as pl
from jax.experimental.pallas import tpu as pltpu

F32 = jnp.float32
BF16 = jnp.bfloat16

D_MODEL = 1024
HEAD_DIM = 64
H_A = 8
H_B = 8
G_B = 2
R_B = H_B // G_B
D_A = H_A * HEAD_DIM
D_B = H_B * HEAD_DIM
DILATED = ((128, 1), (512, 4), (2048, 16))
CMP_LEN = 32
CMP_STRIDE = 16
CMP_HIDDEN = 256
SLC_BLOCK = 64
SLC_TOPK = 16
SLC_LOCAL = 2
WIN = 512
N_BUCKETS = 32
MAX_DIST = 2048
LN_EPS = 1e-5
NEG = -1e30
LOG2E = 1.4426950408889634

LANES = 128
BLK = 128
VMEM_LIMIT = 56 * 1024 * 1024

_C_QA, _C_KA, _C_VA, _C_ZA, _C_QB = 0, 512, 1024, 1536, 2048
_C_KC, _C_VC, _C_KS, _C_VS, _C_KW, _C_VW = 2560, 2688, 2816, 2944, 3072, 3200
_C_ZB, _C_GB, _C_END = 3328, 3840, 3968


def _cparams(sem):
    return pltpu.CompilerParams(dimension_semantics=sem, vmem_limit_bytes=VMEM_LIMIT)


def _lo_mask():
    return lax.broadcasted_iota(jnp.int32, (1, LANES), 1) < HEAD_DIM


def _dot_nt(a, b):
    return lax.dot_general(a, b, (((1,), (1,)), ((), ())), preferred_element_type=F32)


def _bucket_np(dist):
    d = np.maximum(dist, 0)
    max_exact = N_BUCKETS // 2
    large = max_exact + (np.log(np.maximum(d, 1).astype(np.float32) / np.float32(max_exact))
                         / np.float32(math.log(MAX_DIST / max_exact))
                         * np.float32(N_BUCKETS - max_exact)).astype(np.int32)
    return np.where(d < max_exact, d, np.minimum(large, N_BUCKETS - 1)).astype(np.int32)


def _masked_bias(bias_by_bucket, dist, valid):
    t = jnp.moveaxis(bias_by_bucket[_bucket_np(dist)], -1, 0)
    return jnp.where(jnp.asarray(valid)[None], t, NEG)


def _toeplitz(h, rows, cols):
    ext = jnp.concatenate([h[..., rows - 1:], h[..., :1], h[..., :rows - 1]], axis=-1)
    lead = h.shape[:-1]
    flat = jnp.tile(ext, (1,) * len(lead) + (rows,))[..., :rows * (rows - 1 + cols)]
    return flat.reshape(*lead, rows, rows - 1 + cols)[..., :cols]


def _band_table(bias_by_bucket, n_prev, window, scale):
    w = (n_prev + 1) * BLK
    dist = n_prev * BLK - (np.arange(BLK - 1 + w) - (BLK - 1))
    return _toeplitz(_masked_bias(bias_by_bucket, dist * scale, (dist >= 0) & (dist <= window)), BLK, w)


def _diag_table(bias_by_bucket, n_diag):
    n_e = n_diag + 2
    dist = n_diag * BLK - (np.arange(BLK - 1 + n_e * BLK) - (BLK - 1))
    big = _toeplitz(_masked_bias(bias_by_bucket, dist, dist >= 0), BLK, n_e * BLK)
    return jnp.moveaxis(big.reshape(-1, BLK, n_e, BLK), 2, 1)[:, ::-1]


def _cmp_table(bias_by_bucket, seq):
    n_col = seq // CMP_STRIDE
    t = np.arange(2 * n_col - 1) - (n_col - 1)
    dist = -CMP_STRIDE * t[None, :] + np.arange(CMP_STRIDE)[:, None] - (CMP_LEN - 1)
    per_r = _toeplitz(_masked_bias(bias_by_bucket, dist, dist >= 0), n_col, n_col)
    return jnp.moveaxis(per_r, 1, 2).reshape(-1, seq, n_col)


def _ada_kernel(c_ref, w_ref, b_ref, o_ref):
    o_ref[0] = jnp.dot(c_ref[...], w_ref[0].astype(BF16), preferred_element_type=F32) + b_ref[0]


def _ada_mod(c, w_ada, b_ada):
    depth, d, d3 = w_ada.shape
    cp = jnp.zeros((8, d), BF16).at[:c.shape[0]].set(c.astype(BF16))
    nj = d3 // d
    return pl.pallas_call(
        _ada_kernel,
        grid=(depth, nj),
        in_specs=[pl.BlockSpec((8, d), lambda l, j: (0, 0)),
                  pl.BlockSpec((1, d, d), lambda l, j: (l, 0, j)),
                  pl.BlockSpec((1, 1, d), lambda l, j: (l, 0, j))],
        out_specs=pl.BlockSpec((1, 8, d), lambda l, j: (l, 0, j)),
        out_shape=jax.ShapeDtypeStruct((depth, 8, d3), F32),
        compiler_params=_cparams(("parallel", "parallel")),
        name="ada_mod",
    )(cp, w_ada, b_ada.reshape(depth, 1, d3))


def _proj_kernel(x_ref, sh_ref, sc_ref, w_ref, *refs, dils):
    n_lay = len(dils)
    qa, ka, va = refs[:n_lay], refs[n_lay:2 * n_lay], refs[2 * n_lay:3 * n_lay]
    za, qb, kc, vc, ks, vs, kw, vw, zb, gb, h_ref, xs = refs[3 * n_lay:]
    tm = h_ref.shape[0]
    h_ref[...] = (x_ref[0] * (1.0 + sc_ref[0]) + sh_ref[0]).astype(BF16)

    def mm(c0, n):
        return jnp.dot(h_ref[...], w_ref[:, c0:c0 + n], preferred_element_type=F32)

    def emit_regrouped(a, outs):
        outs[0][0] = a.astype(BF16)
        n_slab = a.shape[1] // LANES
        for c in range(n_slab):
            xs[0, c] = a[:, c * LANES:(c + 1) * LANES]
        d_prev = 1
        for lvl, (ref, dil) in enumerate(zip(outs[1:], dils[1:])):
            ratio, len_prev, len_new = dil // d_prev, tm // d_prev, tm // dil
            src, dst = xs.at[lvl % 2], xs.at[(lvl + 1) % 2]
            keep = lvl + 2 < len(dils)
            for rp in range(d_prev):
                for rs in range(ratio):
                    r = rp + d_prev * rs
                    for c in range(n_slab):
                        rows = src[c, pl.ds(rp * len_prev + rs, len_new, stride=ratio), :]
                        ref[0, r, :, c * LANES:(c + 1) * LANES] = rows.astype(BF16)
                        if keep:
                            dst[c, r * len_new:(r + 1) * len_new, :] = rows
            d_prev = dil

    qs = HEAD_DIM ** -0.5 * LOG2E
    emit_regrouped(mm(_C_QA, D_A) * qs, qa)
    emit_regrouped(mm(_C_KA, D_A), ka)
    emit_regrouped(mm(_C_VA, D_A), va)
    za[0] = mm(_C_ZA, D_A).astype(BF16)
    qb[0] = (mm(_C_QB, D_B) * qs).astype(BF16)
    lo = _lo_mask()
    for ref, c0 in ((kc, _C_KC), (vc, _C_VC)):
        a = mm(c0, LANES)
        r = pltpu.roll(a, HEAD_DIM, 1)
        ref[0, 0] = a[:, :HEAD_DIM].astype(BF16)
        ref[0, 1] = r[:, :HEAD_DIM].astype(BF16)
    for ref, c0 in ((ks, _C_KS), (vs, _C_VS), (kw, _C_KW), (vw, _C_VW)):
        a = mm(c0, LANES)
        r = pltpu.roll(a, HEAD_DIM, 1)
        ref[0, :, :LANES] = jnp.where(lo, a, r).astype(BF16)
        ref[0, :, LANES:] = jnp.where(lo, r, a).astype(BF16)
    zb[0] = mm(_C_ZB, D_B).astype(BF16)
    gb[0] = mm(_C_GB, LANES)


def _project(x, shift, scl, w, tm=512):
    b, s, d = x.shape
    dils = tuple(dl for _, dl in DILATED)
    bs = lambda n: pl.BlockSpec((1, tm, n), lambda i, j: (i, j, 0))
    gs = pl.BlockSpec((1, G_B, tm, HEAD_DIM), lambda i, j: (i, 0, j, 0))
    sd = lambda n, dt=BF16: jax.ShapeDtypeStruct((b, s, n), dt)
    gd = jax.ShapeDtypeStruct((b, G_B, s, HEAD_DIM), BF16)
    mod = pl.BlockSpec((1, 1, d), lambda i, j: (i, 0, 0))
    lay_specs = [bs(D_A)] + [pl.BlockSpec((1, dl, tm // dl, D_A), lambda i, j: (i, 0, j, 0)) for dl in dils[1:]]
    lay_shapes = [sd(D_A)] + [jax.ShapeDtypeStruct((b, dl, s // dl, D_A), BF16) for dl in dils[1:]]
    outs = pl.pallas_call(
        functools.partial(_proj_kernel, dils=dils),
        grid=(b, s // tm),
        in_specs=[pl.BlockSpec((1, tm, d), lambda i, j: (i, j, 0)), mod, mod,
                  pl.BlockSpec((d, _C_END), lambda i, j: (0, 0))],
        out_specs=lay_specs * 3 + [bs(D_A), bs(D_B), gs, gs,
                                   bs(2 * LANES), bs(2 * LANES), bs(2 * LANES), bs(2 * LANES), bs(D_B), bs(LANES)],
        out_shape=lay_shapes * 3 + [sd(D_A), sd(D_B), gd, gd,
                                    sd(2 * LANES), sd(2 * LANES), sd(2 * LANES), sd(2 * LANES), sd(D_B), sd(LANES, F32)],
        scratch_shapes=[pltpu.VMEM((tm, d), BF16), pltpu.VMEM((2, D_A // LANES, tm, LANES), F32)],
        compiler_params=_cparams(("parallel", "parallel")),
        name="in_proj",
    )(x, shift, scl, w)
    n = len(dils)
    return (outs[:n], outs[n:2 * n], outs[2 * n:3 * n], *outs[3 * n:])


def _dilated_kernel(*refs, seq, unroll):
    n_pat = len(DILATED)
    qkv = refs[:3 * n_pat]
    t_refs = refs[3 * n_pat:4 * n_pat]
    o_ref = refs[4 * n_pat]
    qlo, qhi, kp, vp, o_r, m_r, l_r, o_t, m_t, l_t, m_acc, n_acc, d_acc = refs[4 * n_pat + 1:]
    lo = _lo_mask()
    nblk = seq // BLK
    chunk = 512
    kp[:BLK, :] = jnp.zeros((BLK, LANES), BF16)
    vp[:BLK, :] = jnp.zeros((BLK, LANES), BF16)

    for pi, ((_, dil), t_ref) in enumerate(zip(DILATED, t_refs)):
        q_ref, k_ref, v_ref = qkv[pi], qkv[n_pat + pi], qkv[2 * n_pat + pi]
        rl = seq // dil
        nb = rl // BLK
        for r in range(dil):
            idx = (0,) if dil == 1 else (0, r)
            q = q_ref[idx]
            zero = jnp.zeros_like(q)
            qlo[r * rl:(r + 1) * rl, :] = jnp.where(lo, q, zero)
            qhi[r * rl:(r + 1) * rl, :] = jnp.where(lo, zero, q)
            kp[BLK + r * rl:BLK + (r + 1) * rl, :] = k_ref[idx]
            vp[BLK + r * rl:BLK + (r + 1) * rl, :] = v_ref[idx]
        o_dst, m_dst, l_dst = (n_acc, m_acc, d_acc) if dil == 1 else (o_r, m_r, l_r)

        def blocks(it, carry, t_ref=t_ref, nb=nb, o_dst=o_dst, m_dst=m_dst, l_dst=l_dst):
            for u in range(unroll):
                g = it * unroll + u
                first = jnp.where((g & (nb - 1)) == 0, 1, 0)
                q0 = pl.multiple_of(g * BLK, BLK)
                qs = jnp.concatenate([qlo[pl.ds(q0, BLK), :], qhi[pl.ds(q0, BLK), :]], axis=0)
                s = _dot_nt(qs, kp[pl.ds(q0, 2 * BLK), :]) + t_ref[0, first]
                m = jnp.max(s, axis=-1, keepdims=True)
                p = jnp.exp2(s - m)
                l = jnp.sum(p, axis=-1, keepdims=True)
                pv = jnp.dot(p.astype(BF16), vp[pl.ds(q0, 2 * BLK), :], preferred_element_type=F32)
                o_dst[pl.ds(q0, BLK), :] = jnp.where(lo, pv[:BLK], pv[BLK:])
                m_dst[pl.ds(q0, BLK), :] = jnp.where(lo, m[:BLK], m[BLK:])
                l_dst[pl.ds(q0, BLK), :] = jnp.where(lo, l[:BLK], l[BLK:])
            return carry

        lax.fori_loop(0, nblk // unroll, blocks, 0)
        if dil == 1:
            continue

        for r in range(dil):
            for src, dst in ((o_r, o_t), (m_r, m_t), (l_r, l_t)):
                dst[pl.ds(r, rl, stride=dil), :] = src[r * rl:(r + 1) * rl, :]

        def fold(c, carry, last=(pi == n_pat - 1)):
            rows = pl.ds(pl.multiple_of(c * chunk, chunk), chunk)
            m_old, m_new = m_acc[rows, :], m_t[rows, :]
            mx = jnp.maximum(m_old, m_new)
            a, b_ = jnp.exp2(m_old - mx), jnp.exp2(m_new - mx)
            n_new = a * n_acc[rows, :] + b_ * o_t[rows, :]
            d_new = a * d_acc[rows, :] + b_ * l_t[rows, :]
            if last:
                o_ref[0, rows, :] = (n_new * (1.0 / d_new)).astype(o_ref.dtype)
            else:
                m_acc[rows, :] = mx
                n_acc[rows, :] = n_new
                d_acc[rows, :] = d_new
            return carry

        lax.fori_loop(0, seq // chunk, fold, 0)


def _dilated_attention(qa, ka, va, tables, unroll=32):
    assert DILATED[0][1] == 1 and all(dl > 1 for _, dl in DILATED[1:])
    b, s, c = qa[0].shape
    xs = pl.BlockSpec((1, s, LANES), lambda i, j: (i, 0, j))
    lay = [xs] + [pl.BlockSpec((1, dl, s // dl, LANES), lambda i, j: (i, 0, 0, j)) for _, dl in DILATED[1:]]
    ts = pl.BlockSpec((1, 2, 2 * BLK, 2 * BLK), lambda i, j: (j, 0, 0, 0))
    big = lambda rows, dt: pltpu.VMEM((rows, LANES), dt)
    return pl.pallas_call(
        functools.partial(_dilated_kernel, seq=s, unroll=unroll),
        grid=(b, c // LANES),
        in_specs=lay * 3 + [ts] * len(DILATED),
        out_specs=xs,
        out_shape=jax.ShapeDtypeStruct((b, s, c), BF16),
        scratch_shapes=[big(s, BF16)] * 2 + [big(s + BLK, BF16)] * 2 + [big(s, F32)] * 9,
        compiler_params=_cparams(("parallel", "parallel")),
        name="dilated_attn",
    )(*qa, *ka, *va, *tables)


def _win_kernel(q_ref, k_ref, v_ref, t_ref, o_ref, qst, kaug, vp, *, seq, n_prev, unroll):
    lo = _lo_mask()
    lane = lax.broadcasted_iota(jnp.int32, (1, LANES), 1)
    flag = lane == HEAD_DIM
    pad = n_prev * BLK
    span = (n_prev + 1) * BLK
    negflag = jnp.where(flag, NEG, 0.0)
    for pr in range(R_B // 2):
        q = q_ref[0, :, pr * LANES:(pr + 1) * LANES].astype(F32)
        qst[2 * pr] = jnp.where(lo, q, negflag).astype(BF16)
        qst[2 * pr + 1] = jnp.where(lo, pltpu.roll(q, HEAD_DIM, 1), negflag).astype(BF16)
    k = k_ref[0]
    kaug[:pad, :] = jnp.broadcast_to(jnp.where(flag, 1.0, 0.0), (pad, LANES)).astype(BF16)
    kaug[pad:, :] = jnp.where(lo, k, jnp.zeros_like(k))
    vp[:pad, :] = jnp.zeros((pad, LANES), BF16)
    vp[pad:, :] = v_ref[0]

    def blocks(it, carry):
        for u in range(unroll):
            n = it * unroll + u
            r0 = pl.multiple_of(n * BLK, BLK)
            qs = jnp.concatenate([qst[h, pl.ds(r0, BLK), :] for h in range(R_B)], axis=0)
            s = _dot_nt(qs, kaug[pl.ds(r0, span), :]) + t_ref[0]
            m = jnp.max(s, axis=-1, keepdims=True)
            p = jnp.exp2(s - m)
            l = jnp.sum(p, axis=-1, keepdims=True)
            pv = jnp.dot(p.astype(BF16), vp[pl.ds(r0, span), :], preferred_element_type=F32) * (1.0 / l)
            for pr in range(R_B // 2):
                o = jnp.where(lo, pv[2 * pr * BLK:(2 * pr + 1) * BLK], pv[(2 * pr + 1) * BLK:(2 * pr + 2) * BLK])
                o_ref[0, pl.ds(r0, BLK), pr * LANES:(pr + 1) * LANES] = o.astype(o_ref.dtype)
        return carry

    lax.fori_loop(0, seq // BLK // unroll, blocks, 0)


def _win_attention(qb, kw_dup, vw_dup, table, n_prev, unroll=8):
    b, s, _ = qb.shape
    gw = R_B * HEAD_DIM
    w = table.shape[-1]
    return pl.pallas_call(
        functools.partial(_win_kernel, seq=s, n_prev=n_prev, unroll=unroll),
        grid=(b, G_B),
        in_specs=[pl.BlockSpec((1, s, gw), lambda i, g: (i, 0, g)),
                  pl.BlockSpec((1, s, LANES), lambda i, g: (i, 0, g)),
                  pl.BlockSpec((1, s, LANES), lambda i, g: (i, 0, g)),
                  pl.BlockSpec((1, R_B * BLK, w), lambda i, g: (g, 0, 0))],
        out_specs=pl.BlockSpec((1, s, gw), lambda i, g: (i, 0, g)),
        out_shape=jax.ShapeDtypeStruct((b, s, D_B), BF16),
        scratch_shapes=[pltpu.VMEM((R_B, s, LANES), BF16)] + [pltpu.VMEM((s + n_prev * BLK, LANES), BF16)] * 2,
        compiler_params=_cparams(("parallel", "parallel")),
        name="nsa_window",
    )(qb, kw_dup, vw_dup, table)


def _compress_kernel(kin, vin, w1, w2, pe, kout, vout):
    n_chunk = kin.shape[2]
    half = w1.shape[1] // 2
    rows = lax.broadcasted_iota(jnp.int32, (n_chunk, 1), 0)
    for t, (xin, out) in enumerate(((kin, kout), (vin, vout))):
        x = xin[0, 0]
        a = jnp.dot(x, w1[t, :half], preferred_element_type=F32)
        b = jnp.dot(x, w1[t, half:], preferred_element_type=F32)
        c = jnp.dot(pe[t], w1[t], preferred_element_type=F32)[0:1]
        hid = a + pltpu.roll(b, n_chunk - 1, 0) + c
        act = hid * jax.nn.sigmoid(hid)
        o = jnp.dot(act.astype(BF16), w2[t], preferred_element_type=F32)
        out[0, 0] = jnp.where(rows < n_chunk - 1, o, 0.0).astype(BF16)


def _compress(kc_in, vc_in, w1, w2d, pe):
    b, g, n_chunk, f = kc_in.shape
    xs = pl.BlockSpec((1, 1, n_chunk, f), lambda i, j: (i, j, 0, 0))
    os_ = pl.BlockSpec((1, 1, n_chunk, LANES), lambda i, j: (i, j, 0, 0))
    full = lambda a: pl.BlockSpec(a.shape, lambda i, j: (0,) * a.ndim)
    od = jax.ShapeDtypeStruct((b, g, n_chunk, LANES), BF16)
    return pl.pallas_call(
        _compress_kernel,
        grid=(b, g),
        in_specs=[xs, xs, full(w1), full(w2d), full(pe)],
        out_specs=[os_, os_],
        out_shape=[od, od],
        compiler_params=_cparams(("parallel", "parallel")),
        name="nsa_compress",
    )(kc_in, vc_in, w1, w2d, pe)


def _cmp_kernel(q_ref, kc_ref, vc_ref, t_ref, ov_ref, o_ref, sb_ref, *, tq):
    i = pl.program_id(2)
    lo = _lo_mask()
    kcd = kc_ref[0, 0]
    vcd = vc_ref[0, 0]
    zero = jnp.zeros_like(vcd)
    vlo = jnp.where(lo, vcd, zero)
    vhi = jnp.where(lo, zero, vcd)
    psum = None
    for pr in range(R_B // 2):
        q = q_ref[0, :, pr * LANES:(pr + 1) * LANES]
        qz = jnp.zeros_like(q)
        acc = None
        for hh, (qm, vm) in enumerate(((jnp.where(lo, q, qz), vlo), (jnp.where(lo, qz, q), vhi))):
            s = _dot_nt(qm, kcd) + t_ref[2 * pr + hh]
            m = jnp.max(s, axis=-1, keepdims=True)
            e = jnp.exp2(s - m)
            l = jnp.sum(e, axis=-1, keepdims=True)
            p = e * jnp.where(m > 0.5 * NEG, 1.0 / l, 0.0)
            psum = p if psum is None else psum + p
            pv = jnp.dot(p.astype(BF16), vm, preferred_element_type=F32)
            acc = pv if acc is None else acc + pv
        o_ref[0, :, pr * LANES:(pr + 1) * LANES] = acc.astype(o_ref.dtype)

    p_hi = psum.astype(BF16)
    p_lo = (psum - p_hi.astype(F32)).astype(BF16)
    ov = ov_ref[...]
    score = (_dot_nt(ov, p_hi) + _dot_nt(ov, p_lo))
    n_slc = score.shape[0]
    blk = lax.broadcasted_iota(jnp.int32, (n_slc, 1), 0)
    pos = i * tq + lax.broadcasted_iota(jnp.int32, (1, tq), 1)
    cur = pos >> int(math.log2(SLC_BLOCK))
    forced = (blk == 0) | ((cur - blk >= 0) & (cur - blk < SLC_LOCAL))
    score = jnp.where(forced, 1e9, jnp.where(blk > cur, -1e9, score))
    sub = lax.broadcasted_iota(jnp.int32, (8, 1), 0)
    sel_rows = []
    for g8 in range(n_slc // 8):
        sg = score[g8 * 8:(g8 + 1) * 8]
        rank = jnp.zeros(sg.shape, F32)
        for mp in range(n_slc):
            row = score[mp:mp + 1]
            if mp < g8 * 8:
                ahead = jnp.where(row >= sg, 1.0, 0.0)
            elif mp >= (g8 + 1) * 8:
                ahead = jnp.where(row > sg, 1.0, 0.0)
            else:
                ahead = jnp.where(sub > mp - g8 * 8, jnp.where(row >= sg, 1.0, 0.0), jnp.where(row > sg, 1.0, 0.0))
            rank = rank + ahead
        sel_rows.append(jnp.where(rank < float(min(SLC_TOPK, n_slc)), 0.0, NEG))
    pad = LANES // n_slc
    selb = jnp.concatenate(sel_rows * pad, axis=0)
    sb_ref[0, 0] = selb.T.astype(BF16)


def _cmp_attention(qb, kcd, vcd, table, ov_t, tq=256):
    b, s, _ = qb.shape
    n_col = kcd.shape[2]
    gw = R_B * HEAD_DIM
    return pl.pallas_call(
        functools.partial(_cmp_kernel, tq=tq),
        grid=(b, G_B, s // tq),
        in_specs=[pl.BlockSpec((1, tq, gw), lambda i, g, j: (i, j, g)),
                  pl.BlockSpec((1, 1, n_col, LANES), lambda i, g, j: (i, g, 0, 0)),
                  pl.BlockSpec((1, 1, n_col, LANES), lambda i, g, j: (i, g, 0, 0)),
                  pl.BlockSpec((R_B, tq, n_col), lambda i, g, j: (g, j, 0)),
                  pl.BlockSpec(ov_t.shape, lambda i, g, j: (0, 0))],
        out_specs=[pl.BlockSpec((1, tq, gw), lambda i, g, j: (i, j, g)),
                   pl.BlockSpec((1, 1, tq, LANES), lambda i, g, j: (i, g, j, 0))],
        out_shape=[jax.ShapeDtypeStruct((b, s, D_B), BF16),
                   jax.ShapeDtypeStruct((b, G_B, s, LANES), BF16)],
        compiler_params=_cparams(("parallel", "parallel", "parallel")),
        name="nsa_cmp_topk",
    )(qb, kcd, vcd, table, ov_t)


def _sel_kernel(q_ref, sb_ref, k_ref, v_ref, t_ref, o_ref, kaug, m_ref, l_ref, acc_ref, *, nblk):
    lo = _lo_mask()
    s_len = nblk * BLK
    n_diag = t_ref.shape[1] - 1
    rowblk = lax.broadcasted_iota(jnp.int32, (s_len, LANES), 0) >> int(math.log2(SLC_BLOCK))
    lane = lax.broadcasted_iota(jnp.int32, (s_len, LANES), 1)
    onehot = jnp.where(lane - HEAD_DIM == rowblk, 1.0, 0.0).astype(BF16)
    kaug[...] = jnp.where(lo, k_ref[0], onehot)
    m_ref[...] = jnp.full(m_ref.shape, NEG, F32)
    l_ref[...] = jnp.zeros(l_ref.shape, F32)
    acc_ref[...] = jnp.zeros(acc_ref.shape, F32)
    kb = 4
    qb = 2
    sub = kb // qb
    kt_rows, qt_rows = kb * BLK, qb * BLK

    def ktile(t, carry):
        k0 = pl.multiple_of(t * kt_rows, kt_rows)
        kt = kaug[pl.ds(k0, kt_rows), :]
        vt = v_ref[0, pl.ds(k0, kt_rows), :]

        def qstep(mp, c):
            scores = []
            for u in range(sub):
                r0 = pl.multiple_of((sub * mp + u) * qt_rows, qt_rows)
                sbf = sb_ref[0, 0, pl.ds(r0, qt_rows), :].astype(F32)
                rows = []
                for pr in range(R_B // 2):
                    q = q_ref[0, pl.ds(r0, qt_rows), pr * LANES:(pr + 1) * LANES].astype(F32)
                    rows.append(jnp.where(lo, q, sbf))
                    rows.append(jnp.where(lo, pltpu.roll(q, HEAD_DIM, 1), sbf))
                qst = jnp.concatenate(rows, axis=0).astype(BF16)
                scores.append(_dot_nt(qst, kt))
            for u in range(sub):
                r0 = pl.multiple_of((sub * mp + u) * qt_rows, qt_rows)
                s = scores[u]
                ps, alphas = [], []
                for h in range(R_B):
                    for a in range(qb):
                        n = qb * (sub * mp + u) + a
                        bias = jnp.concatenate(
                            [t_ref[h, jnp.clip(n - kb * t - c + 1, 0, n_diag)] for c in range(kb)], axis=1)
                        lo_r = h * qt_rows + a * BLK
                        st = pl.ds(r0 + a * BLK, BLK)
                        sh = s[lo_r:lo_r + BLK] + bias
                        m_prev = m_ref[h, st, :]
                        m_new = jnp.maximum(m_prev, jnp.max(sh, axis=-1, keepdims=True))
                        alpha = jnp.exp2(m_prev - m_new)
                        p = jnp.exp2(sh - jnp.concatenate([m_new] * kb, axis=1))
                        psum = p[:, :LANES]
                        for c in range(1, kb):
                            psum = psum + p[:, c * LANES:(c + 1) * LANES]
                        l_ref[h, st, :] = alpha * l_ref[h, st, :] + psum
                        m_ref[h, st, :] = m_new
                        ps.append(p.astype(BF16))
                        alphas.append(alpha)
                pv = jnp.dot(jnp.concatenate(ps, axis=0), vt, preferred_element_type=F32)
                for h in range(R_B):
                    for a in range(qb):
                        lo_r = h * qt_rows + a * BLK
                        st = pl.ds(r0 + a * BLK, BLK)
                        acc_ref[h, st, :] = alphas[qb * h + a] * acc_ref[h, st, :] + pv[lo_r:lo_r + BLK]
            return c

        lax.fori_loop(t, nblk // kb, qstep, 0)
        return carry

    lax.fori_loop(0, nblk // kb, ktile, 0)

    def finish(n, carry):
        st = pl.ds(pl.multiple_of(n * BLK, BLK), BLK)
        for pr in range(R_B // 2):
            inv = [1.0 / jnp.sum(l_ref[2 * pr + hh, st, :], axis=-1, keepdims=True) for hh in range(2)]
            o = jnp.where(lo, acc_ref[2 * pr, st, :] * inv[0], acc_ref[2 * pr + 1, st, :] * inv[1])
            o_ref[0, st, pr * LANES:(pr + 1) * LANES] = o.astype(o_ref.dtype)
        return carry

    lax.fori_loop(0, nblk, finish, 0)


def _sel_attention(qb, selb, ks_dup, vs_dup, table):
    b, s, _ = qb.shape
    gw = R_B * HEAD_DIM
    n_e = table.shape[1]
    one = pl.Buffered(1)
    return pl.pallas_call(
        functools.partial(_sel_kernel, nblk=s // BLK),
        grid=(b, G_B),
        in_specs=[pl.BlockSpec((1, s, gw), lambda i, g: (i, 0, g), pipeline_mode=one),
                  pl.BlockSpec((1, 1, s, LANES), lambda i, g: (i, g, 0, 0), pipeline_mode=one),
                  pl.BlockSpec((1, s, LANES), lambda i, g: (i, 0, g), pipeline_mode=one),
                  pl.BlockSpec((1, s, LANES), lambda i, g: (i, 0, g), pipeline_mode=one),
                  pl.BlockSpec((R_B, n_e, BLK, BLK), lambda i, g: (g, 0, 0, 0), pipeline_mode=one)],
        out_specs=pl.BlockSpec((1, s, gw), lambda i, g: (i, 0, g)),
        out_shape=jax.ShapeDtypeStruct((b, s, D_B), BF16),
        scratch_shapes=[pltpu.VMEM((s, LANES), BF16)] + [pltpu.VMEM((R_B, s, LANES), F32)] * 3,
        compiler_params=_cparams(("parallel", "parallel")),
        name="nsa_selected",
    )(qb, selb, ks_dup, vs_dup, table)


def _combine_kernel(x_ref, oa, za, oc, osl, ow, gb, zb, wo, ex, gate, lng, lnb, out_ref, *, alpha):
    f = lambda r: r[0].astype(F32)
    z = f(za)
    mix_a = f(oa) * (z * jax.nn.sigmoid(z))
    g = jax.nn.sigmoid(gb[0])
    g_hi = g.astype(BF16)
    g_lo = (g - g_hi.astype(F32)).astype(BF16)
    gx = (jnp.dot(g_hi, ex[...], preferred_element_type=F32)
          + jnp.dot(g_lo, ex[...], preferred_element_type=F32))
    ob = gx[:, :D_B] * f(oc) + gx[:, D_B:2 * D_B] * f(osl) + gx[:, 2 * D_B:] * f(ow)
    z = f(zb)
    mix_b = ob * (z * jax.nn.sigmoid(z))
    y = (jnp.dot(mix_a.astype(BF16), wo[:D_A], preferred_element_type=F32)
         + jnp.dot(mix_b.astype(BF16), wo[D_A:], preferred_element_type=F32))
    r = alpha * x_ref[0] + (1.0 + gate[0]) * y
    mu = jnp.mean(r, axis=-1, keepdims=True)
    rc = r - mu
    var = jnp.mean(rc * rc, axis=-1, keepdims=True)
    out_ref[0] = rc * lax.rsqrt(var + LN_EPS) * lng[...] + lnb[...]


def _combine(x, o_a, za, o_cmp, o_slc, o_win, gb, zb, w_out, expand, gate, ln_g, ln_b, alpha, tm=512):
    b, s, d = x.shape
    bs = lambda n: pl.BlockSpec((1, tm, n), lambda i, j: (i, j, 0))
    full = lambda a: pl.BlockSpec(a.shape, lambda i, j: (0,) * a.ndim)
    return pl.pallas_call(
        functools.partial(_combine_kernel, alpha=alpha),
        grid=(b, s // tm),
        in_specs=[bs(d)] + [bs(D_A)] * 2 + [bs(D_B)] * 3 + [bs(LANES), bs(D_B), full(w_out), full(expand),
                  pl.BlockSpec((1, 1, d), lambda i, j: (i, 0, 0)), full(ln_g), full(ln_b)],
        out_specs=bs(d),
        out_shape=jax.ShapeDtypeStruct((b, s, d), F32),
        compiler_params=_cparams(("parallel", "parallel")),
        name="merge_out_proj_ln",
    )(x, o_a, za, o_cmp, o_slc, o_win, gb, zb, w_out, expand, gate, ln_g, ln_b)


def _pack_w_in(w_in):
    n_gb = 3 * H_B
    main = w_in[..., :_C_ZB]
    gbw = w_in[..., _C_ZB:_C_ZB + n_gb]
    zbw = w_in[..., _C_ZB + n_gb:]
    pad = jnp.zeros(w_in.shape[:-1] + (LANES - n_gb,), w_in.dtype)
    return jnp.concatenate([main, zbw, gbw, pad], axis=-1).astype(BF16)


def _gate_expand():
    e = np.zeros((LANES, 3 * D_B), np.float32)
    for i in range(3):
        for h in range(H_B):
            e[i * H_B + h, i * D_B + h * HEAD_DIM:i * D_B + (h + 1) * HEAD_DIM] = 1.0
    return jnp.asarray(e, BF16)


def _overlap_t(n_col, n_slc):
    cs = np.arange(n_col)[None, :] * CMP_STRIDE
    ss = np.arange(n_slc)[:, None] * SLC_BLOCK
    return jnp.asarray(((cs < ss + SLC_BLOCK) & (cs + CMP_LEN > ss)).astype(np.float32), BF16)


def _layer(x, shift, scl, gate, w_in_p, w_out_b, pe_b, w1_b, w2d_b, ln_g, ln_b, tables, consts, alpha):
    b, s, d = x.shape
    (qa, ka, va, za, qb, kc_in, vc_in, ks, vs, kw, vw, zb, gb) = _project(x, shift, scl, w_in_p)

    o_a = _dilated_attention(qa, ka, va, tables["dilated"])

    n_chunk = s // CMP_STRIDE
    flat = lambda t: t.reshape(b, G_B, n_chunk, CMP_STRIDE * HEAD_DIM)
    kcd, vcd = _compress(flat(kc_in), flat(vc_in), w1_b, w2d_b, pe_b)
    o_cmp, selb = _cmp_attention(qb, kcd, vcd, tables["cmp"], consts["overlap_t"])
    o_slc = _sel_attention(qb, selb, ks, vs, tables["sel"])
    o_win = _win_attention(qb, kw, vw, tables["win"], n_prev=-(-(WIN - 1) // BLK))

    return _combine(x, o_a, za, o_cmp, o_slc, o_win, gb, zb, w_out_b, consts["expand"], gate,
                    ln_g, ln_b, alpha)


def kernel(x, c, w_in, w_out, cmp_pe, cmp_w1, cmp_w2, w_ada, b_ada, ln_g, ln_b, rel_bias):
    b, s, d = x.shape
    depth = w_in.shape[0]
    alpha = (2 * depth) ** 0.25
    n_pair = D_A // LANES

    bb = rel_bias.astype(F32) * LOG2E
    def dilated_table(window, dil):
        t = _band_table(bb[:, :H_A], 1, window // dil, dil).reshape(n_pair, 2 * BLK, 2 * BLK)
        first = t.at[..., :BLK].set(NEG)
        return jnp.stack([t, first], axis=1)

    n_prev_win = -(-(WIN - 1) // BLK)
    tables = {
        "dilated": [dilated_table(w, dl) for w, dl in DILATED],
        "win": _band_table(bb[:, H_A:], n_prev_win, WIN - 1, 1).reshape(G_B, R_B * BLK, (n_prev_win + 1) * BLK),
        "cmp": _cmp_table(bb[:, H_A:], s),
        "sel": _diag_table(bb[:, H_A:], 13),
    }
    consts = {"expand": _gate_expand(), "overlap_t": _overlap_t(s // CMP_STRIDE, s // SLC_BLOCK)}

    mod = _ada_mod(c, w_ada, b_ada)
    w_in_p = _pack_w_in(w_in)
    w_out_b = w_out.astype(BF16)
    w1_b = cmp_w1.astype(BF16)
    w2d_b = jnp.concatenate([cmp_w2, cmp_w2], axis=-1).astype(BF16)
    pe_b = jnp.broadcast_to(cmp_pe.reshape(depth, 2, 1, CMP_LEN * HEAD_DIM), (depth, 2, 8, CMP_LEN * HEAD_DIM)).astype(BF16)

    for l in range(depth):
        m = mod[l, :b]
        shift, scl, gate = (m[:, i * d:(i + 1) * d].reshape(b, 1, d) for i in range(3))
        x = _layer(x, shift, scl, gate, w_in_p[l], w_out_b[l], pe_b[l], w1_b[l], w2d_b[l],
                   ln_g[l].reshape(1, d), ln_b[l].reshape(1, d), tables, consts, alpha)
    return x
```

```python
import functools
import math

import numpy as np
import jax
import jax.numpy as jnp
from jax import lax
from jax.experimental import pallas as pl
from jax.experimental.pallas import tpu as pltpu

F32 = jnp.float32
BF16 = jnp.bfloat16

D_MODEL = 1024
HEAD_DIM = 64
H_A = 8
H_B = 8
G_B = 2
R_B = H_B // G_B
D_A = H_A * HEAD_DIM
D_B = H_B * HEAD_DIM
DILATED = ((128, 1), (512, 4), (2048, 16))
CMP_LEN = 32
CMP_STRIDE = 16
CMP_HIDDEN = 256
SLC_BLOCK = 64
SLC_TOPK = 16
SLC_LOCAL = 2
WIN = 512
N_BUCKETS = 32
MAX_DIST = 2048
LN_EPS = 1e-5
NEG = -1e30
LOG2E = 1.4426950408889634

LANES = 128
BLK = 128
VMEM_LIMIT = 56 * 1024 * 1024

_C_QA, _C_KA, _C_VA, _C_ZA, _C_QB = 0, 512, 1024, 1536, 2048
_C_KC, _C_VC, _C_KS, _C_VS, _C_KW, _C_VW = 2560, 2688, 2816, 2944, 3072, 3200
_C_ZB, _C_GB, _C_END = 3328, 3840, 3968


def _cparams(sem):
    return pltpu.CompilerParams(dimension_semantics=sem, vmem_limit_bytes=VMEM_LIMIT)


def _lo_mask():
    return lax.broadcasted_iota(jnp.int32, (1, LANES), 1) < HEAD_DIM


def _dot_nt(a, b):
    return lax.dot_general(a, b, (((1,), (1,)), ((), ())), preferred_element_type=F32)


def _bucket_np(dist):
    d = np.maximum(dist, 0)
    max_exact = N_BUCKETS // 2
    large = max_exact + (np.log(np.maximum(d, 1).astype(np.float32) / np.float32(max_exact))
                         / np.float32(math.log(MAX_DIST / max_exact))
                         * np.float32(N_BUCKETS - max_exact)).astype(np.int32)
    return np.where(d < max_exact, d, np.minimum(large, N_BUCKETS - 1)).astype(np.int32)


def _masked_bias(bias_by_bucket, dist, valid):
    onehot = np.eye(N_BUCKETS, dtype=np.float32)[_bucket_np(dist)]
    t = jnp.einsum("...k,kh->h...", onehot, bias_by_bucket, precision=lax.Precision.HIGHEST)
    return jnp.where(jnp.asarray(valid)[None], t, NEG)


def _toeplitz_kernel(v_ref, o_ref):
    cols = o_ref.shape[-1]
    for e in range(o_ref.shape[1]):
        x = jnp.broadcast_to(v_ref[0, e:e + 1, :], (BLK, v_ref.shape[-1]))
        o_ref[0, e] = pltpu.roll(x, 0, 1, stride=1, stride_axis=0)[:, :cols]


def _toeplitz(h, cols):
    n_h, n_e, n = h.shape
    width = 1 << (n - 1).bit_length()
    vec = jnp.concatenate([h[..., BLK - 1:], jnp.zeros((n_h, n_e, width - n), F32), h[..., :BLK - 1]], axis=-1)
    return pl.pallas_call(
        _toeplitz_kernel,
        grid=(n_h,),
        in_specs=[pl.BlockSpec((1, n_e, width), lambda a: (a, 0, 0))],
        out_specs=pl.BlockSpec((1, n_e, BLK, cols), lambda a: (a, 0, 0, 0)),
        out_shape=jax.ShapeDtypeStruct((n_h, n_e, BLK, cols), F32),
        compiler_params=_cparams(("parallel",)),
        name="toeplitz_table",
    )(vec)


def _band_table(bias_by_bucket, n_prev, window, scale):
    w = (n_prev + 1) * BLK
    dist = n_prev * BLK - (np.arange(BLK - 1 + w) - (BLK - 1))
    h = _masked_bias(bias_by_bucket, dist * scale, (dist >= 0) & (dist <= window))
    return _toeplitz(h[:, None], w)[:, 0]


def _diag_table(bias_by_bucket, n_diag):
    delta = np.arange(-1, n_diag + 1)[:, None]
    dist = delta * BLK - (np.arange(2 * BLK - 1)[None, :] - (BLK - 1))
    return _toeplitz(_masked_bias(bias_by_bucket, dist, dist >= 0), BLK)


def _cmp_table_kernel(base_ref, o_ref, *, rows):
    i = pl.program_id(1)
    n_col = o_ref.shape[-1]
    width = base_ref.shape[-1]
    base = base_ref[0]
    for j in range(rows // CMP_STRIDE):
        chunk = i * (rows // CMP_STRIDE) + j
        shift = (chunk + width - (n_col - 1)) % width
        o_ref[0, j * CMP_STRIDE:(j + 1) * CMP_STRIDE, :] = pltpu.roll(base, shift, 1)[:, :n_col]


def _cmp_table(bias_by_bucket, seq, rows=1024):
    n_col = seq // CMP_STRIDE
    t = np.arange(2 * n_col) - (n_col - 1)
    dist = -CMP_STRIDE * t[None, :] + np.arange(CMP_STRIDE)[:, None] - (CMP_LEN - 1)
    base = _masked_bias(bias_by_bucket, dist, dist >= 0)
    h = base.shape[0]
    return pl.pallas_call(
        functools.partial(_cmp_table_kernel, rows=rows),
        grid=(h, seq // rows),
        in_specs=[pl.BlockSpec((1, CMP_STRIDE, 2 * n_col), lambda a, i: (a, 0, 0))],
        out_specs=pl.BlockSpec((1, rows, n_col), lambda a, i: (a, i, 0)),
        out_shape=jax.ShapeDtypeStruct((h, seq, n_col), F32),
        compiler_params=_cparams(("parallel", "parallel")),
        name="cmp_bias_table",
    )(base)


def _ada_kernel(c_ref, w_ref, b_ref, o_ref):
    o_ref[0] = jnp.dot(c_ref[...], w_ref[0].astype(BF16), preferred_element_type=F32) + b_ref[0]


def _ada_mod(c, w_ada, b_ada):
    depth, d, d3 = w_ada.shape
    cp = jnp.zeros((8, d), BF16).at[:c.shape[0]].set(c.astype(BF16))
    nj = d3 // d
    return pl.pallas_call(
        _ada_kernel,
        grid=(depth, nj),
        in_specs=[pl.BlockSpec((8, d), lambda l, j: (0, 0)),
                  pl.BlockSpec((1, d, d), lambda l, j: (l, 0, j)),
                  pl.BlockSpec((1, 1, d), lambda l, j: (l, 0, j))],
        out_specs=pl.BlockSpec((1, 8, d), lambda l, j: (l, 0, j)),
        out_shape=jax.ShapeDtypeStruct((depth, 8, d3), F32),
        compiler_params=_cparams(("parallel", "parallel")),
        name="ada_mod",
    )(cp, w_ada, b_ada.reshape(depth, 1, d3))


def _proj_kernel(x_ref, sh_ref, sc_ref, w_ref, *refs, dils):
    n_lay = len(dils)
    qa, ka, va = refs[:n_lay], refs[n_lay:2 * n_lay], refs[2 * n_lay:3 * n_lay]
    za, qb, kc, vc, ks, vs, kw, vw, zb, gb, h_ref, xs = refs[3 * n_lay:]
    tm = h_ref.shape[0]
    h_ref[...] = (x_ref[0] * (1.0 + sc_ref[0]) + sh_ref[0]).astype(BF16)

    def mm(c0, n):
        return jnp.dot(h_ref[...], w_ref[:, c0:c0 + n], preferred_element_type=F32)

    def emit_regrouped(a, outs):
        outs[0][0] = a.astype(BF16)
        n_slab = a.shape[1] // LANES
        for c in range(n_slab):
            xs[0, c] = a[:, c * LANES:(c + 1) * LANES]
        d_prev = 1
        for lvl, (ref, dil) in enumerate(zip(outs[1:], dils[1:])):
            ratio, len_prev, len_new = dil // d_prev, tm // d_prev, tm // dil
            src, dst = xs.at[lvl % 2], xs.at[(lvl + 1) % 2]
            keep = lvl + 2 < len(dils)
            for rp in range(d_prev):
                for rs in range(ratio):
                    r = rp + d_prev * rs
                    for c in range(n_slab):
                        rows = src[c, pl.ds(rp * len_prev + rs, len_new, stride=ratio), :]
                        ref[0, r, :, c * LANES:(c + 1) * LANES] = rows.astype(BF16)
                        if keep:
                            dst[c, r * len_new:(r + 1) * len_new, :] = rows
            d_prev = dil

    qs = HEAD_DIM ** -0.5 * LOG2E
    emit_regrouped(mm(_C_QA, D_A) * qs, qa)
    emit_regrouped(mm(_C_KA, D_A), ka)
    emit_regrouped(mm(_C_VA, D_A), va)
    za[0] = mm(_C_ZA, D_A).astype(BF16)
    qb[0] = (mm(_C_QB, D_B) * qs).astype(BF16)
    lo = _lo_mask()
    for ref, c0 in ((kc, _C_KC), (vc, _C_VC)):
        xs[0, 0] = mm(c0, LANES)
        for t in range(0, CMP_STRIDE, 2):
            x0 = xs[0, 0, pl.ds(t, tm // CMP_STRIDE, stride=CMP_STRIDE), :]
            x1 = xs[0, 0, pl.ds(t + 1, tm // CMP_STRIDE, stride=CMP_STRIDE), :]
            cols = slice(t * HEAD_DIM, (t + 2) * HEAD_DIM)
            ref[0, 0, :, cols] = jnp.where(lo, x0, pltpu.roll(x1, HEAD_DIM, 1)).astype(BF16)
            ref[0, 1, :, cols] = jnp.where(lo, pltpu.roll(x0, HEAD_DIM, 1), x1).astype(BF16)
    for ref, c0 in ((ks, _C_KS), (vs, _C_VS), (kw, _C_KW), (vw, _C_VW)):
        a = mm(c0, LANES)
        r = pltpu.roll(a, HEAD_DIM, 1)
        ref[0, :, :LANES] = jnp.where(lo, a, r).astype(BF16)
        ref[0, :, LANES:] = jnp.where(lo, r, a).astype(BF16)
    zb[0] = mm(_C_ZB, D_B).astype(BF16)
    gb[0] = mm(_C_GB, LANES)


def _project(x, shift, scl, w, tm=512):
    b, s, d = x.shape
    dils = tuple(dl for _, dl in DILATED)
    bs = lambda n: pl.BlockSpec((1, tm, n), lambda i, j: (i, j, 0))
    gs = pl.BlockSpec((1, G_B, tm // CMP_STRIDE, CMP_STRIDE * HEAD_DIM), lambda i, j: (i, 0, j, 0))
    sd = lambda n, dt=BF16: jax.ShapeDtypeStruct((b, s, n), dt)
    gd = jax.ShapeDtypeStruct((b, G_B, s // CMP_STRIDE, CMP_STRIDE * HEAD_DIM), BF16)
    mod = pl.BlockSpec((1, 1, d), lambda i, j: (i, 0, 0))
    lay_specs = [bs(D_A)] + [pl.BlockSpec((1, dl, tm // dl, D_A), lambda i, j: (i, 0, j, 0)) for dl in dils[1:]]
    lay_shapes = [sd(D_A)] + [jax.ShapeDtypeStruct((b, dl, s // dl, D_A), BF16) for dl in dils[1:]]
    outs = pl.pallas_call(
        functools.partial(_proj_kernel, dils=dils),
        grid=(b, s // tm),
        in_specs=[pl.BlockSpec((1, tm, d), lambda i, j: (i, j, 0)), mod, mod,
                  pl.BlockSpec((d, _C_END), lambda i, j: (0, 0))],
        out_specs=lay_specs * 3 + [bs(D_A), bs(D_B), gs, gs,
                                   bs(2 * LANES), bs(2 * LANES), bs(2 * LANES), bs(2 * LANES), bs(D_B), bs(LANES)],
        out_shape=lay_shapes * 3 + [sd(D_A), sd(D_B), gd, gd,
                                    sd(2 * LANES), sd(2 * LANES), sd(2 * LANES), sd(2 * LANES), sd(D_B), sd(LANES, F32)],
        scratch_shapes=[pltpu.VMEM((tm, d), BF16), pltpu.VMEM((2, D_A // LANES, tm, LANES), F32)],
        compiler_params=_cparams(("parallel", "parallel")),
        name="in_proj",
    )(x, shift, scl, w)
    n = len(dils)
    return (outs[:n], outs[n:2 * n], outs[2 * n:3 * n], *outs[3 * n:])


def _dilated_kernel(*refs, seq, unroll):
    n_pat = len(DILATED)
    qkv = refs[:3 * n_pat]
    t_refs = refs[3 * n_pat:4 * n_pat]
    o_ref = refs[4 * n_pat]
    qlo, qhi, kp, vp, o_r, m_r, l_r, o_t, m_t, l_t, m_acc, n_acc, d_acc = refs[4 * n_pat + 1:]
    lo = _lo_mask()
    nblk = seq // BLK
    chunk = 512
    kp[:BLK, :] = jnp.zeros((BLK, LANES), BF16)
    vp[:BLK, :] = jnp.zeros((BLK, LANES), BF16)

    for pi, ((_, dil), t_ref) in enumerate(zip(DILATED, t_refs)):
        q_ref, k_ref, v_ref = qkv[pi], qkv[n_pat + pi], qkv[2 * n_pat + pi]
        rl = seq // dil
        nb = rl // BLK
        for r in range(dil):
            idx = (0,) if dil == 1 else (0, r)
            q = q_ref[idx]
            zero = jnp.zeros_like(q)
            qlo[r * rl:(r + 1) * rl, :] = jnp.where(lo, q, zero)
            qhi[r * rl:(r + 1) * rl, :] = jnp.where(lo, zero, q)
            kp[BLK + r * rl:BLK + (r + 1) * rl, :] = k_ref[idx]
            vp[BLK + r * rl:BLK + (r + 1) * rl, :] = v_ref[idx]
        o_dst, m_dst, l_dst = (n_acc, m_acc, d_acc) if dil == 1 else (o_r, m_r, l_r)

        def blocks(it, carry, t_ref=t_ref, nb=nb, o_dst=o_dst, m_dst=m_dst, l_dst=l_dst):
            for u in range(unroll):
                g = it * unroll + u
                first = jnp.where((g & (nb - 1)) == 0, 1, 0)
                q0 = pl.multiple_of(g * BLK, BLK)
                qs = jnp.concatenate([qlo[pl.ds(q0, BLK), :], qhi[pl.ds(q0, BLK), :]], axis=0)
                s = _dot_nt(qs, kp[pl.ds(q0, 2 * BLK), :]) + t_ref[0, first]
                m = jnp.max(s, axis=-1, keepdims=True)
                p = jnp.exp2(s - m)
                l = jnp.sum(p, axis=-1, keepdims=True)
                pv = jnp.dot(p.astype(BF16), vp[pl.ds(q0, 2 * BLK), :], preferred_element_type=F32)
                o_dst[pl.ds(q0, BLK), :] = jnp.where(lo, pv[:BLK], pv[BLK:])
                m_dst[pl.ds(q0, BLK), :] = jnp.where(lo, m[:BLK], m[BLK:])
                l_dst[pl.ds(q0, BLK), :] = jnp.where(lo, l[:BLK], l[BLK:])
            return carry

        lax.fori_loop(0, nblk // unroll, blocks, 0)
        if dil == 1:
            continue

        for r in range(dil):
            for src, dst in ((o_r, o_t), (m_r, m_t), (l_r, l_t)):
                dst[pl.ds(r, rl, stride=dil), :] = src[r * rl:(r + 1) * rl, :]

        def fold(c, carry, last=(pi == n_pat - 1)):
            rows = pl.ds(pl.multiple_of(c * chunk, chunk), chunk)
            m_old, m_new = m_acc[rows, :], m_t[rows, :]
            mx = jnp.maximum(m_old, m_new)
            a, b_ = jnp.exp2(m_old - mx), jnp.exp2(m_new - mx)
            n_new = a * n_acc[rows, :] + b_ * o_t[rows, :]
            d_new = a * d_acc[rows, :] + b_ * l_t[rows, :]
            if last:
                o_ref[0, rows, :] = (n_new * (1.0 / d_new)).astype(o_ref.dtype)
            else:
                m_acc[rows, :] = mx
                n_acc[rows, :] = n_new
                d_acc[rows, :] = d_new
            return carry

        lax.fori_loop(0, seq // chunk, fold, 0)


def _dilated_attention(qa, ka, va, tables, unroll=32):
    assert DILATED[0][1] == 1 and all(dl > 1 for _, dl in DILATED[1:])
    b, s, c = qa[0].shape
    xs = pl.BlockSpec((1, s, LANES), lambda i, j: (i, 0, j))
    lay = [xs] + [pl.BlockSpec((1, dl, s // dl, LANES), lambda i, j: (i, 0, 0, j)) for _, dl in DILATED[1:]]
    ts = pl.BlockSpec((1, 2, 2 * BLK, 2 * BLK), lambda i, j: (j, 0, 0, 0))
    big = lambda rows, dt: pltpu.VMEM((rows, LANES), dt)
    return pl.pallas_call(
        functools.partial(_dilated_kernel, seq=s, unroll=unroll),
        grid=(b, c // LANES),
        in_specs=lay * 3 + [ts] * len(DILATED),
        out_specs=xs,
        out_shape=jax.ShapeDtypeStruct((b, s, c), BF16),
        scratch_shapes=[big(s, BF16)] * 2 + [big(s + BLK, BF16)] * 2 + [big(s, F32)] * 9,
        compiler_params=_cparams(("parallel", "parallel")),
        name="dilated_attn",
    )(*qa, *ka, *va, *tables)


def _win_kernel(q_ref, k_ref, v_ref, t_ref, o_ref, qst, kaug, vp, *, seq, n_prev, unroll):
    lo = _lo_mask()
    lane = lax.broadcasted_iota(jnp.int32, (1, LANES), 1)
    flag = lane == HEAD_DIM
    pad = n_prev * BLK
    span = (n_prev + 1) * BLK
    negflag = jnp.where(flag, NEG, 0.0)
    for pr in range(R_B // 2):
        q = q_ref[0, :, pr * LANES:(pr + 1) * LANES].astype(F32)
        qst[2 * pr] = jnp.where(lo, q, negflag).astype(BF16)
        qst[2 * pr + 1] = jnp.where(lo, pltpu.roll(q, HEAD_DIM, 1), negflag).astype(BF16)
    k = k_ref[0]
    kaug[:pad, :] = jnp.broadcast_to(jnp.where(flag, 1.0, 0.0), (pad, LANES)).astype(BF16)
    kaug[pad:, :] = jnp.where(lo, k, jnp.zeros_like(k))
    vp[:pad, :] = jnp.zeros((pad, LANES), BF16)
    vp[pad:, :] = v_ref[0]

    def blocks(it, carry):
        for u in range(unroll):
            n = it * unroll + u
            r0 = pl.multiple_of(n * BLK, BLK)
            qs = jnp.concatenate([qst[h, pl.ds(r0, BLK), :] for h in range(R_B)], axis=0)
            s = _dot_nt(qs, kaug[pl.ds(r0, span), :]) + t_ref[0]
            m = jnp.max(s, axis=-1, keepdims=True)
            p = jnp.exp2(s - m)
            l = jnp.sum(p, axis=-1, keepdims=True)
            pv = jnp.dot(p.astype(BF16), vp[pl.ds(r0, span), :], preferred_element_type=F32) * (1.0 / l)
            for pr in range(R_B // 2):
                o = jnp.where(lo, pv[2 * pr * BLK:(2 * pr + 1) * BLK], pv[(2 * pr + 1) * BLK:(2 * pr + 2) * BLK])
                o_ref[0, pl.ds(r0, BLK), pr * LANES:(pr + 1) * LANES] = o.astype(o_ref.dtype)
        return carry

    lax.fori_loop(0, seq // BLK // unroll, blocks, 0)


def _win_attention(qb, kw_dup, vw_dup, table, n_prev, unroll=8):
    b, s, _ = qb.shape
    gw = R_B * HEAD_DIM
    w = table.shape[-1]
    return pl.pallas_call(
        functools.partial(_win_kernel, seq=s, n_prev=n_prev, unroll=unroll),
        grid=(b, G_B),
        in_specs=[pl.BlockSpec((1, s, gw), lambda i, g: (i, 0, g)),
                  pl.BlockSpec((1, s, LANES), lambda i, g: (i, 0, g)),
                  pl.BlockSpec((1, s, LANES), lambda i, g: (i, 0, g)),
                  pl.BlockSpec((1, R_B * BLK, w), lambda i, g: (g, 0, 0))],
        out_specs=pl.BlockSpec((1, s, gw), lambda i, g: (i, 0, g)),
        out_shape=jax.ShapeDtypeStruct((b, s, D_B), BF16),
        scratch_shapes=[pltpu.VMEM((R_B, s, LANES), BF16)] + [pltpu.VMEM((s + n_prev * BLK, LANES), BF16)] * 2,
        compiler_params=_cparams(("parallel", "parallel")),
        name="nsa_window",
    )(qb, kw_dup, vw_dup, table)


def _compress_kernel(kin, vin, w1, w2, pe, kout, vout):
    n_chunk = kin.shape[2]
    half = w1.shape[1] // 2
    rows = lax.broadcasted_iota(jnp.int32, (n_chunk, 1), 0)
    for t, (xin, out) in enumerate(((kin, kout), (vin, vout))):
        x = xin[0, 0]
        a = jnp.dot(x, w1[t, :half], preferred_element_type=F32)
        b = jnp.dot(x, w1[t, half:], preferred_element_type=F32)
        c = jnp.dot(pe[t], w1[t], preferred_element_type=F32)[0:1]
        hid = a + pltpu.roll(b, n_chunk - 1, 0) + c
        act = hid * jax.nn.sigmoid(hid)
        o = jnp.dot(act.astype(BF16), w2[t], preferred_element_type=F32)
        out[0, 0] = jnp.where(rows < n_chunk - 1, o, 0.0).astype(BF16)


def _compress(kc_in, vc_in, w1, w2d, pe):
    b, g, n_chunk, f = kc_in.shape
    xs = pl.BlockSpec((1, 1, n_chunk, f), lambda i, j: (i, j, 0, 0))
    os_ = pl.BlockSpec((1, 1, n_chunk, LANES), lambda i, j: (i, j, 0, 0))
    full = lambda a: pl.BlockSpec(a.shape, lambda i, j: (0,) * a.ndim)
    od = jax.ShapeDtypeStruct((b, g, n_chunk, LANES), BF16)
    return pl.pallas_call(
        _compress_kernel,
        grid=(b, g),
        in_specs=[xs, xs, full(w1), full(w2d), full(pe)],
        out_specs=[os_, os_],
        out_shape=[od, od],
        compiler_params=_cparams(("parallel", "parallel")),
        name="nsa_compress",
    )(kc_in, vc_in, w1, w2d, pe)


def _cmp_kernel(q_ref, kc_ref, vc_ref, t_ref, ov_ref, o_ref, sb_ref, *, tq):
    i = pl.program_id(2)
    lo = _lo_mask()
    kcd = kc_ref[0, 0]
    vcd = vc_ref[0, 0]
    zero = jnp.zeros_like(vcd)
    vlo = jnp.where(lo, vcd, zero)
    vhi = jnp.where(lo, zero, vcd)
    psum = None
    for pr in range(R_B // 2):
        q = q_ref[0, :, pr * LANES:(pr + 1) * LANES]
        qz = jnp.zeros_like(q)
        acc = None
        for hh, (qm, vm) in enumerate(((jnp.where(lo, q, qz), vlo), (jnp.where(lo, qz, q), vhi))):
            s = _dot_nt(qm, kcd) + t_ref[2 * pr + hh]
            m = jnp.max(s, axis=-1, keepdims=True)
            e = jnp.exp2(s - m)
            l = jnp.sum(e, axis=-1, keepdims=True)
            p = e * jnp.where(m > 0.5 * NEG, 1.0 / l, 0.0)
            psum = p if psum is None else psum + p
            pv = jnp.dot(p.astype(BF16), vm, preferred_element_type=F32)
            acc = pv if acc is None else acc + pv
        o_ref[0, :, pr * LANES:(pr + 1) * LANES] = acc.astype(o_ref.dtype)

    p_hi = psum.astype(BF16)
    p_lo = (psum - p_hi.astype(F32)).astype(BF16)
    ov = ov_ref[...]
    score = (_dot_nt(ov, p_hi) + _dot_nt(ov, p_lo))
    n_slc = score.shape[0]
    blk = lax.broadcasted_iota(jnp.int32, (n_slc, 1), 0)
    pos = i * tq + lax.broadcasted_iota(jnp.int32, (1, tq), 1)
    cur = pos >> int(math.log2(SLC_BLOCK))
    forced = (blk == 0) | ((cur - blk >= 0) & (cur - blk < SLC_LOCAL))
    score = jnp.where(forced, 1e9, jnp.where(blk > cur, -1e9, score))
    sub = lax.broadcasted_iota(jnp.int32, (8, 1), 0)
    sel_rows = []
    for g8 in range(n_slc // 8):
        sg = score[g8 * 8:(g8 + 1) * 8]
        rank = jnp.zeros(sg.shape, F32)
        for mp in range(n_slc):
            row = score[mp:mp + 1]
            if mp < g8 * 8:
                ahead = jnp.where(row >= sg, 1.0, 0.0)
            elif mp >= (g8 + 1) * 8:
                ahead = jnp.where(row > sg, 1.0, 0.0)
            else:
                ahead = jnp.where(sub > mp - g8 * 8, jnp.where(row >= sg, 1.0, 0.0), jnp.where(row > sg, 1.0, 0.0))
            rank = rank + ahead
        sel_rows.append(jnp.where(rank < float(min(SLC_TOPK, n_slc)), 0.0, NEG))
    pad = LANES // n_slc
    selb = jnp.concatenate(sel_rows * pad, axis=0)
    sb_ref[0, 0] = selb.T.astype(BF16)


def _cmp_attention(qb, kcd, vcd, table, ov_t, tq=512):
    b, s, _ = qb.shape
    n_col = kcd.shape[2]
    gw = R_B * HEAD_DIM
    return pl.pallas_call(
        functools.partial(_cmp_kernel, tq=tq),
        grid=(b, G_B, s // tq),
        in_specs=[pl.BlockSpec((1, tq, gw), lambda i, g, j: (i, j, g)),
                  pl.BlockSpec((1, 1, n_col, LANES), lambda i, g, j: (i, g, 0, 0)),
                  pl.BlockSpec((1, 1, n_col, LANES), lambda i, g, j: (i, g, 0, 0)),
                  pl.BlockSpec((R_B, tq, n_col), lambda i, g, j: (g, j, 0)),
                  pl.BlockSpec(ov_t.shape, lambda i, g, j: (0, 0))],
        out_specs=[pl.BlockSpec((1, tq, gw), lambda i, g, j: (i, j, g)),
                   pl.BlockSpec((1, 1, tq, LANES), lambda i, g, j: (i, g, j, 0))],
        out_shape=[jax.ShapeDtypeStruct((b, s, D_B), BF16),
                   jax.ShapeDtypeStruct((b, G_B, s, LANES), BF16)],
        compiler_params=_cparams(("parallel", "parallel", "parallel")),
        name="nsa_cmp_topk",
    )(qb, kcd, vcd, table, ov_t)


def _sel_kernel(q_ref, sb_ref, k_ref, v_ref, t_ref, o_ref, kaug, m_ref, l_ref, acc_ref, *, nblk):
    lo = _lo_mask()
    s_len = nblk * BLK
    n_diag = t_ref.shape[1] - 1
    rowblk = lax.broadcasted_iota(jnp.int32, (s_len, LANES), 0) >> int(math.log2(SLC_BLOCK))
    lane = lax.broadcasted_iota(jnp.int32, (s_len, LANES), 1)
    onehot = jnp.where(lane - HEAD_DIM == rowblk, 1.0, 0.0).astype(BF16)
    kaug[...] = jnp.where(lo, k_ref[0], onehot)
    m_ref[...] = jnp.full(m_ref.shape, NEG, F32)
    l_ref[...] = jnp.zeros(l_ref.shape, F32)
    acc_ref[...] = jnp.zeros(acc_ref.shape, F32)
    kb = 4
    qb = 2
    sub = kb // qb
    kt_rows, qt_rows = kb * BLK, qb * BLK

    def ktile(t, carry):
        k0 = pl.multiple_of(t * kt_rows, kt_rows)
        kt = kaug[pl.ds(k0, kt_rows), :]
        vt = v_ref[0, pl.ds(k0, kt_rows), :]

        def qstep(mp, c):
            scores = []
            for u in range(sub):
                r0 = pl.multiple_of((sub * mp + u) * qt_rows, qt_rows)
                sbf = sb_ref[0, 0, pl.ds(r0, qt_rows), :].astype(F32)
                rows = []
                for pr in range(R_B // 2):
                    q = q_ref[0, pl.ds(r0, qt_rows), pr * LANES:(pr + 1) * LANES].astype(F32)
                    rows.append(jnp.where(lo, q, sbf))
                    rows.append(jnp.where(lo, pltpu.roll(q, HEAD_DIM, 1), sbf))
                qst = jnp.concatenate(rows, axis=0).astype(BF16)
                scores.append(_dot_nt(qst, kt))
            for u in range(sub):
                r0 = pl.multiple_of((sub * mp + u) * qt_rows, qt_rows)
                s = scores[u]
                ps, alphas = [], []
                for h in range(R_B):
                    for a in range(qb):
                        n = qb * (sub * mp + u) + a
                        bias = jnp.concatenate(
                            [t_ref[h, jnp.clip(n - kb * t - c + 1, 0, n_diag)] for c in range(kb)], axis=1)
                        lo_r = h * qt_rows + a * BLK
                        st = pl.ds(r0 + a * BLK, BLK)
                        sh = s[lo_r:lo_r + BLK] + bias
                        m_prev = m_ref[h, st, :]
                        m_new = jnp.maximum(m_prev, jnp.max(sh, axis=-1, keepdims=True))
                        alpha = jnp.exp2(m_prev - m_new)
                        p = jnp.exp2(sh - jnp.concatenate([m_new] * kb, axis=1))
                        psum = p[:, :LANES]
                        for c in range(1, kb):
                            psum = psum + p[:, c * LANES:(c + 1) * LANES]
                        l_ref[h, st, :] = alpha * l_ref[h, st, :] + psum
                        m_ref[h, st, :] = m_new
                        ps.append(p.astype(BF16))
                        alphas.append(alpha)
                pv = jnp.dot(jnp.concatenate(ps, axis=0), vt, preferred_element_type=F32)
                for h in range(R_B):
                    for a in range(qb):
                        lo_r = h * qt_rows + a * BLK
                        st = pl.ds(r0 + a * BLK, BLK)
                        acc_ref[h, st, :] = alphas[qb * h + a] * acc_ref[h, st, :] + pv[lo_r:lo_r + BLK]
            return c

        lax.fori_loop(t, nblk // kb, qstep, 0)
        return carry

    lax.fori_loop(0, nblk // kb, ktile, 0)

    def finish(n, carry):
        st = pl.ds(pl.multiple_of(n * BLK, BLK), BLK)
        for pr in range(R_B // 2):
            inv = [1.0 / jnp.sum(l_ref[2 * pr + hh, st, :], axis=-1, keepdims=True) for hh in range(2)]
            o = jnp.where(lo, acc_ref[2 * pr, st, :] * inv[0], acc_ref[2 * pr + 1, st, :] * inv[1])
            o_ref[0, st, pr * LANES:(pr + 1) * LANES] = o.astype(o_ref.dtype)
        return carry

    lax.fori_loop(0, nblk, finish, 0)


def _sel_attention(qb, selb, ks_dup, vs_dup, table):
    b, s, _ = qb.shape
    gw = R_B * HEAD_DIM
    n_e = table.shape[1]
    one = pl.Buffered(1)
    return pl.pallas_call(
        functools.partial(_sel_kernel, nblk=s // BLK),
        grid=(b, G_B),
        in_specs=[pl.BlockSpec((1, s, gw), lambda i, g: (i, 0, g), pipeline_mode=one),
                  pl.BlockSpec((1, 1, s, LANES), lambda i, g: (i, g, 0, 0), pipeline_mode=one),
                  pl.BlockSpec((1, s, LANES), lambda i, g: (i, 0, g), pipeline_mode=one),
                  pl.BlockSpec((1, s, LANES), lambda i, g: (i, 0, g), pipeline_mode=one),
                  pl.BlockSpec((R_B, n_e, BLK, BLK), lambda i, g: (g, 0, 0, 0), pipeline_mode=one)],
        out_specs=pl.BlockSpec((1, s, gw), lambda i, g: (i, 0, g)),
        out_shape=jax.ShapeDtypeStruct((b, s, D_B), BF16),
        scratch_shapes=[pltpu.VMEM((s, LANES), BF16)] + [pltpu.VMEM((R_B, s, LANES), F32)] * 3,
        compiler_params=_cparams(("parallel", "parallel")),
        name="nsa_selected",
    )(qb, selb, ks_dup, vs_dup, table)


def _combine_kernel(x_ref, oa, za, oc, osl, ow, gb, zb, wo, ex, gate, lng, lnb, out_ref, *, alpha):
    f = lambda r: r[0].astype(F32)
    z = f(za)
    mix_a = f(oa) * (z * jax.nn.sigmoid(z))
    g = jax.nn.sigmoid(gb[0])
    g_hi = g.astype(BF16)
    g_lo = (g - g_hi.astype(F32)).astype(BF16)
    gx = (jnp.dot(g_hi, ex[...], preferred_element_type=F32)
          + jnp.dot(g_lo, ex[...], preferred_element_type=F32))
    ob = gx[:, :D_B] * f(oc) + gx[:, D_B:2 * D_B] * f(osl) + gx[:, 2 * D_B:] * f(ow)
    z = f(zb)
    mix_b = ob * (z * jax.nn.sigmoid(z))
    y = (jnp.dot(mix_a.astype(BF16), wo[:D_A], preferred_element_type=F32)
         + jnp.dot(mix_b.astype(BF16), wo[D_A:], preferred_element_type=F32))
    r = alpha * x_ref[0] + (1.0 + gate[0]) * y
    mu = jnp.mean(r, axis=-1, keepdims=True)
    rc = r - mu
    var = jnp.mean(rc * rc, axis=-1, keepdims=True)
    out_ref[0] = rc * lax.rsqrt(var + LN_EPS) * lng[...] + lnb[...]


def _combine(x, o_a, za, o_cmp, o_slc, o_win, gb, zb, w_out, expand, gate, ln_g, ln_b, alpha, tm=512):
    b, s, d = x.shape
    bs = lambda n: pl.BlockSpec((1, tm, n), lambda i, j: (i, j, 0))
    full = lambda a: pl.BlockSpec(a.shape, lambda i, j: (0,) * a.ndim)
    return pl.pallas_call(
        functools.partial(_combine_kernel, alpha=alpha),
        grid=(b, s // tm),
        in_specs=[bs(d)] + [bs(D_A)] * 2 + [bs(D_B)] * 3 + [bs(LANES), bs(D_B), full(w_out), full(expand),
                  pl.BlockSpec((1, 1, d), lambda i, j: (i, 0, 0)), full(ln_g), full(ln_b)],
        out_specs=bs(d),
        out_shape=jax.ShapeDtypeStruct((b, s, d), F32),
        compiler_params=_cparams(("parallel", "parallel")),
        name="merge_out_proj_ln",
    )(x, o_a, za, o_cmp, o_slc, o_win, gb, zb, w_out, expand, gate, ln_g, ln_b)


def _pack_w_in(w_in):
    n_gb = 3 * H_B
    main = w_in[..., :_C_ZB]
    gbw = w_in[..., _C_ZB:_C_ZB + n_gb]
    zbw = w_in[..., _C_ZB + n_gb:]
    pad = jnp.zeros(w_in.shape[:-1] + (LANES - n_gb,), w_in.dtype)
    return jnp.concatenate([main, zbw, gbw, pad], axis=-1).astype(BF16)


def _gate_expand():
    e = np.zeros((LANES, 3 * D_B), np.float32)
    for i in range(3):
        for h in range(H_B):
            e[i * H_B + h, i * D_B + h * HEAD_DIM:i * D_B + (h + 1) * HEAD_DIM] = 1.0
    return jnp.asarray(e, BF16)


def _overlap_t(n_col, n_slc):
    cs = np.arange(n_col)[None, :] * CMP_STRIDE
    ss = np.arange(n_slc)[:, None] * SLC_BLOCK
    return jnp.asarray(((cs < ss + SLC_BLOCK) & (cs + CMP_LEN > ss)).astype(np.float32), BF16)


def _layer(x, shift, scl, gate, w_in_p, w_out_b, pe_b, w1_b, w2d_b, ln_g, ln_b, tables, consts, alpha):
    b, s, d = x.shape
    (qa, ka, va, za, qb, kc_in, vc_in, ks, vs, kw, vw, zb, gb) = _project(x, shift, scl, w_in_p)

    o_a = _dilated_attention(qa, ka, va, tables["dilated"])

    kcd, vcd = _compress(kc_in, vc_in, w1_b, w2d_b, pe_b)
    o_cmp, selb = _cmp_attention(qb, kcd, vcd, tables["cmp"], consts["overlap_t"])
    o_slc = _sel_attention(qb, selb, ks, vs, tables["sel"])
    o_win = _win_attention(qb, kw, vw, tables["win"], n_prev=-(-(WIN - 1) // BLK))

    return _combine(x, o_a, za, o_cmp, o_slc, o_win, gb, zb, w_out_b, consts["expand"], gate,
                    ln_g, ln_b, alpha)


def kernel(x, c, w_in, w_out, cmp_pe, cmp_w1, cmp_w2, w_ada, b_ada, ln_g, ln_b, rel_bias):
    b, s, d = x.shape
    depth = w_in.shape[0]
    alpha = (2 * depth) ** 0.25
    n_pair = D_A // LANES

    bb = rel_bias.astype(F32) * LOG2E
    def dilated_table(window, dil):
        t = _band_table(bb[:, :H_A], 1, window // dil, dil).reshape(n_pair, 2 * BLK, 2 * BLK)
        first = t.at[..., :BLK].set(NEG)
        return jnp.stack([t, first], axis=1)

    n_prev_win = -(-(WIN - 1) // BLK)
    tables = {
        "dilated": [dilated_table(w, dl) for w, dl in DILATED],
        "win": _band_table(bb[:, H_A:], n_prev_win, WIN - 1, 1).reshape(G_B, R_B * BLK, (n_prev_win + 1) * BLK),
        "cmp": _cmp_table(bb[:, H_A:], s),
        "sel": _diag_table(bb[:, H_A:], 13),
    }
    consts = {"expand": _gate_expand(), "overlap_t": _overlap_t(s // CMP_STRIDE, s // SLC_BLOCK)}

    mod = _ada_mod(c, w_ada, b_ada)
    w_in_p = _pack_w_in(w_in)
    w_out_b = w_out.astype(BF16)
    w1_b = cmp_w1.astype(BF16)
    w2d_b = jnp.concatenate([cmp_w2, cmp_w2], axis=-1).astype(BF16)
    pe_b = jnp.broadcast_to(cmp_pe.reshape(depth, 2, 1, CMP_LEN * HEAD_DIM), (depth, 2, 8, CMP_LEN * HEAD_DIM)).astype(BF16)

    for l in range(depth):
        m = mod[l, :b]
        shift, scl, gate = (m[:, i * d:(i + 1) * d].reshape(b, 1, d) for i in range(3))
        x = _layer(x, shift, scl, gate, w_in_p[l], w_out_b[l], pe_b[l], w1_b[l], w2d_b[l],
                   ln_g[l].reshape(1, d), ln_b[l].reshape(1, d), tables, consts, alpha)
    return x
```

```python
import functools
import math

import numpy as np
import jax
import jax.numpy as jnp
from jax import lax
from jax.experimental import pallas as pl
from jax.experimental.pallas import tpu as pltpu

F32 = jnp.float32
BF16 = jnp.bfloat16

D_MODEL = 1024
HEAD_DIM = 64
H_A = 8
H_B = 8
G_B = 2
R_B = H_B // G_B
D_A = H_A * HEAD_DIM
D_B = H_B * HEAD_DIM
DILATED = ((128, 1), (512, 4), (2048, 16))
CMP_LEN = 32
CMP_STRIDE = 16
CMP_HIDDEN = 256
SLC_BLOCK = 64
SLC_TOPK = 16
SLC_LOCAL = 2
WIN = 512
N_BUCKETS = 32
MAX_DIST = 2048
LN_EPS = 1e-5
NEG = -1e30
LOG2E = 1.4426950408889634

LANES = 128
BLK = 128
VMEM_LIMIT = 56 * 1024 * 1024

_C_QA, _C_KA, _C_VA, _C_ZA, _C_QB = 0, 512, 1024, 1536, 2048
_C_NARROW = 2560
_C_ZB, _C_END = 3456, 3968
_N_MAIN = 3328


def _cparams(sem):
    return pltpu.CompilerParams(dimension_semantics=sem, vmem_limit_bytes=VMEM_LIMIT)


def _lo_mask():
    return lax.broadcasted_iota(jnp.int32, (1, LANES), 1) < HEAD_DIM


def _dot_nt(a, b):
    return lax.dot_general(a, b, (((1,), (1,)), ((), ())), preferred_element_type=F32)


def _bucket_np(dist):
    d = np.maximum(dist, 0)
    max_exact = N_BUCKETS // 2
    large = max_exact + (np.log(np.maximum(d, 1).astype(np.float32) / np.float32(max_exact))
                         / np.float32(math.log(MAX_DIST / max_exact))
                         * np.float32(N_BUCKETS - max_exact)).astype(np.int32)
    return np.where(d < max_exact, d, np.minimum(large, N_BUCKETS - 1)).astype(np.int32)


def _masked_bias(bias_by_bucket, dist, valid):
    onehot = np.eye(N_BUCKETS, dtype=np.float32)[_bucket_np(dist)]
    t = jnp.einsum("...k,kh->h...", onehot, bias_by_bucket, precision=lax.Precision.HIGHEST)
    return jnp.where(jnp.asarray(valid)[None], t, NEG)


def _toeplitz_kernel(v_ref, o_ref):
    cols = o_ref.shape[-1]
    for e in range(o_ref.shape[1]):
        x = jnp.broadcast_to(v_ref[0, e:e + 1, :], (BLK, v_ref.shape[-1]))
        o_ref[0, e] = pltpu.roll(x, 0, 1, stride=1, stride_axis=0)[:, :cols]


def _toeplitz(h, cols):
    n_h, n_e, n = h.shape
    width = 1 << (n - 1).bit_length()
    vec = jnp.concatenate([h[..., BLK - 1:], jnp.zeros((n_h, n_e, width - n), F32), h[..., :BLK - 1]], axis=-1)
    return pl.pallas_call(
        _toeplitz_kernel,
        grid=(n_h,),
        in_specs=[pl.BlockSpec((1, n_e, width), lambda a: (a, 0, 0))],
        out_specs=pl.BlockSpec((1, n_e, BLK, cols), lambda a: (a, 0, 0, 0)),
        out_shape=jax.ShapeDtypeStruct((n_h, n_e, BLK, cols), F32),
        compiler_params=_cparams(("parallel",)),
        name="toeplitz_table",
    )(vec)


def _band_table(bias_by_bucket, n_prev, window, scale):
    w = (n_prev + 1) * BLK
    dist = n_prev * BLK - (np.arange(BLK - 1 + w) - (BLK - 1))
    h = _masked_bias(bias_by_bucket, dist * scale, (dist >= 0) & (dist <= window))
    return _toeplitz(h[:, None], w)[:, 0]


def _diag_table(bias_by_bucket, n_diag):
    delta = np.arange(-1, n_diag + 1)[:, None]
    dist = delta * BLK - (np.arange(2 * BLK - 1)[None, :] - (BLK - 1))
    return _toeplitz(_masked_bias(bias_by_bucket, dist, dist >= 0), BLK)


def _cmp_table_kernel(base_ref, o_ref, *, rows):
    i = pl.program_id(1)
    n_col = o_ref.shape[-1]
    width = base_ref.shape[-1]
    base = base_ref[0]
    for j in range(rows // CMP_STRIDE):
        chunk = i * (rows // CMP_STRIDE) + j
        shift = (chunk + width - (n_col - 1)) % width
        o_ref[0, j * CMP_STRIDE:(j + 1) * CMP_STRIDE, :] = pltpu.roll(base, shift, 1)[:, :n_col]


def _cmp_table(bias_by_bucket, seq, rows=1024):
    n_col = seq // CMP_STRIDE
    t = np.arange(2 * n_col) - (n_col - 1)
    dist = -CMP_STRIDE * t[None, :] + np.arange(CMP_STRIDE)[:, None] - (CMP_LEN - 1)
    base = _masked_bias(bias_by_bucket, dist, dist >= 0)
    h = base.shape[0]
    return pl.pallas_call(
        functools.partial(_cmp_table_kernel, rows=rows),
        grid=(h, seq // rows),
        in_specs=[pl.BlockSpec((1, CMP_STRIDE, 2 * n_col), lambda a, i: (a, 0, 0))],
        out_specs=pl.BlockSpec((1, rows, n_col), lambda a, i: (a, i, 0)),
        out_shape=jax.ShapeDtypeStruct((h, seq, n_col), F32),
        compiler_params=_cparams(("parallel", "parallel")),
        name="cmp_bias_table",
    )(base)


def _ada_kernel(c_ref, w_ref, b_ref, o_ref):
    o_ref[0] = jnp.dot(c_ref[...], w_ref[0].astype(BF16), preferred_element_type=F32) + b_ref[0]


def _ada_mod(c, w_ada, b_ada):
    depth, d, d3 = w_ada.shape
    cp = jnp.zeros((8, d), BF16).at[:c.shape[0]].set(c.astype(BF16))
    nj = d3 // d
    return pl.pallas_call(
        _ada_kernel,
        grid=(depth, nj),
        in_specs=[pl.BlockSpec((8, d), lambda l, j: (0, 0)),
                  pl.BlockSpec((1, d, d), lambda l, j: (l, 0, j)),
                  pl.BlockSpec((1, 1, d), lambda l, j: (l, 0, j))],
        out_specs=pl.BlockSpec((1, 8, d), lambda l, j: (l, 0, j)),
        out_shape=jax.ShapeDtypeStruct((depth, 8, d3), F32),
        compiler_params=_cparams(("parallel", "parallel")),
        name="ada_mod",
    )(cp, w_ada, b_ada.reshape(depth, 1, d3))


def _proj_kernel(x_ref, sh_ref, sc_ref, w_ref, *refs, dils):
    n_lay = len(dils)
    qa, ka, va = refs[:n_lay], refs[n_lay:2 * n_lay], refs[2 * n_lay:3 * n_lay]
    za, qb, kc, vc, ks, vs, kw, vw, zb, gb, h_ref, xs = refs[3 * n_lay:]
    tm = h_ref.shape[0]
    h_ref[...] = (x_ref[0] * (1.0 + sc_ref[0]) + sh_ref[0]).astype(BF16)

    def mm(c0, n):
        return jnp.dot(h_ref[...], w_ref[:, c0:c0 + n], preferred_element_type=F32)

    def emit_regrouped(a, outs):
        outs[0][0] = a.astype(BF16)
        n_slab = a.shape[1] // LANES
        for c in range(n_slab):
            xs[0, c] = a[:, c * LANES:(c + 1) * LANES]
        d_prev = 1
        for lvl, (ref, dil) in enumerate(zip(outs[1:], dils[1:])):
            ratio, len_prev, len_new = dil // d_prev, tm // d_prev, tm // dil
            src, dst = xs.at[lvl % 2], xs.at[(lvl + 1) % 2]
            keep = lvl + 2 < len(dils)
            for rp in range(d_prev):
                for rs in range(ratio):
                    r = rp + d_prev * rs
                    for c in range(n_slab):
                        rows = src[c, pl.ds(rp * len_prev + rs, len_new, stride=ratio), :]
                        ref[0, r, :, c * LANES:(c + 1) * LANES] = rows.astype(BF16)
                        if keep:
                            dst[c, r * len_new:(r + 1) * len_new, :] = rows
            d_prev = dil

    qs = HEAD_DIM ** -0.5 * LOG2E
    emit_regrouped(mm(_C_QA, D_A) * qs, qa)
    emit_regrouped(mm(_C_KA, D_A), ka)
    emit_regrouped(mm(_C_VA, D_A), va)
    za[0] = mm(_C_ZA, D_A).astype(BF16)
    qb[0] = (mm(_C_QB, D_B) * qs).astype(BF16)
    lo = _lo_mask()
    narrow = mm(_C_NARROW, 7 * LANES)
    piece = lambda i: narrow[:, i * LANES:(i + 1) * LANES]
    for ref, i in ((kc, 0), (vc, 1)):
        xs[0, 0] = piece(i)
        for t in range(0, CMP_STRIDE, 2):
            x0 = xs[0, 0, pl.ds(t, tm // CMP_STRIDE, stride=CMP_STRIDE), :]
            x1 = xs[0, 0, pl.ds(t + 1, tm // CMP_STRIDE, stride=CMP_STRIDE), :]
            cols = slice(t * HEAD_DIM, (t + 2) * HEAD_DIM)
            ref[0, 0, :, cols] = jnp.where(lo, x0, pltpu.roll(x1, HEAD_DIM, 1)).astype(BF16)
            ref[0, 1, :, cols] = jnp.where(lo, pltpu.roll(x0, HEAD_DIM, 1), x1).astype(BF16)
    for ref, i in ((ks, 2), (vs, 3), (kw, 4), (vw, 5)):
        a = piece(i)
        r = pltpu.roll(a, HEAD_DIM, 1)
        ref[0, :, :LANES] = jnp.where(lo, a, r).astype(BF16)
        ref[0, :, LANES:] = jnp.where(lo, r, a).astype(BF16)
    gb[0] = piece(6)
    zb[0] = mm(_C_ZB, D_B).astype(BF16)


def _project(x, shift, scl, w, tm=512):
    b, s, d = x.shape
    dils = tuple(dl for _, dl in DILATED)
    bs = lambda n: pl.BlockSpec((1, tm, n), lambda i, j: (i, j, 0))
    gs = pl.BlockSpec((1, G_B, tm // CMP_STRIDE, CMP_STRIDE * HEAD_DIM), lambda i, j: (i, 0, j, 0))
    sd = lambda n, dt=BF16: jax.ShapeDtypeStruct((b, s, n), dt)
    gd = jax.ShapeDtypeStruct((b, G_B, s // CMP_STRIDE, CMP_STRIDE * HEAD_DIM), BF16)
    mod = pl.BlockSpec((1, 1, d), lambda i, j: (i, 0, 0))
    lay_specs = [bs(D_A)] + [pl.BlockSpec((1, dl, tm // dl, D_A), lambda i, j: (i, 0, j, 0)) for dl in dils[1:]]
    lay_shapes = [sd(D_A)] + [jax.ShapeDtypeStruct((b, dl, s // dl, D_A), BF16) for dl in dils[1:]]
    outs = pl.pallas_call(
        functools.partial(_proj_kernel, dils=dils),
        grid=(b, s // tm),
        in_specs=[pl.BlockSpec((1, tm, d), lambda i, j: (i, j, 0)), mod, mod,
                  pl.BlockSpec((d, _C_END), lambda i, j: (0, 0))],
        out_specs=lay_specs * 3 + [bs(D_A), bs(D_B), gs, gs,
                                   bs(2 * LANES), bs(2 * LANES), bs(2 * LANES), bs(2 * LANES), bs(D_B), bs(LANES)],
        out_shape=lay_shapes * 3 + [sd(D_A), sd(D_B), gd, gd,
                                    sd(2 * LANES), sd(2 * LANES), sd(2 * LANES), sd(2 * LANES), sd(D_B), sd(LANES, F32)],
        scratch_shapes=[pltpu.VMEM((tm, d), BF16), pltpu.VMEM((2, D_A // LANES, tm, LANES), F32)],
        compiler_params=_cparams(("parallel", "parallel")),
        name="in_proj",
    )(x, shift, scl, w)
    n = len(dils)
    return (outs[:n], outs[n:2 * n], outs[2 * n:3 * n], *outs[3 * n:])


def _dilated_kernel(*refs, seq, unroll):
    n_pat = len(DILATED)
    qkv = refs[:3 * n_pat]
    t_refs = refs[3 * n_pat:4 * n_pat]
    o_ref = refs[4 * n_pat]
    qlo, qhi, kp, vp, o_r, m_r, l_r, o_t, m_t, l_t, m_acc, n_acc, d_acc = refs[4 * n_pat + 1:]
    lo = _lo_mask()
    nblk = seq // BLK
    chunk = 512
    kp[:BLK, :] = jnp.zeros((BLK, LANES), BF16)
    vp[:BLK, :] = jnp.zeros((BLK, LANES), BF16)

    for pi, ((_, dil), t_ref) in enumerate(zip(DILATED, t_refs)):
        q_ref, k_ref, v_ref = qkv[pi], qkv[n_pat + pi], qkv[2 * n_pat + pi]
        rl = seq // dil
        nb = rl // BLK
        for r in range(dil):
            idx = (0,) if dil == 1 else (0, r)
            q = q_ref[idx]
            zero = jnp.zeros_like(q)
            qlo[r * rl:(r + 1) * rl, :] = jnp.where(lo, q, zero)
            qhi[r * rl:(r + 1) * rl, :] = jnp.where(lo, zero, q)
            kp[BLK + r * rl:BLK + (r + 1) * rl, :] = k_ref[idx]
            vp[BLK + r * rl:BLK + (r + 1) * rl, :] = v_ref[idx]
        o_dst, m_dst, l_dst = (n_acc, m_acc, d_acc) if dil == 1 else (o_r, m_r, l_r)

        def blocks(it, carry, t_ref=t_ref, nb=nb, o_dst=o_dst, m_dst=m_dst, l_dst=l_dst):
            for u in range(unroll):
                g = it * unroll + u
                first = jnp.where((g & (nb - 1)) == 0, 1, 0)
                q0 = pl.multiple_of(g * BLK, BLK)
                qs = jnp.concatenate([qlo[pl.ds(q0, BLK), :], qhi[pl.ds(q0, BLK), :]], axis=0)
                s = _dot_nt(qs, kp[pl.ds(q0, 2 * BLK), :]) + t_ref[0, first]
                m = jnp.max(s, axis=-1, keepdims=True)
                p = jnp.exp2(s - m)
                l = jnp.sum(p, axis=-1, keepdims=True)
                pv = jnp.dot(p.astype(BF16), vp[pl.ds(q0, 2 * BLK), :], preferred_element_type=F32)
                o_dst[pl.ds(q0, BLK), :] = jnp.where(lo, pv[:BLK], pv[BLK:])
                m_dst[pl.ds(q0, BLK), :] = jnp.where(lo, m[:BLK], m[BLK:])
                l_dst[pl.ds(q0, BLK), :] = jnp.where(lo, l[:BLK], l[BLK:])
            return carry

        lax.fori_loop(0, nblk // unroll, blocks, 0)
        if dil == 1:
            continue

        for r in range(dil):
            for src, dst in ((o_r, o_t), (m_r, m_t), (l_r, l_t)):
                dst[pl.ds(r, rl, stride=dil), :] = src[r * rl:(r + 1) * rl, :]

        def fold(c, carry, last=(pi == n_pat - 1)):
            rows = pl.ds(pl.multiple_of(c * chunk, chunk), chunk)
            m_old, m_new = m_acc[rows, :], m_t[rows, :]
            mx = jnp.maximum(m_old, m_new)
            a, b_ = jnp.exp2(m_old - mx), jnp.exp2(m_new - mx)
            n_new = a * n_acc[rows, :] + b_ * o_t[rows, :]
            d_new = a * d_acc[rows, :] + b_ * l_t[rows, :]
            if last:
                o_ref[0, rows, :] = (n_new * (1.0 / d_new)).astype(o_ref.dtype)
            else:
                m_acc[rows, :] = mx
                n_acc[rows, :] = n_new
                d_acc[rows, :] = d_new
            return carry

        lax.fori_loop(0, seq // chunk, fold, 0)


def _dilated_attention(qa, ka, va, tables, unroll=32):
    assert DILATED[0][1] == 1 and all(dl > 1 for _, dl in DILATED[1:])
    b, s, c = qa[0].shape
    xs = pl.BlockSpec((1, s, LANES), lambda i, j: (i, 0, j))
    lay = [xs] + [pl.BlockSpec((1, dl, s // dl, LANES), lambda i, j: (i, 0, 0, j)) for _, dl in DILATED[1:]]
    ts = pl.BlockSpec((1, 2, 2 * BLK, 2 * BLK), lambda i, j: (j, 0, 0, 0))
    big = lambda rows, dt: pltpu.VMEM((rows, LANES), dt)
    return pl.pallas_call(
        functools.partial(_dilated_kernel, seq=s, unroll=unroll),
        grid=(b, c // LANES),
        in_specs=lay * 3 + [ts] * len(DILATED),
        out_specs=xs,
        out_shape=jax.ShapeDtypeStruct((b, s, c), BF16),
        scratch_shapes=[big(s, BF16)] * 2 + [big(s + BLK, BF16)] * 2 + [big(s, F32)] * 9,
        compiler_params=_cparams(("parallel", "parallel")),
        name="dilated_attn",
    )(*qa, *ka, *va, *tables)


def _win_kernel(q_ref, k_ref, v_ref, t_ref, o_ref, qst, kaug, vp, *, seq, n_prev, unroll):
    lo = _lo_mask()
    lane = lax.broadcasted_iota(jnp.int32, (1, LANES), 1)
    flag = lane == HEAD_DIM
    pad = n_prev * BLK
    span = (n_prev + 1) * BLK
    negflag = jnp.where(flag, NEG, 0.0)
    for pr in range(R_B // 2):
        q = q_ref[0, :, pr * LANES:(pr + 1) * LANES].astype(F32)
        qst[2 * pr] = jnp.where(lo, q, negflag).astype(BF16)
        qst[2 * pr + 1] = jnp.where(lo, pltpu.roll(q, HEAD_DIM, 1), negflag).astype(BF16)
    k = k_ref[0]
    kaug[:pad, :] = jnp.broadcast_to(jnp.where(flag, 1.0, 0.0), (pad, LANES)).astype(BF16)
    kaug[pad:, :] = jnp.where(lo, k, jnp.zeros_like(k))
    vp[:pad, :] = jnp.zeros((pad, LANES), BF16)
    vp[pad:, :] = v_ref[0]

    def blocks(it, carry):
        for u in range(unroll):
            n = it * unroll + u
            r0 = pl.multiple_of(n * BLK, BLK)
            qs = jnp.concatenate([qst[h, pl.ds(r0, BLK), :] for h in range(R_B)], axis=0)
            s = _dot_nt(qs, kaug[pl.ds(r0, span), :]) + t_ref[0]
            m = jnp.max(s, axis=-1, keepdims=True)
            p = jnp.exp2(s - m)
            l = jnp.sum(p, axis=-1, keepdims=True)
            pv = jnp.dot(p.astype(BF16), vp[pl.ds(r0, span), :], preferred_element_type=F32) * (1.0 / l)
            for pr in range(R_B // 2):
                o = jnp.where(lo, pv[2 * pr * BLK:(2 * pr + 1) * BLK], pv[(2 * pr + 1) * BLK:(2 * pr + 2) * BLK])
                o_ref[0, pl.ds(r0, BLK), pr * LANES:(pr + 1) * LANES] = o.astype(o_ref.dtype)
        return carry

    lax.fori_loop(0, seq // BLK // unroll, blocks, 0)


def _win_attention(qb, kw_dup, vw_dup, table, n_prev, unroll=8):
    b, s, _ = qb.shape
    gw = R_B * HEAD_DIM
    w = table.shape[-1]
    return pl.pallas_call(
        functools.partial(_win_kernel, seq=s, n_prev=n_prev, unroll=unroll),
        grid=(b, G_B),
        in_specs=[pl.BlockSpec((1, s, gw), lambda i, g: (i, 0, g)),
                  pl.BlockSpec((1, s, LANES), lambda i, g: (i, 0, g)),
                  pl.BlockSpec((1, s, LANES), lambda i, g: (i, 0, g)),
                  pl.BlockSpec((1, R_B * BLK, w), lambda i, g: (g, 0, 0))],
        out_specs=pl.BlockSpec((1, s, gw), lambda i, g: (i, 0, g)),
        out_shape=jax.ShapeDtypeStruct((b, s, D_B), BF16),
        scratch_shapes=[pltpu.VMEM((R_B, s, LANES), BF16)] + [pltpu.VMEM((s + n_prev * BLK, LANES), BF16)] * 2,
        compiler_params=_cparams(("parallel", "parallel")),
        name="nsa_window",
    )(qb, kw_dup, vw_dup, table)


def _compress_kernel(kin, vin, w1, w2, pe, kout, vout):
    n_chunk = kin.shape[2]
    half = w1.shape[1] // 2
    rows = lax.broadcasted_iota(jnp.int32, (n_chunk, 1), 0)
    for t, (xin, out) in enumerate(((kin, kout), (vin, vout))):
        x = xin[0, 0]
        a = jnp.dot(x, w1[t, :half], preferred_element_type=F32)
        b = jnp.dot(x, w1[t, half:], preferred_element_type=F32)
        c = jnp.dot(pe[t], w1[t], preferred_element_type=F32)[0:1]
        hid = a + pltpu.roll(b, n_chunk - 1, 0) + c
        act = hid * jax.nn.sigmoid(hid)
        o = jnp.dot(act.astype(BF16), w2[t], preferred_element_type=F32)
        out[0, 0] = jnp.where(rows < n_chunk - 1, o, 0.0).astype(BF16)


def _compress(kc_in, vc_in, w1, w2d, pe):
    b, g, n_chunk, f = kc_in.shape
    xs = pl.BlockSpec((1, 1, n_chunk, f), lambda i, j: (i, j, 0, 0))
    os_ = pl.BlockSpec((1, 1, n_chunk, LANES), lambda i, j: (i, j, 0, 0))
    full = lambda a: pl.BlockSpec(a.shape, lambda i, j: (0,) * a.ndim)
    od = jax.ShapeDtypeStruct((b, g, n_chunk, LANES), BF16)
    return pl.pallas_call(
        _compress_kernel,
        grid=(b, g),
        in_specs=[xs, xs, full(w1), full(w2d), full(pe)],
        out_specs=[os_, os_],
        out_shape=[od, od],
        compiler_params=_cparams(("parallel", "parallel")),
        name="nsa_compress",
    )(kc_in, vc_in, w1, w2d, pe)


def _cmp_kernel(q_ref, kc_ref, vc_ref, t_ref, ov_ref, o_ref, sb_ref, *, tq):
    i = pl.program_id(2)
    lo = _lo_mask()
    n_col, n_slc = kc_ref.shape[2], ov_ref.shape[0]

    def attend_and_rank(ncol, nsel):
        kcd = kc_ref[0, 0, :ncol, :]
        vcd = vc_ref[0, 0, :ncol, :]
        zero = jnp.zeros_like(vcd)
        vlo = jnp.where(lo, vcd, zero)
        vhi = jnp.where(lo, zero, vcd)
        psum = None
        for pr in range(R_B // 2):
            q = q_ref[0, :, pr * LANES:(pr + 1) * LANES]
            qz = jnp.zeros_like(q)
            acc = None
            for hh, (qm, vm) in enumerate(((jnp.where(lo, q, qz), vlo), (jnp.where(lo, qz, q), vhi))):
                s = _dot_nt(qm, kcd) + t_ref[2 * pr + hh, :, :ncol]
                m = jnp.max(s, axis=-1, keepdims=True)
                e = jnp.exp2(s - m)
                l = jnp.sum(e, axis=-1, keepdims=True)
                p = e * jnp.where(m > 0.5 * NEG, 1.0 / l, 0.0)
                psum = p if psum is None else psum + p
                pv = jnp.dot(p.astype(BF16), vm, preferred_element_type=F32)
                acc = pv if acc is None else acc + pv
            o_ref[0, :, pr * LANES:(pr + 1) * LANES] = acc.astype(o_ref.dtype)

        p_hi = psum.astype(BF16)
        p_lo = (psum - p_hi.astype(F32)).astype(BF16)
        ov = ov_ref[:nsel, :ncol]
        score = (_dot_nt(ov, p_hi) + _dot_nt(ov, p_lo))
        blk = lax.broadcasted_iota(jnp.int32, (nsel, 1), 0)
        pos = i * tq + lax.broadcasted_iota(jnp.int32, (1, tq), 1)
        cur = pos >> int(math.log2(SLC_BLOCK))
        forced = (blk == 0) | ((cur - blk >= 0) & (cur - blk < SLC_LOCAL))
        score = jnp.where(forced, 1e9, jnp.where(blk > cur, -1e9, score))
        sub = lax.broadcasted_iota(jnp.int32, (8, 1), 0)
        sel_rows = []
        for g8 in range(nsel // 8):
            sg = score[g8 * 8:(g8 + 1) * 8]
            rank = jnp.zeros(sg.shape, F32)
            for mp in range(nsel):
                row = score[mp:mp + 1]
                if mp < g8 * 8:
                    ahead = jnp.where(row >= sg, 1.0, 0.0)
                elif mp >= (g8 + 1) * 8:
                    ahead = jnp.where(row > sg, 1.0, 0.0)
                else:
                    ahead = jnp.where(sub > mp - g8 * 8, jnp.where(row >= sg, 1.0, 0.0), jnp.where(row > sg, 1.0, 0.0))
                rank = rank + ahead
            sel_rows.append(jnp.where(rank < float(min(SLC_TOPK, n_slc)), 0.0, NEG))
        if nsel < n_slc:
            sel_rows.append(jnp.full((n_slc - nsel, tq), NEG, F32))
        pad = LANES // n_slc
        selb = jnp.concatenate(sel_rows * pad, axis=0)
        sb_ref[0, 0] = selb.T.astype(BF16)

    half = pl.num_programs(2) // 2
    pl.when(i < half)(lambda: attend_and_rank(n_col // 2, n_slc // 2))
    pl.when(i >= half)(lambda: attend_and_rank(n_col, n_slc))


def _cmp_attention(qb, kcd, vcd, table, ov_t, tq=512):
    b, s, _ = qb.shape
    n_col = kcd.shape[2]
    gw = R_B * HEAD_DIM
    return pl.pallas_call(
        functools.partial(_cmp_kernel, tq=tq),
        grid=(b, G_B, s // tq),
        in_specs=[pl.BlockSpec((1, tq, gw), lambda i, g, j: (i, j, g)),
                  pl.BlockSpec((1, 1, n_col, LANES), lambda i, g, j: (i, g, 0, 0)),
                  pl.BlockSpec((1, 1, n_col, LANES), lambda i, g, j: (i, g, 0, 0)),
                  pl.BlockSpec((R_B, tq, n_col), lambda i, g, j: (g, j, 0)),
                  pl.BlockSpec(ov_t.shape, lambda i, g, j: (0, 0))],
        out_specs=[pl.BlockSpec((1, tq, gw), lambda i, g, j: (i, j, g)),
                   pl.BlockSpec((1, 1, tq, LANES), lambda i, g, j: (i, g, j, 0))],
        out_shape=[jax.ShapeDtypeStruct((b, s, D_B), BF16),
                   jax.ShapeDtypeStruct((b, G_B, s, LANES), BF16)],
        compiler_params=_cparams(("parallel", "parallel", "parallel")),
        name="nsa_cmp_topk",
    )(qb, kcd, vcd, table, ov_t)


def _sel_kernel(q_ref, sb_ref, k_ref, v_ref, t_ref, o_ref, kaug, m_ref, l_ref, acc_ref, *, nblk):
    lo = _lo_mask()
    s_len = nblk * BLK
    n_diag = t_ref.shape[1] - 1
    rowblk = lax.broadcasted_iota(jnp.int32, (s_len, LANES), 0) >> int(math.log2(SLC_BLOCK))
    lane = lax.broadcasted_iota(jnp.int32, (s_len, LANES), 1)
    onehot = jnp.where(lane - HEAD_DIM == rowblk, 1.0, 0.0).astype(BF16)
    kaug[...] = jnp.where(lo, k_ref[0], onehot)
    m_ref[...] = jnp.full(m_ref.shape, NEG, F32)
    l_ref[...] = jnp.zeros(l_ref.shape, F32)
    acc_ref[...] = jnp.zeros(acc_ref.shape, F32)
    kb = 4
    qb = 2
    sub = kb // qb
    kt_rows, qt_rows = kb * BLK, qb * BLK

    def ktile(t, carry):
        k0 = pl.multiple_of(t * kt_rows, kt_rows)
        kt = kaug[pl.ds(k0, kt_rows), :]
        vt = v_ref[0, pl.ds(k0, kt_rows), :]

        def qstep(mp, c):
            scores = []
            for u in range(sub):
                r0 = pl.multiple_of((sub * mp + u) * qt_rows, qt_rows)
                sbf = sb_ref[0, 0, pl.ds(r0, qt_rows), :].astype(F32)
                rows = []
                for pr in range(R_B // 2):
                    q = q_ref[0, pl.ds(r0, qt_rows), pr * LANES:(pr + 1) * LANES].astype(F32)
                    rows.append(jnp.where(lo, q, sbf))
                    rows.append(jnp.where(lo, pltpu.roll(q, HEAD_DIM, 1), sbf))
                qst = jnp.concatenate(rows, axis=0).astype(BF16)
                scores.append(_dot_nt(qst, kt))
            for u in range(sub):
                r0 = pl.multiple_of((sub * mp + u) * qt_rows, qt_rows)
                s = scores[u]
                ps, alphas = [], []
                for h in range(R_B):
                    for a in range(qb):
                        n = qb * (sub * mp + u) + a
                        bias = jnp.concatenate(
                            [t_ref[h, jnp.clip(n - kb * t - c + 1, 0, n_diag)] for c in range(kb)], axis=1)
                        lo_r = h * qt_rows + a * BLK
                        st = pl.ds(r0 + a * BLK, BLK)
                        sh = s[lo_r:lo_r + BLK] + bias
                        m_prev = m_ref[h, st, :]
                        m_new = jnp.maximum(m_prev, jnp.max(sh, axis=-1, keepdims=True))
                        alpha = jnp.exp2(m_prev - m_new)
                        p = jnp.exp2(sh - jnp.concatenate([m_new] * kb, axis=1))
                        psum = p[:, :LANES]
                        for c in range(1, kb):
                            psum = psum + p[:, c * LANES:(c + 1) * LANES]
                        l_ref[h, st, :] = alpha * l_ref[h, st, :] + psum
                        m_ref[h, st, :] = m_new
                        ps.append(p.astype(BF16))
                        alphas.append(alpha)
                pv = jnp.dot(jnp.concatenate(ps, axis=0), vt, preferred_element_type=F32)
                for h in range(R_B):
                    for a in range(qb):
                        lo_r = h * qt_rows + a * BLK
                        st = pl.ds(r0 + a * BLK, BLK)
                        acc_ref[h, st, :] = alphas[qb * h + a] * acc_ref[h, st, :] + pv[lo_r:lo_r + BLK]
            return c

        lax.fori_loop(t, nblk // kb, qstep, 0)
        return carry

    lax.fori_loop(0, nblk // kb, ktile, 0)

    def finish(n, carry):
        st = pl.ds(pl.multiple_of(n * BLK, BLK), BLK)
        for pr in range(R_B // 2):
            inv = [1.0 / jnp.sum(l_ref[2 * pr + hh, st, :], axis=-1, keepdims=True) for hh in range(2)]
            o = jnp.where(lo, acc_ref[2 * pr, st, :] * inv[0], acc_ref[2 * pr + 1, st, :] * inv[1])
            o_ref[0, st, pr * LANES:(pr + 1) * LANES] = o.astype(o_ref.dtype)
        return carry

    lax.fori_loop(0, nblk, finish, 0)


def _sel_attention(qb, selb, ks_dup, vs_dup, table):
    b, s, _ = qb.shape
    gw = R_B * HEAD_DIM
    n_e = table.shape[1]
    one = pl.Buffered(1)
    return pl.pallas_call(
        functools.partial(_sel_kernel, nblk=s // BLK),
        grid=(b, G_B),
        in_specs=[pl.BlockSpec((1, s, gw), lambda i, g: (i, 0, g), pipeline_mode=one),
                  pl.BlockSpec((1, 1, s, LANES), lambda i, g: (i, g, 0, 0), pipeline_mode=one),
                  pl.BlockSpec((1, s, LANES), lambda i, g: (i, 0, g), pipeline_mode=one),
                  pl.BlockSpec((1, s, LANES), lambda i, g: (i, 0, g), pipeline_mode=one),
                  pl.BlockSpec((R_B, n_e, BLK, BLK), lambda i, g: (g, 0, 0, 0), pipeline_mode=one)],
        out_specs=pl.BlockSpec((1, s, gw), lambda i, g: (i, 0, g)),
        out_shape=jax.ShapeDtypeStruct((b, s, D_B), BF16),
        scratch_shapes=[pltpu.VMEM((s, LANES), BF16)] + [pltpu.VMEM((R_B, s, LANES), F32)] * 3,
        compiler_params=_cparams(("parallel", "parallel")),
        name="nsa_selected",
    )(qb, selb, ks_dup, vs_dup, table)


def _combine_kernel(x_ref, oa, za, oc, osl, ow, gb, zb, wo, ex, gate, lng, lnb, out_ref, *, alpha):
    f = lambda r: r[0].astype(F32)
    z = f(za)
    mix_a = f(oa) * (z * jax.nn.sigmoid(z))
    g = jax.nn.sigmoid(gb[0])
    g_hi = g.astype(BF16)
    g_lo = (g - g_hi.astype(F32)).astype(BF16)
    gx = (jnp.dot(g_hi, ex[...], preferred_element_type=F32)
          + jnp.dot(g_lo, ex[...], preferred_element_type=F32))
    ob = gx[:, :D_B] * f(oc) + gx[:, D_B:2 * D_B] * f(osl) + gx[:, 2 * D_B:] * f(ow)
    z = f(zb)
    mix_b = ob * (z * jax.nn.sigmoid(z))
    y = (jnp.dot(mix_a.astype(BF16), wo[:D_A], preferred_element_type=F32)
         + jnp.dot(mix_b.astype(BF16), wo[D_A:], preferred_element_type=F32))
    r = alpha * x_ref[0] + (1.0 + gate[0]) * y
    mu = jnp.mean(r, axis=-1, keepdims=True)
    rc = r - mu
    var = jnp.mean(rc * rc, axis=-1, keepdims=True)
    out_ref[0] = rc * lax.rsqrt(var + LN_EPS) * lng[...] + lnb[...]


def _combine(x, o_a, za, o_cmp, o_slc, o_win, gb, zb, w_out, expand, gate, ln_g, ln_b, alpha, tm=512):
    b, s, d = x.shape
    bs = lambda n: pl.BlockSpec((1, tm, n), lambda i, j: (i, j, 0))
    full = lambda a: pl.BlockSpec(a.shape, lambda i, j: (0,) * a.ndim)
    return pl.pallas_call(
        functools.partial(_combine_kernel, alpha=alpha),
        grid=(b, s // tm),
        in_specs=[bs(d)] + [bs(D_A)] * 2 + [bs(D_B)] * 3 + [bs(LANES), bs(D_B), full(w_out), full(expand),
                  pl.BlockSpec((1, 1, d), lambda i, j: (i, 0, 0)), full(ln_g), full(ln_b)],
        out_specs=bs(d),
        out_shape=jax.ShapeDtypeStruct((b, s, d), F32),
        compiler_params=_cparams(("parallel", "parallel")),
        name="merge_out_proj_ln",
    )(x, o_a, za, o_cmp, o_slc, o_win, gb, zb, w_out, expand, gate, ln_g, ln_b)


def _pack_w_in(w_in):
    n_gb = 3 * H_B
    main = w_in[..., :_N_MAIN]
    gbw = w_in[..., _N_MAIN:_N_MAIN + n_gb]
    zbw = w_in[..., _N_MAIN + n_gb:]
    pad = jnp.zeros(w_in.shape[:-1] + (LANES - n_gb,), w_in.dtype)
    return jnp.concatenate([main, gbw, pad, zbw], axis=-1).astype(BF16)


def _gate_expand():
    e = np.zeros((LANES, 3 * D_B), np.float32)
    for i in range(3):
        for h in range(H_B):
            e[i * H_B + h, i * D_B + h * HEAD_DIM:i * D_B + (h + 1) * HEAD_DIM] = 1.0
    return jnp.asarray(e, BF16)


def _overlap_t(n_col, n_slc):
    cs = np.arange(n_col)[None, :] * CMP_STRIDE
    ss = np.arange(n_slc)[:, None] * SLC_BLOCK
    return jnp.asarray(((cs < ss + SLC_BLOCK) & (cs + CMP_LEN > ss)).astype(np.float32), BF16)


def _layer(x, shift, scl, gate, w_in_p, w_out_b, pe_b, w1_b, w2d_b, ln_g, ln_b, tables, consts, alpha):
    b, s, d = x.shape
    (qa, ka, va, za, qb, kc_in, vc_in, ks, vs, kw, vw, zb, gb) = _project(x, shift, scl, w_in_p)

    o_a = _dilated_attention(qa, ka, va, tables["dilated"])

    kcd, vcd = _compress(kc_in, vc_in, w1_b, w2d_b, pe_b)
    o_cmp, selb = _cmp_attention(qb, kcd, vcd, tables["cmp"], consts["overlap_t"])
    o_slc = _sel_attention(qb, selb, ks, vs, tables["sel"])
    o_win = _win_attention(qb, kw, vw, tables["win"], n_prev=-(-(WIN - 1) // BLK))

    return _combine(x, o_a, za, o_cmp, o_slc, o_win, gb, zb, w_out_b, consts["expand"], gate,
                    ln_g, ln_b, alpha)


def kernel(x, c, w_in, w_out, cmp_pe, cmp_w1, cmp_w2, w_ada, b_ada, ln_g, ln_b, rel_bias):
    b, s, d = x.shape
    depth = w_in.shape[0]
    alpha = (2 * depth) ** 0.25
    n_pair = D_A // LANES

    bb = rel_bias.astype(F32) * LOG2E
    def dilated_table(window, dil):
        t = _band_table(bb[:, :H_A], 1, window // dil, dil).reshape(n_pair, 2 * BLK, 2 * BLK)
        first = t.at[..., :BLK].set(NEG)
        return jnp.stack([t, first], axis=1)

    n_prev_win = -(-(WIN - 1) // BLK)
    tables = {
        "dilated": [dilated_table(w, dl) for w, dl in DILATED],
        "win": _band_table(bb[:, H_A:], n_prev_win, WIN - 1, 1).reshape(G_B, R_B * BLK, (n_prev_win + 1) * BLK),
        "cmp": _cmp_table(bb[:, H_A:], s),
        "sel": _diag_table(bb[:, H_A:], 13),
    }
    consts = {"expand": _gate_expand(), "overlap_t": _overlap_t(s // CMP_STRIDE, s // SLC_BLOCK)}

    mod = _ada_mod(c, w_ada, b_ada)
    w_in_p = _pack_w_in(w_in)
    w_out_b = w_out.astype(BF16)
    w1_b = cmp_w1.astype(BF16)
    w2d_b = jnp.concatenate([cmp_w2, cmp_w2], axis=-1).astype(BF16)
    pe_b = jnp.broadcast_to(cmp_pe.reshape(depth, 2, 1, CMP_LEN * HEAD_DIM), (depth, 2, 8, CMP_LEN * HEAD_DIM)).astype(BF16)

    for l in range(depth):
        m = mod[l, :b]
        shift, scl, gate = (m[:, i * d:(i + 1) * d].reshape(b, 1, d) for i in range(3))
        x = _layer(x, shift, scl, gate, w_in_p[l], w_out_b[l], pe_b[l], w1_b[l], w2d_b[l],
                   ln_g[l].reshape(1, d), ln_b[l].reshape(1, d), tables, consts, alpha)
    return x
```

```python
import functools
import math

import numpy as np
import jax
import jax.numpy as jnp
from jax import lax
from jax.experimental import pallas as pl
from jax.experimental.pallas import tpu as pltpu

F32 = jnp.float32
BF16 = jnp.bfloat16

D_MODEL = 1024
HEAD_DIM = 64
H_A = 8
H_B = 8
G_B = 2
R_B = H_B // G_B
D_A = H_A * HEAD_DIM
D_B = H_B * HEAD_DIM
DILATED = ((128, 1), (512, 4), (2048, 16))
CMP_LEN = 32
CMP_STRIDE = 16
CMP_HIDDEN = 256
SLC_BLOCK = 64
SLC_TOPK = 16
SLC_LOCAL = 2
WIN = 512
N_BUCKETS = 32
MAX_DIST = 2048
LN_EPS = 1e-5
NEG = -1e30
LOG2E = 1.4426950408889634

LANES = 128
BLK = 128
VMEM_LIMIT = 56 * 1024 * 1024

_C_QA, _C_KA, _C_VA, _C_ZA, _C_QB = 0, 512, 1024, 1536, 2048
_C_NARROW = 2560
_C_ZB, _C_END = 3456, 3968
_N_MAIN = 3328


def _cparams(sem):
    return pltpu.CompilerParams(dimension_semantics=sem, vmem_limit_bytes=VMEM_LIMIT)


def _lo_mask():
    return lax.broadcasted_iota(jnp.int32, (1, LANES), 1) < HEAD_DIM


def _dot_nt(a, b):
    return lax.dot_general(a, b, (((1,), (1,)), ((), ())), preferred_element_type=F32)


def _bucket_np(dist):
    d = np.maximum(dist, 0)
    max_exact = N_BUCKETS // 2
    large = max_exact + (np.log(np.maximum(d, 1).astype(np.float32) / np.float32(max_exact))
                         / np.float32(math.log(MAX_DIST / max_exact))
                         * np.float32(N_BUCKETS - max_exact)).astype(np.int32)
    return np.where(d < max_exact, d, np.minimum(large, N_BUCKETS - 1)).astype(np.int32)


def _masked_bias(bias_by_bucket, dist, valid):
    onehot = np.eye(N_BUCKETS, dtype=np.float32)[_bucket_np(dist)]
    t = jnp.einsum("...k,kh->h...", onehot, bias_by_bucket, precision=lax.Precision.HIGHEST)
    return jnp.where(jnp.asarray(valid)[None], t, NEG)


def _toeplitz_kernel(v_ref, o_ref):
    cols = o_ref.shape[-1]
    for e in range(o_ref.shape[1]):
        x = jnp.broadcast_to(v_ref[0, e:e + 1, :], (BLK, v_ref.shape[-1]))
        o_ref[0, e] = pltpu.roll(x, 0, 1, stride=1, stride_axis=0)[:, :cols]


def _toeplitz(h, cols):
    n_h, n_e, n = h.shape
    width = 1 << (n - 1).bit_length()
    vec = jnp.concatenate([h[..., BLK - 1:], jnp.zeros((n_h, n_e, width - n), F32), h[..., :BLK - 1]], axis=-1)
    return pl.pallas_call(
        _toeplitz_kernel,
        grid=(n_h,),
        in_specs=[pl.BlockSpec((1, n_e, width), lambda a: (a, 0, 0))],
        out_specs=pl.BlockSpec((1, n_e, BLK, cols), lambda a: (a, 0, 0, 0)),
        out_shape=jax.ShapeDtypeStruct((n_h, n_e, BLK, cols), F32),
        compiler_params=_cparams(("parallel",)),
        name="toeplitz_table",
    )(vec)


def _band_table(bias_by_bucket, n_prev, window, scale):
    w = (n_prev + 1) * BLK
    dist = n_prev * BLK - (np.arange(BLK - 1 + w) - (BLK - 1))
    h = _masked_bias(bias_by_bucket, dist * scale, (dist >= 0) & (dist <= window))
    return _toeplitz(h[:, None], w)[:, 0]


def _diag_table(bias_by_bucket, n_diag):
    delta = np.arange(-1, n_diag + 1)[:, None]
    dist = delta * BLK - (np.arange(2 * BLK - 1)[None, :] - (BLK - 1))
    return _toeplitz(_masked_bias(bias_by_bucket, dist, dist >= 0), BLK)


def _cmp_table_kernel(base_ref, o_ref, *, rows):
    i = pl.program_id(1)
    n_col = o_ref.shape[-1]
    width = base_ref.shape[-1]
    base = base_ref[0]
    for j in range(rows // CMP_STRIDE):
        chunk = i * (rows // CMP_STRIDE) + j
        shift = (chunk + width - (n_col - 1)) % width
        o_ref[0, j * CMP_STRIDE:(j + 1) * CMP_STRIDE, :] = pltpu.roll(base, shift, 1)[:, :n_col]


def _cmp_table(bias_by_bucket, seq, rows=1024):
    n_col = seq // CMP_STRIDE
    t = np.arange(2 * n_col) - (n_col - 1)
    dist = -CMP_STRIDE * t[None, :] + np.arange(CMP_STRIDE)[:, None] - (CMP_LEN - 1)
    base = _masked_bias(bias_by_bucket, dist, dist >= 0)
    h = base.shape[0]
    return pl.pallas_call(
        functools.partial(_cmp_table_kernel, rows=rows),
        grid=(h, seq // rows),
        in_specs=[pl.BlockSpec((1, CMP_STRIDE, 2 * n_col), lambda a, i: (a, 0, 0))],
        out_specs=pl.BlockSpec((1, rows, n_col), lambda a, i: (a, i, 0)),
        out_shape=jax.ShapeDtypeStruct((h, seq, n_col), F32),
        compiler_params=_cparams(("parallel", "parallel")),
        name="cmp_bias_table",
    )(base)


def _ada_kernel(c_ref, w_ref, b_ref, o_ref):
    o_ref[0] = jnp.dot(c_ref[...], w_ref[0].astype(BF16), preferred_element_type=F32) + b_ref[0]


def _ada_mod(c, w_ada, b_ada):
    depth, d, d3 = w_ada.shape
    cp = jnp.zeros((8, d), BF16).at[:c.shape[0]].set(c.astype(BF16))
    nj = d3 // d
    return pl.pallas_call(
        _ada_kernel,
        grid=(depth, nj),
        in_specs=[pl.BlockSpec((8, d), lambda l, j: (0, 0)),
                  pl.BlockSpec((1, d, d), lambda l, j: (l, 0, j)),
                  pl.BlockSpec((1, 1, d), lambda l, j: (l, 0, j))],
        out_specs=pl.BlockSpec((1, 8, d), lambda l, j: (l, 0, j)),
        out_shape=jax.ShapeDtypeStruct((depth, 8, d3), F32),
        compiler_params=_cparams(("parallel", "parallel")),
        name="ada_mod",
    )(cp, w_ada, b_ada.reshape(depth, 1, d3))


def _proj_kernel(x_ref, sh_ref, sc_ref, w_ref, *refs, dils):
    n_lay = len(dils)
    qa, ka, va = refs[:n_lay], refs[n_lay:2 * n_lay], refs[2 * n_lay:3 * n_lay]
    za, qb, kc, vc, ks, vs, kw, vw, zb, gb, h_ref, xs = refs[3 * n_lay:]
    tm = h_ref.shape[0]
    h_ref[...] = (x_ref[0] * (1.0 + sc_ref[0]) + sh_ref[0]).astype(BF16)

    def mm(c0, n):
        return jnp.dot(h_ref[...], w_ref[:, c0:c0 + n], preferred_element_type=F32)

    def emit_regrouped(a, outs):
        outs[0][0] = a.astype(BF16)
        n_slab = a.shape[1] // LANES
        for c in range(n_slab):
            xs[0, c] = a[:, c * LANES:(c + 1) * LANES]
        d_prev = 1
        for lvl, (ref, dil) in enumerate(zip(outs[1:], dils[1:])):
            ratio, len_prev, len_new = dil // d_prev, tm // d_prev, tm // dil
            src, dst = xs.at[lvl % 2], xs.at[(lvl + 1) % 2]
            keep = lvl + 2 < len(dils)
            for rp in range(d_prev):
                for rs in range(ratio):
                    r = rp + d_prev * rs
                    for c in range(n_slab):
                        rows = src[c, pl.ds(rp * len_prev + rs, len_new, stride=ratio), :]
                        ref[0, r, :, c * LANES:(c + 1) * LANES] = rows.astype(BF16)
                        if keep:
                            dst[c, r * len_new:(r + 1) * len_new, :] = rows
            d_prev = dil

    qs = HEAD_DIM ** -0.5 * LOG2E
    emit_regrouped(mm(_C_QA, D_A) * qs, qa)
    emit_regrouped(mm(_C_KA, D_A), ka)
    emit_regrouped(mm(_C_VA, D_A), va)
    za[0] = mm(_C_ZA, D_A).astype(BF16)
    qb[0] = (mm(_C_QB, D_B) * qs).astype(BF16)
    lo = _lo_mask()
    narrow = mm(_C_NARROW, 7 * LANES)
    piece = lambda i: narrow[:, i * LANES:(i + 1) * LANES]
    for ref, i in ((kc, 0), (vc, 1)):
        xs[0, 0] = piece(i)
        for t in range(0, CMP_STRIDE, 2):
            x0 = xs[0, 0, pl.ds(t, tm // CMP_STRIDE, stride=CMP_STRIDE), :]
            x1 = xs[0, 0, pl.ds(t + 1, tm // CMP_STRIDE, stride=CMP_STRIDE), :]
            cols = slice(t * HEAD_DIM, (t + 2) * HEAD_DIM)
            ref[0, 0, :, cols] = jnp.where(lo, x0, pltpu.roll(x1, HEAD_DIM, 1)).astype(BF16)
            ref[0, 1, :, cols] = jnp.where(lo, pltpu.roll(x0, HEAD_DIM, 1), x1).astype(BF16)
    for ref, i in ((ks, 2), (vs, 3), (kw, 4), (vw, 5)):
        a = piece(i)
        r = pltpu.roll(a, HEAD_DIM, 1)
        ref[0, :, :LANES] = jnp.where(lo, a, r).astype(BF16)
        ref[0, :, LANES:] = jnp.where(lo, r, a).astype(BF16)
    gb[0] = piece(6)
    zb[0] = mm(_C_ZB, D_B).astype(BF16)


def _project(x, shift, scl, w, tm=512):
    b, s, d = x.shape
    dils = tuple(dl for _, dl in DILATED)
    bs = lambda n: pl.BlockSpec((1, tm, n), lambda i, j: (i, j, 0))
    gs = pl.BlockSpec((1, G_B, tm // CMP_STRIDE, CMP_STRIDE * HEAD_DIM), lambda i, j: (i, 0, j, 0))
    sd = lambda n, dt=BF16: jax.ShapeDtypeStruct((b, s, n), dt)
    gd = jax.ShapeDtypeStruct((b, G_B, s // CMP_STRIDE, CMP_STRIDE * HEAD_DIM), BF16)
    mod = pl.BlockSpec((1, 1, d), lambda i, j: (i, 0, 0))
    lay_specs = [bs(D_A)] + [pl.BlockSpec((1, dl, tm // dl, D_A), lambda i, j: (i, 0, j, 0)) for dl in dils[1:]]
    lay_shapes = [sd(D_A)] + [jax.ShapeDtypeStruct((b, dl, s // dl, D_A), BF16) for dl in dils[1:]]
    outs = pl.pallas_call(
        functools.partial(_proj_kernel, dils=dils),
        grid=(b, s // tm),
        in_specs=[pl.BlockSpec((1, tm, d), lambda i, j: (i, j, 0)), mod, mod,
                  pl.BlockSpec((d, _C_END), lambda i, j: (0, 0))],
        out_specs=lay_specs * 3 + [bs(D_A), bs(D_B), gs, gs,
                                   bs(2 * LANES), bs(2 * LANES), bs(2 * LANES), bs(2 * LANES), bs(D_B), bs(LANES)],
        out_shape=lay_shapes * 3 + [sd(D_A), sd(D_B), gd, gd,
                                    sd(2 * LANES), sd(2 * LANES), sd(2 * LANES), sd(2 * LANES), sd(D_B), sd(LANES, F32)],
        scratch_shapes=[pltpu.VMEM((tm, d), BF16), pltpu.VMEM((2, D_A // LANES, tm, LANES), F32)],
        compiler_params=_cparams(("parallel", "parallel")),
        name="in_proj",
    )(x, shift, scl, w)
    n = len(dils)
    return (outs[:n], outs[n:2 * n], outs[2 * n:3 * n], *outs[3 * n:])


def _dilated_kernel(*refs, seq, unroll):
    n_pat = len(DILATED)
    qkv = refs[:3 * n_pat]
    t_refs = refs[3 * n_pat:4 * n_pat]
    o_ref = refs[4 * n_pat]
    qlo, qhi, kp, vp, o_r, m_r, l_r, o_t, m_t, l_t, m_acc, n_acc, d_acc = refs[4 * n_pat + 1:]
    lo = _lo_mask()
    nblk = seq // BLK
    chunk = 512
    kp[:BLK, :] = jnp.zeros((BLK, LANES), BF16)
    vp[:BLK, :] = jnp.zeros((BLK, LANES), BF16)

    for pi, ((_, dil), t_ref) in enumerate(zip(DILATED, t_refs)):
        q_ref, k_ref, v_ref = qkv[pi], qkv[n_pat + pi], qkv[2 * n_pat + pi]
        rl = seq // dil
        nb = rl // BLK
        for r in range(dil):
            idx = (0,) if dil == 1 else (0, r)
            q = q_ref[idx]
            zero = jnp.zeros_like(q)
            qlo[r * rl:(r + 1) * rl, :] = jnp.where(lo, q, zero)
            qhi[r * rl:(r + 1) * rl, :] = jnp.where(lo, zero, q)
            kp[BLK + r * rl:BLK + (r + 1) * rl, :] = k_ref[idx]
            vp[BLK + r * rl:BLK + (r + 1) * rl, :] = v_ref[idx]
        o_dst, m_dst, l_dst = (n_acc, m_acc, d_acc) if dil == 1 else (o_r, m_r, l_r)

        def blocks(it, carry, t_ref=t_ref, nb=nb, o_dst=o_dst, m_dst=m_dst, l_dst=l_dst):
            for u in range(unroll):
                g = it * unroll + u
                first = jnp.where((g & (nb - 1)) == 0, 1, 0)
                q0 = pl.multiple_of(g * BLK, BLK)
                qs = jnp.concatenate([qlo[pl.ds(q0, BLK), :], qhi[pl.ds(q0, BLK), :]], axis=0)
                s = _dot_nt(qs, kp[pl.ds(q0, 2 * BLK), :]) + t_ref[0, first]
                m = jnp.max(s, axis=-1, keepdims=True)
                p = jnp.exp2(s - m)
                l = jnp.sum(p, axis=-1, keepdims=True)
                pv = jnp.dot(p.astype(BF16), vp[pl.ds(q0, 2 * BLK), :], preferred_element_type=F32)
                o_dst[pl.ds(q0, BLK), :] = jnp.where(lo, pv[:BLK], pv[BLK:])
                m_dst[pl.ds(q0, BLK), :] = jnp.where(lo, m[:BLK], m[BLK:])
                l_dst[pl.ds(q0, BLK), :] = jnp.where(lo, l[:BLK], l[BLK:])
            return carry

        lax.fori_loop(0, nblk // unroll, blocks, 0)
        if dil == 1:
            continue

        for r in range(dil):
            for src, dst in ((o_r, o_t), (m_r, m_t), (l_r, l_t)):
                dst[pl.ds(r, rl, stride=dil), :] = src[r * rl:(r + 1) * rl, :]

        def fold(c, carry, last=(pi == n_pat - 1)):
            rows = pl.ds(pl.multiple_of(c * chunk, chunk), chunk)
            m_old, m_new = m_acc[rows, :], m_t[rows, :]
            mx = jnp.maximum(m_old, m_new)
            a, b_ = jnp.exp2(m_old - mx), jnp.exp2(m_new - mx)
            n_new = a * n_acc[rows, :] + b_ * o_t[rows, :]
            d_new = a * d_acc[rows, :] + b_ * l_t[rows, :]
            if last:
                o_ref[0, rows, :] = (n_new * (1.0 / d_new)).astype(o_ref.dtype)
            else:
                m_acc[rows, :] = mx
                n_acc[rows, :] = n_new
                d_acc[rows, :] = d_new
            return carry

        lax.fori_loop(0, seq // chunk, fold, 0)


def _dilated_attention(qa, ka, va, tables, unroll=32):
    assert DILATED[0][1] == 1 and all(dl > 1 for _, dl in DILATED[1:])
    b, s, c = qa[0].shape
    xs = pl.BlockSpec((1, s, LANES), lambda i, j: (i, 0, j))
    lay = [xs] + [pl.BlockSpec((1, dl, s // dl, LANES), lambda i, j: (i, 0, 0, j)) for _, dl in DILATED[1:]]
    ts = pl.BlockSpec((1, 2, 2 * BLK, 2 * BLK), lambda i, j: (j, 0, 0, 0))
    big = lambda rows, dt: pltpu.VMEM((rows, LANES), dt)
    return pl.pallas_call(
        functools.partial(_dilated_kernel, seq=s, unroll=unroll),
        grid=(b, c // LANES),
        in_specs=lay * 3 + [ts] * len(DILATED),
        out_specs=xs,
        out_shape=jax.ShapeDtypeStruct((b, s, c), BF16),
        scratch_shapes=[big(s, BF16)] * 2 + [big(s + BLK, BF16)] * 2 + [big(s, F32)] * 9,
        compiler_params=_cparams(("parallel", "parallel")),
        name="dilated_attn",
    )(*qa, *ka, *va, *tables)


def _win_kernel(q_ref, k_ref, v_ref, t_ref, o_ref, qst, kaug, vp, *, seq, n_prev, unroll):
    lo = _lo_mask()
    lane = lax.broadcasted_iota(jnp.int32, (1, LANES), 1)
    flag = lane == HEAD_DIM
    pad = n_prev * BLK
    span = (n_prev + 1) * BLK
    negflag = jnp.where(flag, NEG, 0.0)
    for pr in range(R_B // 2):
        q = q_ref[0, :, pr * LANES:(pr + 1) * LANES].astype(F32)
        qst[2 * pr] = jnp.where(lo, q, negflag).astype(BF16)
        qst[2 * pr + 1] = jnp.where(lo, pltpu.roll(q, HEAD_DIM, 1), negflag).astype(BF16)
    k = k_ref[0]
    kaug[:pad, :] = jnp.broadcast_to(jnp.where(flag, 1.0, 0.0), (pad, LANES)).astype(BF16)
    kaug[pad:, :] = jnp.where(lo, k, jnp.zeros_like(k))
    vp[:pad, :] = jnp.zeros((pad, LANES), BF16)
    vp[pad:, :] = jnp.where(lo, v_ref[0], jnp.where(flag, 1.0, 0.0).astype(BF16))

    def blocks(it, carry):
        for u in range(unroll):
            n = it * unroll + u
            r0 = pl.multiple_of(n * BLK, BLK)
            qs = jnp.concatenate([qst[h, pl.ds(r0, BLK), :] for h in range(R_B)], axis=0)
            s = _dot_nt(qs, kaug[pl.ds(r0, span), :]) + t_ref[0]
            m = jnp.max(s, axis=-1, keepdims=True)
            p = jnp.exp2(s - m)
            pv = jnp.dot(p.astype(BF16), vp[pl.ds(r0, span), :], preferred_element_type=F32)
            pv = pv * (1.0 / pv[:, HEAD_DIM:HEAD_DIM + 1])
            for pr in range(R_B // 2):
                even, odd = pv[2 * pr * BLK:(2 * pr + 1) * BLK], pv[(2 * pr + 1) * BLK:(2 * pr + 2) * BLK]
                o = jnp.where(lo, even, pltpu.roll(odd, HEAD_DIM, 1))
                o_ref[0, pl.ds(r0, BLK), pr * LANES:(pr + 1) * LANES] = o.astype(o_ref.dtype)
        return carry

    lax.fori_loop(0, seq // BLK // unroll, blocks, 0)


def _win_attention(qb, kw_dup, vw_dup, table, n_prev, unroll=8):
    b, s, _ = qb.shape
    gw = R_B * HEAD_DIM
    w = table.shape[-1]
    return pl.pallas_call(
        functools.partial(_win_kernel, seq=s, n_prev=n_prev, unroll=unroll),
        grid=(b, G_B),
        in_specs=[pl.BlockSpec((1, s, gw), lambda i, g: (i, 0, g)),
                  pl.BlockSpec((1, s, LANES), lambda i, g: (i, 0, g)),
                  pl.BlockSpec((1, s, LANES), lambda i, g: (i, 0, g)),
                  pl.BlockSpec((1, R_B * BLK, w), lambda i, g: (g, 0, 0))],
        out_specs=pl.BlockSpec((1, s, gw), lambda i, g: (i, 0, g)),
        out_shape=jax.ShapeDtypeStruct((b, s, D_B), BF16),
        scratch_shapes=[pltpu.VMEM((R_B, s, LANES), BF16)] + [pltpu.VMEM((s + n_prev * BLK, LANES), BF16)] * 2,
        compiler_params=_cparams(("parallel", "parallel")),
        name="nsa_window",
    )(qb, kw_dup, vw_dup, table)


def _compress_kernel(kin, vin, w1, w2, pe, kout, vout):
    n_chunk = kin.shape[2]
    half = w1.shape[1] // 2
    rows = lax.broadcasted_iota(jnp.int32, (n_chunk, 1), 0)
    for t, (xin, out) in enumerate(((kin, kout), (vin, vout))):
        x = xin[0, 0]
        a = jnp.dot(x, w1[t, :half], preferred_element_type=F32)
        b = jnp.dot(x, w1[t, half:], preferred_element_type=F32)
        c = jnp.dot(pe[t], w1[t], preferred_element_type=F32)[0:1]
        hid = a + pltpu.roll(b, n_chunk - 1, 0) + c
        act = hid * jax.nn.sigmoid(hid)
        o = jnp.dot(act.astype(BF16), w2[t], preferred_element_type=F32)
        out[0, 0] = jnp.where(rows < n_chunk - 1, o, 0.0).astype(BF16)


def _compress(kc_in, vc_in, w1, w2d, pe):
    b, g, n_chunk, f = kc_in.shape
    xs = pl.BlockSpec((1, 1, n_chunk, f), lambda i, j: (i, j, 0, 0))
    os_ = pl.BlockSpec((1, 1, n_chunk, LANES), lambda i, j: (i, j, 0, 0))
    full = lambda a: pl.BlockSpec(a.shape, lambda i, j: (0,) * a.ndim)
    od = jax.ShapeDtypeStruct((b, g, n_chunk, LANES), BF16)
    return pl.pallas_call(
        _compress_kernel,
        grid=(b, g),
        in_specs=[xs, xs, full(w1), full(w2d), full(pe)],
        out_specs=[os_, os_],
        out_shape=[od, od],
        compiler_params=_cparams(("parallel", "parallel")),
        name="nsa_compress",
    )(kc_in, vc_in, w1, w2d, pe)


def _cmp_kernel(q_ref, kc_ref, vc_ref, t_ref, ov_ref, o_ref, sb_ref, *, tq):
    i = pl.program_id(2)
    lo = _lo_mask()
    n_col, n_slc = kc_ref.shape[2], ov_ref.shape[0]

    def attend_and_rank(ncol, nsel):
        kcd = kc_ref[0, 0, :ncol, :]
        vcd = vc_ref[0, 0, :ncol, :]
        zero = jnp.zeros_like(vcd)
        vlo = jnp.where(lo, vcd, zero)
        vhi = jnp.where(lo, zero, vcd)
        psum = None
        for pr in range(R_B // 2):
            q = q_ref[0, :, pr * LANES:(pr + 1) * LANES]
            qz = jnp.zeros_like(q)
            acc = None
            for hh, (qm, vm) in enumerate(((jnp.where(lo, q, qz), vlo), (jnp.where(lo, qz, q), vhi))):
                s = _dot_nt(qm, kcd) + t_ref[2 * pr + hh, :, :ncol]
                m = jnp.max(s, axis=-1, keepdims=True)
                e = jnp.exp2(s - m)
                l = jnp.sum(e, axis=-1, keepdims=True)
                p = e * jnp.where(m > 0.5 * NEG, 1.0 / l, 0.0)
                psum = p if psum is None else psum + p
                pv = jnp.dot(p.astype(BF16), vm, preferred_element_type=F32)
                acc = pv if acc is None else acc + pv
            o_ref[0, :, pr * LANES:(pr + 1) * LANES] = acc.astype(o_ref.dtype)

        p_hi = psum.astype(BF16)
        p_lo = (psum - p_hi.astype(F32)).astype(BF16)
        ov = ov_ref[:nsel, :ncol]
        score = (_dot_nt(ov, p_hi) + _dot_nt(ov, p_lo))
        blk = lax.broadcasted_iota(jnp.int32, (nsel, 1), 0)
        pos = i * tq + lax.broadcasted_iota(jnp.int32, (1, tq), 1)
        cur = pos >> int(math.log2(SLC_BLOCK))
        forced = (blk == 0) | ((cur - blk >= 0) & (cur - blk < SLC_LOCAL))
        score = jnp.where(forced, 1e9, jnp.where(blk > cur, -1e9, score))
        sub = lax.broadcasted_iota(jnp.int32, (8, 1), 0)
        sel_rows = []
        for g8 in range(nsel // 8):
            sg = score[g8 * 8:(g8 + 1) * 8]
            rank = jnp.zeros(sg.shape, F32)
            for mp in range(nsel):
                row = score[mp:mp + 1]
                if mp < g8 * 8:
                    ahead = jnp.where(row >= sg, 1.0, 0.0)
                elif mp >= (g8 + 1) * 8:
                    ahead = jnp.where(row > sg, 1.0, 0.0)
                else:
                    ahead = jnp.where(sub > mp - g8 * 8, jnp.where(row >= sg, 1.0, 0.0), jnp.where(row > sg, 1.0, 0.0))
                rank = rank + ahead
            sel_rows.append(jnp.where(rank < float(min(SLC_TOPK, n_slc)), 0.0, NEG))
        if nsel < n_slc:
            sel_rows.append(jnp.full((n_slc - nsel, tq), NEG, F32))
        pad = LANES // n_slc
        selb = jnp.concatenate(sel_rows * pad, axis=0)
        sb_ref[0, 0] = selb.T.astype(BF16)

    half = pl.num_programs(2) // 2
    pl.when(i < half)(lambda: attend_and_rank(n_col // 2, n_slc // 2))
    pl.when(i >= half)(lambda: attend_and_rank(n_col, n_slc))


def _cmp_attention(qb, kcd, vcd, table, ov_t, tq=512):
    b, s, _ = qb.shape
    n_col = kcd.shape[2]
    gw = R_B * HEAD_DIM
    return pl.pallas_call(
        functools.partial(_cmp_kernel, tq=tq),
        grid=(b, G_B, s // tq),
        in_specs=[pl.BlockSpec((1, tq, gw), lambda i, g, j: (i, j, g)),
                  pl.BlockSpec((1, 1, n_col, LANES), lambda i, g, j: (i, g, 0, 0)),
                  pl.BlockSpec((1, 1, n_col, LANES), lambda i, g, j: (i, g, 0, 0)),
                  pl.BlockSpec((R_B, tq, n_col), lambda i, g, j: (g, j, 0)),
                  pl.BlockSpec(ov_t.shape, lambda i, g, j: (0, 0))],
        out_specs=[pl.BlockSpec((1, tq, gw), lambda i, g, j: (i, j, g)),
                   pl.BlockSpec((1, 1, tq, LANES), lambda i, g, j: (i, g, j, 0))],
        out_shape=[jax.ShapeDtypeStruct((b, s, D_B), BF16),
                   jax.ShapeDtypeStruct((b, G_B, s, LANES), BF16)],
        compiler_params=_cparams(("parallel", "parallel", "parallel")),
        name="nsa_cmp_topk",
    )(qb, kcd, vcd, table, ov_t)


def _sel_kernel(q_ref, sb_ref, k_ref, v_ref, t_ref, o_ref, kaug, vp, m_ref, acc_ref, *, nblk):
    lo = _lo_mask()
    s_len = nblk * BLK
    n_diag = t_ref.shape[1] - 1
    rowblk = lax.broadcasted_iota(jnp.int32, (s_len, LANES), 0) >> int(math.log2(SLC_BLOCK))
    lane = lax.broadcasted_iota(jnp.int32, (s_len, LANES), 1)
    onehot = jnp.where(lane - HEAD_DIM == rowblk, 1.0, 0.0).astype(BF16)
    kaug[...] = jnp.where(lo, k_ref[0], onehot)
    vp[...] = jnp.where(lo, v_ref[0], jnp.where(lane == HEAD_DIM, 1.0, 0.0).astype(BF16))
    m_ref[...] = jnp.full(m_ref.shape, NEG, F32)
    acc_ref[...] = jnp.zeros(acc_ref.shape, F32)
    kb = 4
    qb = 2
    sub = kb // qb
    kt_rows, qt_rows = kb * BLK, qb * BLK

    def ktile(t, carry):
        k0 = pl.multiple_of(t * kt_rows, kt_rows)
        kt = kaug[pl.ds(k0, kt_rows), :]
        vt = vp[pl.ds(k0, kt_rows), :]

        def qstep(mp, c):
            scores = []
            for u in range(sub):
                r0 = pl.multiple_of((sub * mp + u) * qt_rows, qt_rows)
                sbf = sb_ref[0, 0, pl.ds(r0, qt_rows), :].astype(F32)
                rows = []
                for pr in range(R_B // 2):
                    q = q_ref[0, pl.ds(r0, qt_rows), pr * LANES:(pr + 1) * LANES].astype(F32)
                    rows.append(jnp.where(lo, q, sbf))
                    rows.append(jnp.where(lo, pltpu.roll(q, HEAD_DIM, 1), sbf))
                qst = jnp.concatenate(rows, axis=0).astype(BF16)
                bias = jnp.concatenate([
                    jnp.concatenate([t_ref[h, jnp.clip(qb * (sub * mp + u) + a - kb * t - c + 1, 0, n_diag)]
                                     for c in range(kb)], axis=1)
                    for h in range(R_B) for a in range(qb)], axis=0)
                scores.append(_dot_nt(qst, kt) + bias)
            for u in range(sub):
                r0 = pl.multiple_of((sub * mp + u) * qt_rows, qt_rows)
                s = scores[u]
                ps, alphas = [], []
                for h in range(R_B):
                    for a in range(qb):
                        lo_r = h * qt_rows + a * BLK
                        st = pl.ds(r0 + a * BLK, BLK)
                        sh = s[lo_r:lo_r + BLK]
                        m_prev = m_ref[h, st, :]
                        m_new = jnp.maximum(m_prev, jnp.max(sh, axis=-1, keepdims=True))
                        alpha = jnp.exp2(m_prev - m_new)
                        p = jnp.exp2(sh - jnp.concatenate([m_new] * kb, axis=1))
                        m_ref[h, st, :] = m_new
                        ps.append(p.astype(BF16))
                        alphas.append(alpha)
                pv = jnp.dot(jnp.concatenate(ps, axis=0), vt, preferred_element_type=F32)
                for h in range(R_B):
                    for a in range(qb):
                        lo_r = h * qt_rows + a * BLK
                        st = pl.ds(r0 + a * BLK, BLK)
                        acc_ref[h, st, :] = alphas[qb * h + a] * acc_ref[h, st, :] + pv[lo_r:lo_r + BLK]
            return c

        lax.fori_loop(t, nblk // kb, qstep, 0)
        return carry

    lax.fori_loop(0, nblk // kb, ktile, 0)

    def finish(n, carry):
        st = pl.ds(pl.multiple_of(n * BLK, BLK), BLK)
        for pr in range(R_B // 2):
            even, odd = acc_ref[2 * pr, st, :], acc_ref[2 * pr + 1, st, :]
            even = even * (1.0 / even[:, HEAD_DIM:HEAD_DIM + 1])
            odd = odd * (1.0 / odd[:, HEAD_DIM:HEAD_DIM + 1])
            o = jnp.where(lo, even, pltpu.roll(odd, HEAD_DIM, 1))
            o_ref[0, st, pr * LANES:(pr + 1) * LANES] = o.astype(o_ref.dtype)
        return carry

    lax.fori_loop(0, nblk, finish, 0)


def _sel_attention(qb, selb, ks_dup, vs_dup, table):
    b, s, _ = qb.shape
    gw = R_B * HEAD_DIM
    n_e = table.shape[1]
    one = pl.Buffered(1)
    return pl.pallas_call(
        functools.partial(_sel_kernel, nblk=s // BLK),
        grid=(b, G_B),
        in_specs=[pl.BlockSpec((1, s, gw), lambda i, g: (i, 0, g), pipeline_mode=one),
                  pl.BlockSpec((1, 1, s, LANES), lambda i, g: (i, g, 0, 0), pipeline_mode=one),
                  pl.BlockSpec((1, s, LANES), lambda i, g: (i, 0, g), pipeline_mode=one),
                  pl.BlockSpec((1, s, LANES), lambda i, g: (i, 0, g), pipeline_mode=one),
                  pl.BlockSpec((R_B, n_e, BLK, BLK), lambda i, g: (g, 0, 0, 0), pipeline_mode=one)],
        out_specs=pl.BlockSpec((1, s, gw), lambda i, g: (i, 0, g)),
        out_shape=jax.ShapeDtypeStruct((b, s, D_B), BF16),
        scratch_shapes=[pltpu.VMEM((s, LANES), BF16)] * 2 + [pltpu.VMEM((R_B, s, LANES), F32)] * 2,
        compiler_params=_cparams(("parallel", "parallel")),
        name="nsa_selected",
    )(qb, selb, ks_dup, vs_dup, table)


def _combine_kernel(x_ref, oa, za, oc, osl, ow, gb, zb, wo, ex, gate, lng, lnb, out_ref, *, alpha):
    f = lambda r: r[0].astype(F32)
    z = f(za)
    mix_a = f(oa) * (z * jax.nn.sigmoid(z))
    g = jax.nn.sigmoid(gb[0])
    g_hi = g.astype(BF16)
    g_lo = (g - g_hi.astype(F32)).astype(BF16)
    gx = (jnp.dot(g_hi, ex[...], preferred_element_type=F32)
          + jnp.dot(g_lo, ex[...], preferred_element_type=F32))
    ob = gx[:, :D_B] * f(oc) + gx[:, D_B:2 * D_B] * f(osl) + gx[:, 2 * D_B:] * f(ow)
    z = f(zb)
    mix_b = ob * (z * jax.nn.sigmoid(z))
    y = (jnp.dot(mix_a.astype(BF16), wo[:D_A], preferred_element_type=F32)
         + jnp.dot(mix_b.astype(BF16), wo[D_A:], preferred_element_type=F32))
    r = alpha * x_ref[0] + (1.0 + gate[0]) * y
    mu = jnp.mean(r, axis=-1, keepdims=True)
    rc = r - mu
    var = jnp.mean(rc * rc, axis=-1, keepdims=True)
    out_ref[0] = rc * lax.rsqrt(var + LN_EPS) * lng[...] + lnb[...]


def _combine(x, o_a, za, o_cmp, o_slc, o_win, gb, zb, w_out, expand, gate, ln_g, ln_b, alpha, tm=512):
    b, s, d = x.shape
    bs = lambda n: pl.BlockSpec((1, tm, n), lambda i, j: (i, j, 0))
    full = lambda a: pl.BlockSpec(a.shape, lambda i, j: (0,) * a.ndim)
    return pl.pallas_call(
        functools.partial(_combine_kernel, alpha=alpha),
        grid=(b, s // tm),
        in_specs=[bs(d)] + [bs(D_A)] * 2 + [bs(D_B)] * 3 + [bs(LANES), bs(D_B), full(w_out), full(expand),
                  pl.BlockSpec((1, 1, d), lambda i, j: (i, 0, 0)), full(ln_g), full(ln_b)],
        out_specs=bs(d),
        out_shape=jax.ShapeDtypeStruct((b, s, d), F32),
        compiler_params=_cparams(("parallel", "parallel")),
        name="merge_out_proj_ln",
    )(x, o_a, za, o_cmp, o_slc, o_win, gb, zb, w_out, expand, gate, ln_g, ln_b)


def _pack_w_in(w_in):
    n_gb = 3 * H_B
    main = w_in[..., :_N_MAIN]
    gbw = w_in[..., _N_MAIN:_N_MAIN + n_gb]
    zbw = w_in[..., _N_MAIN + n_gb:]
    pad = jnp.zeros(w_in.shape[:-1] + (LANES - n_gb,), w_in.dtype)
    return jnp.concatenate([main, gbw, pad, zbw], axis=-1).astype(BF16)


def _gate_expand():
    e = np.zeros((LANES, 3 * D_B), np.float32)
    for i in range(3):
        for h in range(H_B):
            e[i * H_B + h, i * D_B + h * HEAD_DIM:i * D_B + (h + 1) * HEAD_DIM] = 1.0
    return jnp.asarray(e, BF16)


def _overlap_t(n_col, n_slc):
    cs = np.arange(n_col)[None, :] * CMP_STRIDE
    ss = np.arange(n_slc)[:, None] * SLC_BLOCK
    return jnp.asarray(((cs < ss + SLC_BLOCK) & (cs + CMP_LEN > ss)).astype(np.float32), BF16)


def _layer(x, shift, scl, gate, w_in_p, w_out_b, pe_b, w1_b, w2d_b, ln_g, ln_b, tables, consts, alpha):
    b, s, d = x.shape
    (qa, ka, va, za, qb, kc_in, vc_in, ks, vs, kw, vw, zb, gb) = _project(x, shift, scl, w_in_p)

    o_a = _dilated_attention(qa, ka, va, tables["dilated"])

    kcd, vcd = _compress(kc_in, vc_in, w1_b, w2d_b, pe_b)
    o_cmp, selb = _cmp_attention(qb, kcd, vcd, tables["cmp"], consts["overlap_t"])
    o_slc = _sel_attention(qb, selb, ks, vs, tables["sel"])
    o_win = _win_attention(qb, kw, vw, tables["win"], n_prev=-(-(WIN - 1) // BLK))

    return _combine(x, o_a, za, o_cmp, o_slc, o_win, gb, zb, w_out_b, consts["expand"], gate,
                    ln_g, ln_b, alpha)


def kernel(x, c, w_in, w_out, cmp_pe, cmp_w1, cmp_w2, w_ada, b_ada, ln_g, ln_b, rel_bias):
    b, s, d = x.shape
    depth = w_in.shape[0]
    alpha = (2 * depth) ** 0.25
    n_pair = D_A // LANES

    bb = rel_bias.astype(F32) * LOG2E
    def dilated_table(window, dil):
        t = _band_table(bb[:, :H_A], 1, window // dil, dil).reshape(n_pair, 2 * BLK, 2 * BLK)
        first = t.at[..., :BLK].set(NEG)
        return jnp.stack([t, first], axis=1)

    n_prev_win = -(-(WIN - 1) // BLK)
    tables = {
        "dilated": [dilated_table(w, dl) for w, dl in DILATED],
        "win": _band_table(bb[:, H_A:], n_prev_win, WIN - 1, 1).reshape(G_B, R_B * BLK, (n_prev_win + 1) * BLK),
        "cmp": _cmp_table(bb[:, H_A:], s),
        "sel": _diag_table(bb[:, H_A:], 13),
    }
    consts = {"expand": _gate_expand(), "overlap_t": _overlap_t(s // CMP_STRIDE, s // SLC_BLOCK)}

    mod = _ada_mod(c, w_ada, b_ada)
    w_in_p = _pack_w_in(w_in)
    w_out_b = w_out.astype(BF16)
    w1_b = cmp_w1.astype(BF16)
    w2d_b = jnp.concatenate([cmp_w2, cmp_w2], axis=-1).astype(BF16)
    pe_b = jnp.broadcast_to(cmp_pe.reshape(depth, 2, 1, CMP_LEN * HEAD_DIM), (depth, 2, 8, CMP_LEN * HEAD_DIM)).astype(BF16)

    for l in range(depth):
        m = mod[l, :b]
        shift, scl, gate = (m[:, i * d:(i + 1) * d].reshape(b, 1, d) for i in range(3))
        x = _layer(x, shift, scl, gate, w_in_p[l], w_out_b[l], pe_b[l], w1_b[l], w2d_b[l],
                   ln_g[l].reshape(1, d), ln_b[l].reshape(1, d), tables, consts, alpha)
    return x
```

```python
import functools
import math

import numpy as np
import jax
import jax.numpy as jnp
from jax import lax
from jax.experimental import pallas as pl
from jax.experimental.pallas import tpu as pltpu

F32 = jnp.float32
BF16 = jnp.bfloat16

D_MODEL = 1024
HEAD_DIM = 64
H_A = 8
H_B = 8
G_B = 2
R_B = H_B // G_B
D_A = H_A * HEAD_DIM
D_B = H_B * HEAD_DIM
DILATED = ((128, 1), (512, 4), (2048, 16))
CMP_LEN = 32
CMP_STRIDE = 16
CMP_HIDDEN = 256
SLC_BLOCK = 64
SLC_TOPK = 16
SLC_LOCAL = 2
WIN = 512
N_BUCKETS = 32
MAX_DIST = 2048
LN_EPS = 1e-5
NEG = -1e30
LOG2E = 1.4426950408889634

LANES = 128
BLK = 128
VMEM_LIMIT = 56 * 1024 * 1024

_C_QA, _C_KA, _C_VA, _C_ZA, _C_QB = 0, 512, 1024, 1536, 2048
_C_NARROW = 2560
_C_ZB, _C_END = 3456, 3968
_N_MAIN = 3328


def _cparams(sem):
    return pltpu.CompilerParams(dimension_semantics=sem, vmem_limit_bytes=VMEM_LIMIT)


def _lo_mask():
    return lax.broadcasted_iota(jnp.int32, (1, LANES), 1) < HEAD_DIM


def _dot_nt(a, b):
    return lax.dot_general(a, b, (((1,), (1,)), ((), ())), preferred_element_type=F32)


def _bucket_np(dist):
    d = np.maximum(dist, 0)
    max_exact = N_BUCKETS // 2
    large = max_exact + (np.log(np.maximum(d, 1).astype(np.float32) / np.float32(max_exact))
                         / np.float32(math.log(MAX_DIST / max_exact))
                         * np.float32(N_BUCKETS - max_exact)).astype(np.int32)
    return np.where(d < max_exact, d, np.minimum(large, N_BUCKETS - 1)).astype(np.int32)


def _masked_bias(bias_by_bucket, dist, valid):
    onehot = np.eye(N_BUCKETS, dtype=np.float32)[_bucket_np(dist)]
    t = jnp.einsum("...k,kh->h...", onehot, bias_by_bucket, precision=lax.Precision.HIGHEST)
    return jnp.where(jnp.asarray(valid)[None], t, NEG)


def _toeplitz_kernel(v_ref, o_ref):
    cols = o_ref.shape[-1]
    for e in range(o_ref.shape[1]):
        x = jnp.broadcast_to(v_ref[0, e:e + 1, :], (BLK, v_ref.shape[-1]))
        o_ref[0, e] = pltpu.roll(x, 0, 1, stride=1, stride_axis=0)[:, :cols]


def _toeplitz(h, cols):
    n_h, n_e, n = h.shape
    width = 1 << (n - 1).bit_length()
    vec = jnp.concatenate([h[..., BLK - 1:], jnp.zeros((n_h, n_e, width - n), F32), h[..., :BLK - 1]], axis=-1)
    return pl.pallas_call(
        _toeplitz_kernel,
        grid=(n_h,),
        in_specs=[pl.BlockSpec((1, n_e, width), lambda a: (a, 0, 0))],
        out_specs=pl.BlockSpec((1, n_e, BLK, cols), lambda a: (a, 0, 0, 0)),
        out_shape=jax.ShapeDtypeStruct((n_h, n_e, BLK, cols), F32),
        compiler_params=_cparams(("parallel",)),
        name="toeplitz_table",
    )(vec)


def _band_table(bias_by_bucket, n_prev, window, scale):
    w = (n_prev + 1) * BLK
    dist = n_prev * BLK - (np.arange(BLK - 1 + w) - (BLK - 1))
    h = _masked_bias(bias_by_bucket, dist * scale, (dist >= 0) & (dist <= window))
    return _toeplitz(h[:, None], w)[:, 0]


def _diag_table(bias_by_bucket, n_diag):
    delta = np.arange(-1, n_diag + 1)[:, None]
    dist = delta * BLK - (np.arange(2 * BLK - 1)[None, :] - (BLK - 1))
    return _toeplitz(_masked_bias(bias_by_bucket, dist, dist >= 0), BLK)


def _cmp_table_kernel(base_ref, o_ref, *, rows):
    i = pl.program_id(1)
    n_col = o_ref.shape[-1]
    width = base_ref.shape[-1]
    base = base_ref[0]
    for j in range(rows // CMP_STRIDE):
        chunk = i * (rows // CMP_STRIDE) + j
        shift = (chunk + width - (n_col - 1)) % width
        o_ref[0, j * CMP_STRIDE:(j + 1) * CMP_STRIDE, :] = pltpu.roll(base, shift, 1)[:, :n_col]


def _cmp_table(bias_by_bucket, seq, rows=1024):
    n_col = seq // CMP_STRIDE
    t = np.arange(2 * n_col) - (n_col - 1)
    dist = -CMP_STRIDE * t[None, :] + np.arange(CMP_STRIDE)[:, None] - (CMP_LEN - 1)
    base = _masked_bias(bias_by_bucket, dist, dist >= 0)
    h = base.shape[0]
    return pl.pallas_call(
        functools.partial(_cmp_table_kernel, rows=rows),
        grid=(h, seq // rows),
        in_specs=[pl.BlockSpec((1, CMP_STRIDE, 2 * n_col), lambda a, i: (a, 0, 0))],
        out_specs=pl.BlockSpec((1, rows, n_col), lambda a, i: (a, i, 0)),
        out_shape=jax.ShapeDtypeStruct((h, seq, n_col), F32),
        compiler_params=_cparams(("parallel", "parallel")),
        name="cmp_bias_table",
    )(base)


def _ada_kernel(c_ref, w_ref, b_ref, o_ref):
    o_ref[0] = jnp.dot(c_ref[...], w_ref[0].astype(BF16), preferred_element_type=F32) + b_ref[0]


def _ada_mod(c, w_ada, b_ada):
    depth, d, d3 = w_ada.shape
    cp = jnp.zeros((8, d), BF16).at[:c.shape[0]].set(c.astype(BF16))
    nj = d3 // d
    return pl.pallas_call(
        _ada_kernel,
        grid=(depth, nj),
        in_specs=[pl.BlockSpec((8, d), lambda l, j: (0, 0)),
                  pl.BlockSpec((1, d, d), lambda l, j: (l, 0, j)),
                  pl.BlockSpec((1, 1, d), lambda l, j: (l, 0, j))],
        out_specs=pl.BlockSpec((1, 8, d), lambda l, j: (l, 0, j)),
        out_shape=jax.ShapeDtypeStruct((depth, 8, d3), F32),
        compiler_params=_cparams(("parallel", "parallel")),
        name="ada_mod",
    )(cp, w_ada, b_ada.reshape(depth, 1, d3))


def _proj_kernel(x_ref, sh_ref, sc_ref, w_ref, *refs, dils):
    n_lay = len(dils)
    qa, ka, va = refs[:n_lay], refs[n_lay:2 * n_lay], refs[2 * n_lay:3 * n_lay]
    za, qb, kc, vc, ks, vs, kw, vw, zb, gb, h_ref, xs = refs[3 * n_lay:]
    tm = h_ref.shape[0]
    h_ref[...] = (x_ref[0] * (1.0 + sc_ref[0]) + sh_ref[0]).astype(BF16)

    def mm(c0, n):
        return jnp.dot(h_ref[...], w_ref[:, c0:c0 + n], preferred_element_type=F32)

    def emit_regrouped(a, outs):
        outs[0][0] = a.astype(BF16)
        n_slab = a.shape[1] // LANES
        for c in range(n_slab):
            xs[0, c] = a[:, c * LANES:(c + 1) * LANES]
        d_prev = 1
        for lvl, (ref, dil) in enumerate(zip(outs[1:], dils[1:])):
            ratio, len_prev, len_new = dil // d_prev, tm // d_prev, tm // dil
            src, dst = xs.at[lvl % 2], xs.at[(lvl + 1) % 2]
            keep = lvl + 2 < len(dils)
            for rp in range(d_prev):
                for rs in range(ratio):
                    r = rp + d_prev * rs
                    for c in range(n_slab):
                        rows = src[c, pl.ds(rp * len_prev + rs, len_new, stride=ratio), :]
                        ref[0, r, :, c * LANES:(c + 1) * LANES] = rows.astype(BF16)
                        if keep:
                            dst[c, r * len_new:(r + 1) * len_new, :] = rows
            d_prev = dil

    qs = HEAD_DIM ** -0.5 * LOG2E
    emit_regrouped(mm(_C_QA, D_A) * qs, qa)
    emit_regrouped(mm(_C_KA, D_A), ka)
    emit_regrouped(mm(_C_VA, D_A), va)
    za[0] = mm(_C_ZA, D_A).astype(BF16)
    qb[0] = (mm(_C_QB, D_B) * qs).astype(BF16)
    lo = _lo_mask()
    narrow = mm(_C_NARROW, 7 * LANES)
    piece = lambda i: narrow[:, i * LANES:(i + 1) * LANES]
    for ref, i in ((kc, 0), (vc, 1)):
        xs[0, 0] = piece(i)
        for t in range(0, CMP_STRIDE, 2):
            x0 = xs[0, 0, pl.ds(t, tm // CMP_STRIDE, stride=CMP_STRIDE), :]
            x1 = xs[0, 0, pl.ds(t + 1, tm // CMP_STRIDE, stride=CMP_STRIDE), :]
            cols = slice(t * HEAD_DIM, (t + 2) * HEAD_DIM)
            ref[0, 0, :, cols] = jnp.where(lo, x0, pltpu.roll(x1, HEAD_DIM, 1)).astype(BF16)
            ref[0, 1, :, cols] = jnp.where(lo, pltpu.roll(x0, HEAD_DIM, 1), x1).astype(BF16)
    for ref, i in ((ks, 2), (vs, 3), (kw, 4), (vw, 5)):
        a = piece(i)
        r = pltpu.roll(a, HEAD_DIM, 1)
        ref[0, :, :LANES] = jnp.where(lo, a, r).astype(BF16)
        ref[0, :, LANES:] = jnp.where(lo, r, a).astype(BF16)
    gb[0] = piece(6)
    zb[0] = mm(_C_ZB, D_B).astype(BF16)


def _project(x, shift, scl, w, layer, tm=512):
    b, s, d = x.shape
    dils = tuple(dl for _, dl in DILATED)
    bs = lambda n: pl.BlockSpec((1, tm, n), lambda i, j: (i, j, 0))
    gs = pl.BlockSpec((1, G_B, tm // CMP_STRIDE, CMP_STRIDE * HEAD_DIM), lambda i, j: (i, 0, j, 0))
    sd = lambda n, dt=BF16: jax.ShapeDtypeStruct((b, s, n), dt)
    gd = jax.ShapeDtypeStruct((b, G_B, s // CMP_STRIDE, CMP_STRIDE * HEAD_DIM), BF16)
    mod = pl.BlockSpec((1, 1, d), lambda i, j: (i, 0, 0))
    lay_specs = [bs(D_A)] + [pl.BlockSpec((1, dl, tm // dl, D_A), lambda i, j: (i, 0, j, 0)) for dl in dils[1:]]
    lay_shapes = [sd(D_A)] + [jax.ShapeDtypeStruct((b, dl, s // dl, D_A), BF16) for dl in dils[1:]]
    outs = pl.pallas_call(
        functools.partial(_proj_kernel, dils=dils),
        grid=(b, s // tm),
        in_specs=[pl.BlockSpec((1, tm, d), lambda i, j: (i, j, 0)), mod, mod,
                  pl.BlockSpec((None, d, _C_END), lambda i, j: (layer, 0, 0))],
        out_specs=lay_specs * 3 + [bs(D_A), bs(D_B), gs, gs,
                                   bs(2 * LANES), bs(2 * LANES), bs(2 * LANES), bs(2 * LANES), bs(D_B), bs(LANES)],
        out_shape=lay_shapes * 3 + [sd(D_A), sd(D_B), gd, gd,
                                    sd(2 * LANES), sd(2 * LANES), sd(2 * LANES), sd(2 * LANES), sd(D_B), sd(LANES, F32)],
        scratch_shapes=[pltpu.VMEM((tm, d), BF16), pltpu.VMEM((2, D_A // LANES, tm, LANES), F32)],
        compiler_params=_cparams(("parallel", "parallel")),
        name="in_proj",
    )(x, shift, scl, w)
    n = len(dils)
    return (outs[:n], outs[n:2 * n], outs[2 * n:3 * n], *outs[3 * n:])


def _dilated_kernel(*refs, seq, unroll):
    n_pat = len(DILATED)
    qkv = refs[:3 * n_pat]
    t_refs = refs[3 * n_pat:4 * n_pat]
    o_ref = refs[4 * n_pat]
    qlo, qhi, kp, vp, o_r, m_r, l_r, o_t, m_t, l_t, m_acc, n_acc, d_acc = refs[4 * n_pat + 1:]
    lo = _lo_mask()
    nblk = seq // BLK
    chunk = 512
    kp[:BLK, :] = jnp.zeros((BLK, LANES), BF16)
    vp[:BLK, :] = jnp.zeros((BLK, LANES), BF16)

    for pi, ((_, dil), t_ref) in enumerate(zip(DILATED, t_refs)):
        q_ref, k_ref, v_ref = qkv[pi], qkv[n_pat + pi], qkv[2 * n_pat + pi]
        rl = seq // dil
        nb = rl // BLK
        for r in range(dil):
            idx = (0,) if dil == 1 else (0, r)
            q = q_ref[idx]
            zero = jnp.zeros_like(q)
            qlo[r * rl:(r + 1) * rl, :] = jnp.where(lo, q, zero)
            qhi[r * rl:(r + 1) * rl, :] = jnp.where(lo, zero, q)
            kp[BLK + r * rl:BLK + (r + 1) * rl, :] = k_ref[idx]
            vp[BLK + r * rl:BLK + (r + 1) * rl, :] = v_ref[idx]
        o_dst, m_dst, l_dst = (n_acc, m_acc, d_acc) if dil == 1 else (o_r, m_r, l_r)

        def blocks(it, carry, t_ref=t_ref, nb=nb, o_dst=o_dst, m_dst=m_dst, l_dst=l_dst):
            for u in range(unroll):
                g = it * unroll + u
                first = jnp.where((g & (nb - 1)) == 0, 1, 0)
                q0 = pl.multiple_of(g * BLK, BLK)
                qs = jnp.concatenate([qlo[pl.ds(q0, BLK), :], qhi[pl.ds(q0, BLK), :]], axis=0)
                s = _dot_nt(qs, kp[pl.ds(q0, 2 * BLK), :]) + t_ref[0, first]
                m = jnp.max(s, axis=-1, keepdims=True)
                p = jnp.exp2(s - m)
                l = jnp.sum(p, axis=-1, keepdims=True)
                pv = jnp.dot(p.astype(BF16), vp[pl.ds(q0, 2 * BLK), :], preferred_element_type=F32)
                o_dst[pl.ds(q0, BLK), :] = jnp.where(lo, pv[:BLK], pv[BLK:])
                m_dst[pl.ds(q0, BLK), :] = jnp.where(lo, m[:BLK], m[BLK:])
                l_dst[pl.ds(q0, BLK), :] = jnp.where(lo, l[:BLK], l[BLK:])
            return carry

        lax.fori_loop(0, nblk // unroll, blocks, 0)
        if dil == 1:
            continue

        for r in range(dil):
            for src, dst in ((o_r, o_t), (m_r, m_t), (l_r, l_t)):
                dst[pl.ds(r, rl, stride=dil), :] = src[r * rl:(r + 1) * rl, :]

        def fold(c, carry, last=(pi == n_pat - 1)):
            rows = pl.ds(pl.multiple_of(c * chunk, chunk), chunk)
            m_old, m_new = m_acc[rows, :], m_t[rows, :]
            mx = jnp.maximum(m_old, m_new)
            a, b_ = jnp.exp2(m_old - mx), jnp.exp2(m_new - mx)
            n_new = a * n_acc[rows, :] + b_ * o_t[rows, :]
            d_new = a * d_acc[rows, :] + b_ * l_t[rows, :]
            if last:
                o_ref[0, rows, :] = (n_new * (1.0 / d_new)).astype(o_ref.dtype)
            else:
                m_acc[rows, :] = mx
                n_acc[rows, :] = n_new
                d_acc[rows, :] = d_new
            return carry

        lax.fori_loop(0, seq // chunk, fold, 0)


def _dilated_attention(qa, ka, va, tables, unroll=32):
    assert DILATED[0][1] == 1 and all(dl > 1 for _, dl in DILATED[1:])
    b, s, c = qa[0].shape
    xs = pl.BlockSpec((1, s, LANES), lambda i, j: (i, 0, j))
    lay = [xs] + [pl.BlockSpec((1, dl, s // dl, LANES), lambda i, j: (i, 0, 0, j)) for _, dl in DILATED[1:]]
    ts = pl.BlockSpec((1, 2, 2 * BLK, 2 * BLK), lambda i, j: (j, 0, 0, 0))
    big = lambda rows, dt: pltpu.VMEM((rows, LANES), dt)
    return pl.pallas_call(
        functools.partial(_dilated_kernel, seq=s, unroll=unroll),
        grid=(b, c // LANES),
        in_specs=lay * 3 + [ts] * len(DILATED),
        out_specs=xs,
        out_shape=jax.ShapeDtypeStruct((b, s, c), BF16),
        scratch_shapes=[big(s, BF16)] * 2 + [big(s + BLK, BF16)] * 2 + [big(s, F32)] * 9,
        compiler_params=_cparams(("parallel", "parallel")),
        name="dilated_attn",
    )(*qa, *ka, *va, *tables)


def _win_kernel(q_ref, k_ref, v_ref, t_ref, o_ref, qst, kaug, vp, *, seq, n_prev, unroll):
    lo = _lo_mask()
    lane = lax.broadcasted_iota(jnp.int32, (1, LANES), 1)
    flag = lane == HEAD_DIM
    pad = n_prev * BLK
    span = (n_prev + 1) * BLK
    negflag = jnp.where(flag, NEG, 0.0)
    for pr in range(R_B // 2):
        q = q_ref[0, :, pr * LANES:(pr + 1) * LANES].astype(F32)
        qst[2 * pr] = jnp.where(lo, q, negflag).astype(BF16)
        qst[2 * pr + 1] = jnp.where(lo, pltpu.roll(q, HEAD_DIM, 1), negflag).astype(BF16)
    k = k_ref[0]
    kaug[:pad, :] = jnp.broadcast_to(jnp.where(flag, 1.0, 0.0), (pad, LANES)).astype(BF16)
    kaug[pad:, :] = jnp.where(lo, k, jnp.zeros_like(k))
    vp[:pad, :] = jnp.zeros((pad, LANES), BF16)
    vp[pad:, :] = jnp.where(lo, v_ref[0], jnp.where(flag, 1.0, 0.0).astype(BF16))

    def blocks(it, carry):
        for u in range(unroll):
            n = it * unroll + u
            r0 = pl.multiple_of(n * BLK, BLK)
            qs = jnp.concatenate([qst[h, pl.ds(r0, BLK), :] for h in range(R_B)], axis=0)
            s = _dot_nt(qs, kaug[pl.ds(r0, span), :]) + t_ref[0]
            m = jnp.max(s, axis=-1, keepdims=True)
            p = jnp.exp2(s - m)
            pv = jnp.dot(p.astype(BF16), vp[pl.ds(r0, span), :], preferred_element_type=F32)
            pv = pv * (1.0 / pv[:, HEAD_DIM:HEAD_DIM + 1])
            for pr in range(R_B // 2):
                even, odd = pv[2 * pr * BLK:(2 * pr + 1) * BLK], pv[(2 * pr + 1) * BLK:(2 * pr + 2) * BLK]
                o = jnp.where(lo, even, pltpu.roll(odd, HEAD_DIM, 1))
                o_ref[0, pl.ds(r0, BLK), pr * LANES:(pr + 1) * LANES] = o.astype(o_ref.dtype)
        return carry

    lax.fori_loop(0, seq // BLK // unroll, blocks, 0)


def _win_attention(qb, kw_dup, vw_dup, table, n_prev, unroll=8):
    b, s, _ = qb.shape
    gw = R_B * HEAD_DIM
    w = table.shape[-1]
    return pl.pallas_call(
        functools.partial(_win_kernel, seq=s, n_prev=n_prev, unroll=unroll),
        grid=(b, G_B),
        in_specs=[pl.BlockSpec((1, s, gw), lambda i, g: (i, 0, g)),
                  pl.BlockSpec((1, s, LANES), lambda i, g: (i, 0, g)),
                  pl.BlockSpec((1, s, LANES), lambda i, g: (i, 0, g)),
                  pl.BlockSpec((1, R_B * BLK, w), lambda i, g: (g, 0, 0))],
        out_specs=pl.BlockSpec((1, s, gw), lambda i, g: (i, 0, g)),
        out_shape=jax.ShapeDtypeStruct((b, s, D_B), BF16),
        scratch_shapes=[pltpu.VMEM((R_B, s, LANES), BF16)] + [pltpu.VMEM((s + n_prev * BLK, LANES), BF16)] * 2,
        compiler_params=_cparams(("parallel", "parallel")),
        name="nsa_window",
    )(qb, kw_dup, vw_dup, table)


def _compress_kernel(kin, vin, w1, w2, pe, kout, vout):
    n_chunk = kin.shape[2]
    half = w1.shape[1] // 2
    rows = lax.broadcasted_iota(jnp.int32, (n_chunk, 1), 0)
    for t, (xin, out) in enumerate(((kin, kout), (vin, vout))):
        x = xin[0, 0]
        a = jnp.dot(x, w1[t, :half], preferred_element_type=F32)
        b = jnp.dot(x, w1[t, half:], preferred_element_type=F32)
        c = jnp.dot(pe[t], w1[t], preferred_element_type=F32)[0:1]
        hid = a + pltpu.roll(b, n_chunk - 1, 0) + c
        act = hid * jax.nn.sigmoid(hid)
        o = jnp.dot(act.astype(BF16), w2[t], preferred_element_type=F32)
        out[0, 0] = jnp.where(rows < n_chunk - 1, o, 0.0).astype(BF16)


def _compress(kc_in, vc_in, w1, w2d, pe, layer):
    b, g, n_chunk, f = kc_in.shape
    xs = pl.BlockSpec((1, 1, n_chunk, f), lambda i, j: (i, j, 0, 0))
    os_ = pl.BlockSpec((1, 1, n_chunk, LANES), lambda i, j: (i, j, 0, 0))
    full = lambda a: pl.BlockSpec((None,) + a.shape[1:], lambda i, j: (layer,) + (0,) * (a.ndim - 1))
    od = jax.ShapeDtypeStruct((b, g, n_chunk, LANES), BF16)
    return pl.pallas_call(
        _compress_kernel,
        grid=(b, g),
        in_specs=[xs, xs, full(w1), full(w2d), full(pe)],
        out_specs=[os_, os_],
        out_shape=[od, od],
        compiler_params=_cparams(("parallel", "parallel")),
        name="nsa_compress",
    )(kc_in, vc_in, w1, w2d, pe)


def _cmp_kernel(q_ref, kc_ref, vc_ref, t_ref, ov_ref, o_ref, sb_ref, *, tq):
    i = pl.program_id(2)
    lo = _lo_mask()
    n_col, n_slc = kc_ref.shape[2], ov_ref.shape[0]

    def attend_and_rank(ncol, nsel):
        kcd = kc_ref[0, 0, :ncol, :]
        vcd = vc_ref[0, 0, :ncol, :]
        zero = jnp.zeros_like(vcd)
        vlo = jnp.where(lo, vcd, zero)
        vhi = jnp.where(lo, zero, vcd)
        psum = None
        for pr in range(R_B // 2):
            q = q_ref[0, :, pr * LANES:(pr + 1) * LANES]
            qz = jnp.zeros_like(q)
            acc = None
            for hh, (qm, vm) in enumerate(((jnp.where(lo, q, qz), vlo), (jnp.where(lo, qz, q), vhi))):
                s = _dot_nt(qm, kcd) + t_ref[2 * pr + hh, :, :ncol]
                m = jnp.max(s, axis=-1, keepdims=True)
                e = jnp.exp2(s - m)
                l = jnp.sum(e, axis=-1, keepdims=True)
                p = e * jnp.where(m > 0.5 * NEG, 1.0 / l, 0.0)
                psum = p if psum is None else psum + p
                pv = jnp.dot(p.astype(BF16), vm, preferred_element_type=F32)
                acc = pv if acc is None else acc + pv
            o_ref[0, :, pr * LANES:(pr + 1) * LANES] = acc.astype(o_ref.dtype)

        p_hi = psum.astype(BF16)
        p_lo = (psum - p_hi.astype(F32)).astype(BF16)
        ov = ov_ref[:nsel, :ncol]
        score = (_dot_nt(ov, p_hi) + _dot_nt(ov, p_lo))
        blk = lax.broadcasted_iota(jnp.int32, (nsel, 1), 0)
        pos = i * tq + lax.broadcasted_iota(jnp.int32, (1, tq), 1)
        cur = pos >> int(math.log2(SLC_BLOCK))
        forced = (blk == 0) | ((cur - blk >= 0) & (cur - blk < SLC_LOCAL))
        score = jnp.where(forced, 1e9, jnp.where(blk > cur, -1e9, score))
        sub = lax.broadcasted_iota(jnp.int32, (8, 1), 0)
        sel_rows = []
        for g8 in range(nsel // 8):
            sg = score[g8 * 8:(g8 + 1) * 8]
            rank = jnp.zeros(sg.shape, F32)
            for mp in range(nsel):
                row = score[mp:mp + 1]
                if mp < g8 * 8:
                    ahead = jnp.where(row >= sg, 1.0, 0.0)
                elif mp >= (g8 + 1) * 8:
                    ahead = jnp.where(row > sg, 1.0, 0.0)
                else:
                    ahead = jnp.where(sub > mp - g8 * 8, jnp.where(row >= sg, 1.0, 0.0), jnp.where(row > sg, 1.0, 0.0))
                rank = rank + ahead
            sel_rows.append(jnp.where(rank < float(min(SLC_TOPK, n_slc)), 0.0, NEG))
        if nsel < n_slc:
            sel_rows.append(jnp.full((n_slc - nsel, tq), NEG, F32))
        pad = LANES // n_slc
        selb = jnp.concatenate(sel_rows * pad, axis=0)
        sb_ref[0, 0] = selb.T.astype(BF16)

    half = pl.num_programs(2) // 2
    pl.when(i < half)(lambda: attend_and_rank(n_col // 2, n_slc // 2))
    pl.when(i >= half)(lambda: attend_and_rank(n_col, n_slc))


def _cmp_attention(qb, kcd, vcd, table, ov_t, tq=512):
    b, s, _ = qb.shape
    n_col = kcd.shape[2]
    gw = R_B * HEAD_DIM
    return pl.pallas_call(
        functools.partial(_cmp_kernel, tq=tq),
        grid=(b, G_B, s // tq),
        in_specs=[pl.BlockSpec((1, tq, gw), lambda i, g, j: (i, j, g)),
                  pl.BlockSpec((1, 1, n_col, LANES), lambda i, g, j: (i, g, 0, 0)),
                  pl.BlockSpec((1, 1, n_col, LANES), lambda i, g, j: (i, g, 0, 0)),
                  pl.BlockSpec((R_B, tq, n_col), lambda i, g, j: (g, j, 0)),
                  pl.BlockSpec(ov_t.shape, lambda i, g, j: (0, 0))],
        out_specs=[pl.BlockSpec((1, tq, gw), lambda i, g, j: (i, j, g)),
                   pl.BlockSpec((1, 1, tq, LANES), lambda i, g, j: (i, g, j, 0))],
        out_shape=[jax.ShapeDtypeStruct((b, s, D_B), BF16),
                   jax.ShapeDtypeStruct((b, G_B, s, LANES), BF16)],
        compiler_params=_cparams(("parallel", "parallel", "parallel")),
        name="nsa_cmp_topk",
    )(qb, kcd, vcd, table, ov_t)


def _sel_kernel(q_ref, sb_ref, k_ref, v_ref, t_ref, o_ref, kaug, vp, m_ref, acc_ref, *, nblk):
    lo = _lo_mask()
    s_len = nblk * BLK
    n_diag = t_ref.shape[1] - 1
    h0 = pl.program_id(1) * R_B
    rowblk = lax.broadcasted_iota(jnp.int32, (s_len, LANES), 0) >> int(math.log2(SLC_BLOCK))
    lane = lax.broadcasted_iota(jnp.int32, (s_len, LANES), 1)
    onehot = jnp.where(lane - HEAD_DIM == rowblk, 1.0, 0.0).astype(BF16)
    kaug[...] = jnp.where(lo, k_ref[0], onehot)
    vp[...] = jnp.where(lo, v_ref[0], jnp.where(lane == HEAD_DIM, 1.0, 0.0).astype(BF16))
    m_ref[...] = jnp.full(m_ref.shape, NEG, F32)
    acc_ref[...] = jnp.zeros(acc_ref.shape, F32)
    kb = 4
    qb = 2
    sub = kb // qb
    kt_rows, qt_rows = kb * BLK, qb * BLK

    def ktile(t, carry):
        k0 = pl.multiple_of(t * kt_rows, kt_rows)
        kt = kaug[pl.ds(k0, kt_rows), :]
        vt = vp[pl.ds(k0, kt_rows), :]

        def qstep(mp, c):
            scores = []
            for u in range(sub):
                r0 = pl.multiple_of((sub * mp + u) * qt_rows, qt_rows)
                sbf = sb_ref[0, 0, pl.ds(r0, qt_rows), :].astype(F32)
                rows = []
                for pr in range(R_B // 2):
                    q = q_ref[0, pl.ds(r0, qt_rows), pr * LANES:(pr + 1) * LANES].astype(F32)
                    rows.append(jnp.where(lo, q, sbf))
                    rows.append(jnp.where(lo, pltpu.roll(q, HEAD_DIM, 1), sbf))
                qst = jnp.concatenate(rows, axis=0).astype(BF16)
                bias = jnp.concatenate([
                    jnp.concatenate([t_ref[h0 + h, jnp.clip(qb * (sub * mp + u) + a - kb * t - c + 1, 0, n_diag)]
                                     for c in range(kb)], axis=1)
                    for h in range(R_B) for a in range(qb)], axis=0)
                scores.append(_dot_nt(qst, kt) + bias)
            for u in range(sub):
                r0 = pl.multiple_of((sub * mp + u) * qt_rows, qt_rows)
                s = scores[u]
                ps, alphas = [], []
                for h in range(R_B):
                    for a in range(qb):
                        lo_r = h * qt_rows + a * BLK
                        st = pl.ds(r0 + a * BLK, BLK)
                        sh = s[lo_r:lo_r + BLK]
                        m_prev = m_ref[h, st, :]
                        m_new = jnp.maximum(m_prev, jnp.max(sh, axis=-1, keepdims=True))
                        alpha = jnp.exp2(m_prev - m_new)
                        p = jnp.exp2(sh - jnp.concatenate([m_new] * kb, axis=1))
                        m_ref[h, st, :] = m_new
                        ps.append(p.astype(BF16))
                        alphas.append(alpha)
                pv = jnp.dot(jnp.concatenate(ps, axis=0), vt, preferred_element_type=F32)
                for h in range(R_B):
                    for a in range(qb):
                        lo_r = h * qt_rows + a * BLK
                        st = pl.ds(r0 + a * BLK, BLK)
                        acc_ref[h, st, :] = alphas[qb * h + a] * acc_ref[h, st, :] + pv[lo_r:lo_r + BLK]
            return c

        lax.fori_loop(t, nblk // kb, qstep, 0)
        return carry

    lax.fori_loop(0, nblk // kb, ktile, 0)

    def finish(n, carry):
        st = pl.ds(pl.multiple_of(n * BLK, BLK), BLK)
        for pr in range(R_B // 2):
            even, odd = acc_ref[2 * pr, st, :], acc_ref[2 * pr + 1, st, :]
            even = even * (1.0 / even[:, HEAD_DIM:HEAD_DIM + 1])
            odd = odd * (1.0 / odd[:, HEAD_DIM:HEAD_DIM + 1])
            o = jnp.where(lo, even, pltpu.roll(odd, HEAD_DIM, 1))
            o_ref[0, st, pr * LANES:(pr + 1) * LANES] = o.astype(o_ref.dtype)
        return carry

    lax.fori_loop(0, nblk, finish, 0)


def _sel_attention(qb, selb, ks_dup, vs_dup, table):
    b, s, _ = qb.shape
    gw = R_B * HEAD_DIM
    n_e = table.shape[1]
    return pl.pallas_call(
        functools.partial(_sel_kernel, nblk=s // BLK),
        grid=(b, G_B),
        in_specs=[pl.BlockSpec((1, s, gw), lambda i, g: (i, 0, g)),
                  pl.BlockSpec((1, 1, s, LANES), lambda i, g: (i, g, 0, 0)),
                  pl.BlockSpec((1, s, LANES), lambda i, g: (i, 0, g)),
                  pl.BlockSpec((1, s, LANES), lambda i, g: (i, 0, g)),
                  pl.BlockSpec(table.shape, lambda i, g: (0, 0, 0, 0), pipeline_mode=pl.Buffered(1))],
        out_specs=pl.BlockSpec((1, s, gw), lambda i, g: (i, 0, g)),
        out_shape=jax.ShapeDtypeStruct((b, s, D_B), BF16),
        scratch_shapes=[pltpu.VMEM((s, LANES), BF16)] * 2 + [pltpu.VMEM((R_B, s, LANES), F32)] * 2,
        compiler_params=_cparams(("parallel", "parallel")),
        name="nsa_selected",
    )(qb, selb, ks_dup, vs_dup, table)


def _combine_kernel(x_ref, oa, za, oc, osl, ow, gb, zb, wo, ex, gate, lng, lnb, out_ref, *, alpha):
    f = lambda r: r[0].astype(F32)
    z = f(za)
    mix_a = f(oa) * (z * jax.nn.sigmoid(z))
    g = jax.nn.sigmoid(gb[0])
    g_hi = g.astype(BF16)
    g_lo = (g - g_hi.astype(F32)).astype(BF16)
    gx = (jnp.dot(g_hi, ex[...], preferred_element_type=F32)
          + jnp.dot(g_lo, ex[...], preferred_element_type=F32))
    ob = gx[:, :D_B] * f(oc) + gx[:, D_B:2 * D_B] * f(osl) + gx[:, 2 * D_B:] * f(ow)
    z = f(zb)
    mix_b = ob * (z * jax.nn.sigmoid(z))
    y = (jnp.dot(mix_a.astype(BF16), wo[:D_A], preferred_element_type=F32)
         + jnp.dot(mix_b.astype(BF16), wo[D_A:], preferred_element_type=F32))
    r = alpha * x_ref[0] + (1.0 + gate[0]) * y
    mu = jnp.mean(r, axis=-1, keepdims=True)
    rc = r - mu
    var = jnp.mean(rc * rc, axis=-1, keepdims=True)
    out_ref[0] = rc * lax.rsqrt(var + LN_EPS) * lng[...] + lnb[...]


def _combine(x, o_a, za, o_cmp, o_slc, o_win, gb, zb, w_out, expand, gate, ln_g, ln_b, alpha, layer, tm=512):
    b, s, d = x.shape
    bs = lambda n: pl.BlockSpec((1, tm, n), lambda i, j: (i, j, 0))
    full = lambda a: pl.BlockSpec(a.shape, lambda i, j: (0,) * a.ndim)
    return pl.pallas_call(
        functools.partial(_combine_kernel, alpha=alpha),
        grid=(b, s // tm),
        in_specs=[bs(d)] + [bs(D_A)] * 2 + [bs(D_B)] * 3 + [bs(LANES), bs(D_B),
                  pl.BlockSpec((None,) + w_out.shape[1:], lambda i, j: (layer, 0, 0)), full(expand),
                  pl.BlockSpec((1, 1, d), lambda i, j: (i, 0, 0)), full(ln_g), full(ln_b)],
        out_specs=bs(d),
        out_shape=jax.ShapeDtypeStruct((b, s, d), F32),
        compiler_params=_cparams(("parallel", "parallel")),
        name="merge_out_proj_ln",
    )(x, o_a, za, o_cmp, o_slc, o_win, gb, zb, w_out, expand, gate, ln_g, ln_b)


def _pack_w_in(w_in):
    n_gb = 3 * H_B
    main = w_in[..., :_N_MAIN]
    gbw = w_in[..., _N_MAIN:_N_MAIN + n_gb]
    zbw = w_in[..., _N_MAIN + n_gb:]
    pad = jnp.zeros(w_in.shape[:-1] + (LANES - n_gb,), w_in.dtype)
    return jnp.concatenate([main, gbw, pad, zbw], axis=-1).astype(BF16)


def _gate_expand():
    e = np.zeros((LANES, 3 * D_B), np.float32)
    for i in range(3):
        for h in range(H_B):
            e[i * H_B + h, i * D_B + h * HEAD_DIM:i * D_B + (h + 1) * HEAD_DIM] = 1.0
    return jnp.asarray(e, BF16)


def _overlap_t(n_col, n_slc):
    cs = np.arange(n_col)[None, :] * CMP_STRIDE
    ss = np.arange(n_slc)[:, None] * SLC_BLOCK
    return jnp.asarray(((cs < ss + SLC_BLOCK) & (cs + CMP_LEN > ss)).astype(np.float32), BF16)


def _layer(x, shift, scl, gate, w_in_p, w_out_b, pe_b, w1_b, w2d_b, ln_g, ln_b, tables, consts, alpha, layer):
    b, s, d = x.shape
    (qa, ka, va, za, qb, kc_in, vc_in, ks, vs, kw, vw, zb, gb) = _project(x, shift, scl, w_in_p, layer)

    o_a = _dilated_attention(qa, ka, va, tables["dilated"])

    kcd, vcd = _compress(kc_in, vc_in, w1_b, w2d_b, pe_b, layer)
    o_cmp, selb = _cmp_attention(qb, kcd, vcd, tables["cmp"], consts["overlap_t"])
    o_slc = _sel_attention(qb, selb, ks, vs, tables["sel"])
    o_win = _win_attention(qb, kw, vw, tables["win"], n_prev=-(-(WIN - 1) // BLK))

    return _combine(x, o_a, za, o_cmp, o_slc, o_win, gb, zb, w_out_b, consts["expand"], gate,
                    ln_g, ln_b, alpha, layer)


def kernel(x, c, w_in, w_out, cmp_pe, cmp_w1, cmp_w2, w_ada, b_ada, ln_g, ln_b, rel_bias):
    b, s, d = x.shape
    depth = w_in.shape[0]
    alpha = (2 * depth) ** 0.25
    n_pair = D_A // LANES

    bb = rel_bias.astype(F32) * LOG2E
    def dilated_table(window, dil):
        t = _band_table(bb[:, :H_A], 1, window // dil, dil).reshape(n_pair, 2 * BLK, 2 * BLK)
        first = t.at[..., :BLK].set(NEG)
        return jnp.stack([t, first], axis=1)

    n_prev_win = -(-(WIN - 1) // BLK)
    tables = {
        "dilated": [dilated_table(w, dl) for w, dl in DILATED],
        "win": _band_table(bb[:, H_A:], n_prev_win, WIN - 1, 1).reshape(G_B, R_B * BLK, (n_prev_win + 1) * BLK),
        "cmp": _cmp_table(bb[:, H_A:], s),
        "sel": _diag_table(bb[:, H_A:], 13),
    }
    consts = {"expand": _gate_expand(), "overlap_t": _overlap_t(s // CMP_STRIDE, s // SLC_BLOCK)}

    mod = _ada_mod(c, w_ada, b_ada)
    w_in_p = _pack_w_in(w_in)
    w_out_b = w_out.astype(BF16)
    w1_b = cmp_w1.astype(BF16)
    w2d_b = jnp.concatenate([cmp_w2, cmp_w2], axis=-1).astype(BF16)
    pe_b = jnp.broadcast_to(cmp_pe.reshape(depth, 2, 1, CMP_LEN * HEAD_DIM), (depth, 2, 8, CMP_LEN * HEAD_DIM)).astype(BF16)

    for l in range(depth):
        m = mod[l, :b]
        shift, scl, gate = (m[:, i * d:(i + 1) * d].reshape(b, 1, d) for i in range(3))
        x = _layer(x, shift, scl, gate, w_in_p, w_out_b, pe_b, w1_b, w2d_b,
                   ln_g[l].reshape(1, d), ln_b[l].reshape(1, d), tables, consts, alpha, l)
    return x
```

```python
import functools
import math

import numpy as np
import jax
import jax.numpy as jnp
from jax import lax
from jax.experimental import pallas as pl
from jax.experimental.pallas import tpu as pltpu

F32 = jnp.float32
BF16 = jnp.bfloat16

D_MODEL = 1024
HEAD_DIM = 64
H_A = 8
H_B = 8
G_B = 2
R_B = H_B // G_B
D_A = H_A * HEAD_DIM
D_B = H_B * HEAD_DIM
DILATED = ((128, 1), (512, 4), (2048, 16))
CMP_LEN = 32
CMP_STRIDE = 16
CMP_HIDDEN = 256
SLC_BLOCK = 64
SLC_TOPK = 16
SLC_LOCAL = 2
WIN = 512
N_BUCKETS = 32
MAX_DIST = 2048
LN_EPS = 1e-5
NEG = -1e30
LOG2E = 1.4426950408889634

LANES = 128
BLK = 128
VMEM_BYTES_V7X = 64 * 1024 * 1024
VMEM_LIMIT = VMEM_BYTES_V7X - 8 * 1024 * 1024

_C_QA, _C_KA, _C_VA, _C_ZA, _C_QB = 0, 512, 1024, 1536, 2048
_C_NARROW = 2560
_C_ZB, _C_END = 3456, 3968
_N_MAIN = 3328


def _cparams(sem):
    return pltpu.CompilerParams(dimension_semantics=sem, vmem_limit_bytes=VMEM_LIMIT)


def _lo_mask():
    return lax.broadcasted_iota(jnp.int32, (1, LANES), 1) < HEAD_DIM


def _dot_nt(a, b):
    return lax.dot_general(a, b, (((1,), (1,)), ((), ())), preferred_element_type=F32)


def _bucket_np(dist):
    d = np.maximum(dist, 0)
    max_exact = N_BUCKETS // 2
    large = max_exact + (np.log(np.maximum(d, 1).astype(np.float32) / np.float32(max_exact))
                         / np.float32(math.log(MAX_DIST / max_exact))
                         * np.float32(N_BUCKETS - max_exact)).astype(np.int32)
    return np.where(d < max_exact, d, np.minimum(large, N_BUCKETS - 1)).astype(np.int32)


def _masked_bias(bias_by_bucket, dist, valid):
    onehot = np.eye(N_BUCKETS, dtype=np.float32)[_bucket_np(dist)]
    t = jnp.einsum("...k,kh->h...", onehot, bias_by_bucket, precision=lax.Precision.HIGHEST)
    return jnp.where(jnp.asarray(valid)[None], t, NEG)


def _toeplitz_kernel(v_ref, o_ref):
    cols = o_ref.shape[-1]
    for e in range(o_ref.shape[1]):
        x = jnp.broadcast_to(v_ref[0, e:e + 1, :], (BLK, v_ref.shape[-1]))
        o_ref[0, e] = pltpu.roll(x, 0, 1, stride=1, stride_axis=0)[:, :cols]


def _toeplitz(h, cols):
    n_h, n_e, n = h.shape
    width = 1 << (n - 1).bit_length()
    vec = jnp.concatenate([h[..., BLK - 1:], jnp.zeros((n_h, n_e, width - n), F32), h[..., :BLK - 1]], axis=-1)
    return pl.pallas_call(
        _toeplitz_kernel,
        grid=(n_h,),
        in_specs=[pl.BlockSpec((1, n_e, width), lambda a: (a, 0, 0))],
        out_specs=pl.BlockSpec((1, n_e, BLK, cols), lambda a: (a, 0, 0, 0)),
        out_shape=jax.ShapeDtypeStruct((n_h, n_e, BLK, cols), F32),
        compiler_params=_cparams(("parallel",)),
        name="toeplitz_table",
    )(vec)


def _band_table(bias_by_bucket, n_prev, window, scale):
    w = (n_prev + 1) * BLK
    dist = n_prev * BLK - (np.arange(BLK - 1 + w) - (BLK - 1))
    h = _masked_bias(bias_by_bucket, dist * scale, (dist >= 0) & (dist <= window))
    return _toeplitz(h[:, None], w)[:, 0]


def _diag_table(bias_by_bucket, n_diag):
    delta = np.arange(-1, n_diag + 1)[:, None]
    dist = delta * BLK - (np.arange(2 * BLK - 1)[None, :] - (BLK - 1))
    return _toeplitz(_masked_bias(bias_by_bucket, dist, dist >= 0), BLK)


def _cmp_table_kernel(base_ref, o_ref, *, rows):
    i = pl.program_id(1)
    n_col = o_ref.shape[-1]
    width = base_ref.shape[-1]
    base = base_ref[0]
    for j in range(rows // CMP_STRIDE):
        chunk = i * (rows // CMP_STRIDE) + j
        shift = (chunk + width - (n_col - 1)) % width
        o_ref[0, j * CMP_STRIDE:(j + 1) * CMP_STRIDE, :] = pltpu.roll(base, shift, 1)[:, :n_col]


def _cmp_table(bias_by_bucket, seq, rows=1024):
    n_col = seq // CMP_STRIDE
    t = np.arange(2 * n_col) - (n_col - 1)
    dist = -CMP_STRIDE * t[None, :] + np.arange(CMP_STRIDE)[:, None] - (CMP_LEN - 1)
    base = _masked_bias(bias_by_bucket, dist, dist >= 0)
    h = base.shape[0]
    return pl.pallas_call(
        functools.partial(_cmp_table_kernel, rows=rows),
        grid=(h, seq // rows),
        in_specs=[pl.BlockSpec((1, CMP_STRIDE, 2 * n_col), lambda a, i: (a, 0, 0))],
        out_specs=pl.BlockSpec((1, rows, n_col), lambda a, i: (a, i, 0)),
        out_shape=jax.ShapeDtypeStruct((h, seq, n_col), F32),
        compiler_params=_cparams(("parallel", "parallel")),
        name="cmp_bias_table",
    )(base)


def _ada_kernel(c_ref, w_ref, b_ref, o_ref):
    o_ref[0] = jnp.dot(c_ref[...], w_ref[0].astype(BF16), preferred_element_type=F32) + b_ref[0]


def _ada_mod(c, w_ada, b_ada):
    depth, d, d3 = w_ada.shape
    cp = jnp.zeros((8, d), BF16).at[:c.shape[0]].set(c.astype(BF16))
    nj = d3 // d
    return pl.pallas_call(
        _ada_kernel,
        grid=(depth, nj),
        in_specs=[pl.BlockSpec((8, d), lambda l, j: (0, 0)),
                  pl.BlockSpec((1, d, d), lambda l, j: (l, 0, j)),
                  pl.BlockSpec((1, 1, d), lambda l, j: (l, 0, j))],
        out_specs=pl.BlockSpec((1, 8, d), lambda l, j: (l, 0, j)),
        out_shape=jax.ShapeDtypeStruct((depth, 8, d3), F32),
        compiler_params=_cparams(("parallel", "parallel")),
        name="ada_mod",
    )(cp, w_ada, b_ada.reshape(depth, 1, d3))


def _proj_kernel(x_ref, sh_ref, sc_ref, w_ref, *refs, dils):
    n_lay = len(dils)
    qa, ka, va = refs[:n_lay], refs[n_lay:2 * n_lay], refs[2 * n_lay:3 * n_lay]
    za, qb, kc, vc, ks, vs, kw, vw, zb, gb, h_ref, xs = refs[3 * n_lay:]
    tm = h_ref.shape[0]
    h_ref[...] = (x_ref[0] * (1.0 + sc_ref[0]) + sh_ref[0]).astype(BF16)

    def mm(c0, n):
        return jnp.dot(h_ref[...], w_ref[:, c0:c0 + n], preferred_element_type=F32)

    def emit_regrouped(a, outs):
        outs[0][0] = a.astype(BF16)
        n_slab = a.shape[1] // LANES
        for c in range(n_slab):
            xs[0, c] = a[:, c * LANES:(c + 1) * LANES]
        d_prev = 1
        for lvl, (ref, dil) in enumerate(zip(outs[1:], dils[1:])):
            ratio, len_prev, len_new = dil // d_prev, tm // d_prev, tm // dil
            src, dst = xs.at[lvl % 2], xs.at[(lvl + 1) % 2]
            keep = lvl + 2 < len(dils)
            for rp in range(d_prev):
                for rs in range(ratio):
                    r = rp + d_prev * rs
                    for c in range(n_slab):
                        rows = src[c, pl.ds(rp * len_prev + rs, len_new, stride=ratio), :]
                        ref[0, r, :, c * LANES:(c + 1) * LANES] = rows.astype(BF16)
                        if keep:
                            dst[c, r * len_new:(r + 1) * len_new, :] = rows
            d_prev = dil

    qs = HEAD_DIM ** -0.5 * LOG2E
    emit_regrouped(mm(_C_QA, D_A) * qs, qa)
    emit_regrouped(mm(_C_KA, D_A), ka)
    emit_regrouped(mm(_C_VA, D_A), va)
    za[0] = mm(_C_ZA, D_A).astype(BF16)
    qb[0] = (mm(_C_QB, D_B) * qs).astype(BF16)
    lo = _lo_mask()
    narrow = mm(_C_NARROW, 7 * LANES)
    piece = lambda i: narrow[:, i * LANES:(i + 1) * LANES]
    for ref, i in ((kc, 0), (vc, 1)):
        xs[0, 0] = piece(i)
        for t in range(0, CMP_STRIDE, 2):
            x0 = xs[0, 0, pl.ds(t, tm // CMP_STRIDE, stride=CMP_STRIDE), :]
            x1 = xs[0, 0, pl.ds(t + 1, tm // CMP_STRIDE, stride=CMP_STRIDE), :]
            cols = slice(t * HEAD_DIM, (t + 2) * HEAD_DIM)
            ref[0, 0, :, cols] = jnp.where(lo, x0, pltpu.roll(x1, HEAD_DIM, 1)).astype(BF16)
            ref[0, 1, :, cols] = jnp.where(lo, pltpu.roll(x0, HEAD_DIM, 1), x1).astype(BF16)
    for ref, i in ((ks, 2), (vs, 3), (kw, 4), (vw, 5)):
        a = piece(i)
        r = pltpu.roll(a, HEAD_DIM, 1)
        ref[0, :, :LANES] = jnp.where(lo, a, r).astype(BF16)
        ref[0, :, LANES:] = jnp.where(lo, r, a).astype(BF16)
    gb[0] = piece(6)
    zb[0] = mm(_C_ZB, D_B).astype(BF16)


def _project(x, shift, scl, w, layer, tm=512):
    b, s, d = x.shape
    dils = tuple(dl for _, dl in DILATED)
    bs = lambda n: pl.BlockSpec((1, tm, n), lambda i, j: (i, j, 0))
    gs = pl.BlockSpec((1, G_B, tm // CMP_STRIDE, CMP_STRIDE * HEAD_DIM), lambda i, j: (i, 0, j, 0))
    sd = lambda n, dt=BF16: jax.ShapeDtypeStruct((b, s, n), dt)
    gd = jax.ShapeDtypeStruct((b, G_B, s // CMP_STRIDE, CMP_STRIDE * HEAD_DIM), BF16)
    mod = pl.BlockSpec((1, 1, d), lambda i, j: (i, 0, 0))
    lay_specs = [bs(D_A)] + [pl.BlockSpec((1, dl, tm // dl, D_A), lambda i, j: (i, 0, j, 0)) for dl in dils[1:]]
    lay_shapes = [sd(D_A)] + [jax.ShapeDtypeStruct((b, dl, s // dl, D_A), BF16) for dl in dils[1:]]
    outs = pl.pallas_call(
        functools.partial(_proj_kernel, dils=dils),
        grid=(b, s // tm),
        in_specs=[pl.BlockSpec((1, tm, d), lambda i, j: (i, j, 0)), mod, mod,
                  pl.BlockSpec((None, d, _C_END), lambda i, j: (layer, 0, 0))],
        out_specs=lay_specs * 3 + [bs(D_A), bs(D_B), gs, gs,
                                   bs(2 * LANES), bs(2 * LANES), bs(2 * LANES), bs(2 * LANES), bs(D_B), bs(LANES)],
        out_shape=lay_shapes * 3 + [sd(D_A), sd(D_B), gd, gd,
                                    sd(2 * LANES), sd(2 * LANES), sd(2 * LANES), sd(2 * LANES), sd(D_B), sd(LANES, F32)],
        scratch_shapes=[pltpu.VMEM((tm, d), BF16), pltpu.VMEM((2, D_A // LANES, tm, LANES), F32)],
        compiler_params=_cparams(("parallel", "parallel")),
        name="in_proj",
    )(x, shift, scl, w)
    n = len(dils)
    return (outs[:n], outs[n:2 * n], outs[2 * n:3 * n], *outs[3 * n:])


def _dilated_kernel(*refs, seq, unroll):
    n_pat = len(DILATED)
    qkv = refs[:3 * n_pat]
    t_refs = refs[3 * n_pat:4 * n_pat]
    o_ref = refs[4 * n_pat]
    qlo, qhi, kp, vp, o_r, m_r, l_r, o_t, m_t, l_t, m_acc, n_acc, d_acc = refs[4 * n_pat + 1:]
    lo = _lo_mask()
    nblk = seq // BLK
    chunk = 512
    kp[:BLK, :] = jnp.zeros((BLK, LANES), BF16)
    vp[:BLK, :] = jnp.zeros((BLK, LANES), BF16)

    for pi, ((_, dil), t_ref) in enumerate(zip(DILATED, t_refs)):
        q_ref, k_ref, v_ref = qkv[pi], qkv[n_pat + pi], qkv[2 * n_pat + pi]
        rl = seq // dil
        nb = rl // BLK
        for r in range(dil):
            idx = (0,) if dil == 1 else (0, r)
            q = q_ref[idx]
            zero = jnp.zeros_like(q)
            qlo[r * rl:(r + 1) * rl, :] = jnp.where(lo, q, zero)
            qhi[r * rl:(r + 1) * rl, :] = jnp.where(lo, zero, q)
            kp[BLK + r * rl:BLK + (r + 1) * rl, :] = k_ref[idx]
            vp[BLK + r * rl:BLK + (r + 1) * rl, :] = v_ref[idx]
        o_dst, m_dst, l_dst = (n_acc, m_acc, d_acc) if dil == 1 else (o_r, m_r, l_r)

        def blocks(it, carry, t_ref=t_ref, nb=nb, o_dst=o_dst, m_dst=m_dst, l_dst=l_dst):
            for u in range(unroll):
                g = it * unroll + u
                first = jnp.where((g & (nb - 1)) == 0, 1, 0)
                q0 = pl.multiple_of(g * BLK, BLK)
                qs = jnp.concatenate([qlo[pl.ds(q0, BLK), :], qhi[pl.ds(q0, BLK), :]], axis=0)
                s = _dot_nt(qs, kp[pl.ds(q0, 2 * BLK), :]) + t_ref[0, first]
                m = jnp.max(s, axis=-1, keepdims=True)
                p = jnp.exp2(s - m)
                l = jnp.sum(p, axis=-1, keepdims=True)
                pv = jnp.dot(p.astype(BF16), vp[pl.ds(q0, 2 * BLK), :], preferred_element_type=F32)
                o_dst[pl.ds(q0, BLK), :] = jnp.where(lo, pv[:BLK], pv[BLK:])
                m_dst[pl.ds(q0, BLK), :] = jnp.where(lo, m[:BLK], m[BLK:])
                l_dst[pl.ds(q0, BLK), :] = jnp.where(lo, l[:BLK], l[BLK:])
            return carry

        lax.fori_loop(0, nblk // unroll, blocks, 0)
        if dil == 1:
            continue

        for r in range(dil):
            for src, dst in ((o_r, o_t), (m_r, m_t), (l_r, l_t)):
                dst[pl.ds(r, rl, stride=dil), :] = src[r * rl:(r + 1) * rl, :]

        def fold(c, carry, last=(pi == n_pat - 1)):
            rows = pl.ds(pl.multiple_of(c * chunk, chunk), chunk)
            m_old, m_new = m_acc[rows, :], m_t[rows, :]
            mx = jnp.maximum(m_old, m_new)
            a, b_ = jnp.exp2(m_old - mx), jnp.exp2(m_new - mx)
            n_new = a * n_acc[rows, :] + b_ * o_t[rows, :]
            d_new = a * d_acc[rows, :] + b_ * l_t[rows, :]
            if last:
                o_ref[0, rows, :] = (n_new * (1.0 / d_new)).astype(o_ref.dtype)
            else:
                m_acc[rows, :] = mx
                n_acc[rows, :] = n_new
                d_acc[rows, :] = d_new
            return carry

        lax.fori_loop(0, seq // chunk, fold, 0)


def _dilated_attention(qa, ka, va, tables, unroll=32):
    assert DILATED[0][1] == 1 and all(dl > 1 for _, dl in DILATED[1:])
    b, s, c = qa[0].shape
    xs = pl.BlockSpec((1, s, LANES), lambda i, j: (i, 0, j))
    lay = [xs] + [pl.BlockSpec((1, dl, s // dl, LANES), lambda i, j: (i, 0, 0, j)) for _, dl in DILATED[1:]]
    ts = pl.BlockSpec((1, 2, 2 * BLK, 2 * BLK), lambda i, j: (j, 0, 0, 0))
    big = lambda rows, dt: pltpu.VMEM((rows, LANES), dt)
    return pl.pallas_call(
        functools.partial(_dilated_kernel, seq=s, unroll=unroll),
        grid=(b, c // LANES),
        in_specs=lay * 3 + [ts] * len(DILATED),
        out_specs=xs,
        out_shape=jax.ShapeDtypeStruct((b, s, c), BF16),
        scratch_shapes=[big(s, BF16)] * 2 + [big(s + BLK, BF16)] * 2 + [big(s, F32)] * 9,
        compiler_params=_cparams(("parallel", "parallel")),
        name="dilated_attn",
    )(*qa, *ka, *va, *tables)


def _win_kernel(q_ref, k_ref, v_ref, t_ref, o_ref, qst, kaug, vp, *, seq, n_prev, unroll):
    lo = _lo_mask()
    lane = lax.broadcasted_iota(jnp.int32, (1, LANES), 1)
    flag = lane == HEAD_DIM
    pad = n_prev * BLK
    span = (n_prev + 1) * BLK
    negflag = jnp.where(flag, NEG, 0.0)
    for pr in range(R_B // 2):
        q = q_ref[0, :, pr * LANES:(pr + 1) * LANES].astype(F32)
        qst[2 * pr] = jnp.where(lo, q, negflag).astype(BF16)
        qst[2 * pr + 1] = jnp.where(lo, pltpu.roll(q, HEAD_DIM, 1), negflag).astype(BF16)
    k = k_ref[0]
    kaug[:pad, :] = jnp.broadcast_to(jnp.where(flag, 1.0, 0.0), (pad, LANES)).astype(BF16)
    kaug[pad:, :] = jnp.where(lo, k, jnp.zeros_like(k))
    vp[:pad, :] = jnp.zeros((pad, LANES), BF16)
    vp[pad:, :] = jnp.where(lo, v_ref[0], jnp.where(flag, 1.0, 0.0).astype(BF16))

    def blocks(it, carry):
        for u in range(unroll):
            n = it * unroll + u
            r0 = pl.multiple_of(n * BLK, BLK)
            qs = jnp.concatenate([qst[h, pl.ds(r0, BLK), :] for h in range(R_B)], axis=0)
            s = _dot_nt(qs, kaug[pl.ds(r0, span), :]) + t_ref[0]
            m = jnp.max(s, axis=-1, keepdims=True)
            p = jnp.exp2(s - m)
            pv = jnp.dot(p.astype(BF16), vp[pl.ds(r0, span), :], preferred_element_type=F32)
            pv = pv * (1.0 / pv[:, HEAD_DIM:HEAD_DIM + 1])
            for pr in range(R_B // 2):
                even, odd = pv[2 * pr * BLK:(2 * pr + 1) * BLK], pv[(2 * pr + 1) * BLK:(2 * pr + 2) * BLK]
                o = jnp.where(lo, even, pltpu.roll(odd, HEAD_DIM, 1))
                o_ref[0, pl.ds(r0, BLK), pr * LANES:(pr + 1) * LANES] = o.astype(o_ref.dtype)
        return carry

    lax.fori_loop(0, seq // BLK // unroll, blocks, 0)


def _win_attention(qb, kw_dup, vw_dup, table, n_prev, unroll=8):
    b, s, _ = qb.shape
    gw = R_B * HEAD_DIM
    w = table.shape[-1]
    return pl.pallas_call(
        functools.partial(_win_kernel, seq=s, n_prev=n_prev, unroll=unroll),
        grid=(b, G_B),
        in_specs=[pl.BlockSpec((1, s, gw), lambda i, g: (i, 0, g)),
                  pl.BlockSpec((1, s, LANES), lambda i, g: (i, 0, g)),
                  pl.BlockSpec((1, s, LANES), lambda i, g: (i, 0, g)),
                  pl.BlockSpec((1, R_B * BLK, w), lambda i, g: (g, 0, 0))],
        out_specs=pl.BlockSpec((1, s, gw), lambda i, g: (i, 0, g)),
        out_shape=jax.ShapeDtypeStruct((b, s, D_B), BF16),
        scratch_shapes=[pltpu.VMEM((R_B, s, LANES), BF16)] + [pltpu.VMEM((s + n_prev * BLK, LANES), BF16)] * 2,
        compiler_params=_cparams(("parallel", "parallel")),
        name="nsa_window",
    )(qb, kw_dup, vw_dup, table)


def _compress_kernel(kin, vin, w1, w2, pe, kout, vout):
    n_chunk = kin.shape[2]
    half = w1.shape[1] // 2
    rows = lax.broadcasted_iota(jnp.int32, (n_chunk, 1), 0)
    for t, (xin, out) in enumerate(((kin, kout), (vin, vout))):
        x = xin[0, 0]
        a = jnp.dot(x, w1[t, :half], preferred_element_type=F32)
        b = jnp.dot(x, w1[t, half:], preferred_element_type=F32)
        c = jnp.dot(pe[t], w1[t], preferred_element_type=F32)[0:1]
        hid = a + pltpu.roll(b, n_chunk - 1, 0) + c
        act = hid * jax.nn.sigmoid(hid)
        o = jnp.dot(act.astype(BF16), w2[t], preferred_element_type=F32)
        out[0, 0] = jnp.where(rows < n_chunk - 1, o, 0.0).astype(BF16)


def _compress(kc_in, vc_in, w1, w2d, pe, layer):
    b, g, n_chunk, f = kc_in.shape
    xs = pl.BlockSpec((1, 1, n_chunk, f), lambda i, j: (i, j, 0, 0))
    os_ = pl.BlockSpec((1, 1, n_chunk, LANES), lambda i, j: (i, j, 0, 0))
    full = lambda a: pl.BlockSpec((None,) + a.shape[1:], lambda i, j: (layer,) + (0,) * (a.ndim - 1))
    od = jax.ShapeDtypeStruct((b, g, n_chunk, LANES), BF16)
    return pl.pallas_call(
        _compress_kernel,
        grid=(b, g),
        in_specs=[xs, xs, full(w1), full(w2d), full(pe)],
        out_specs=[os_, os_],
        out_shape=[od, od],
        compiler_params=_cparams(("parallel", "parallel")),
        name="nsa_compress",
    )(kc_in, vc_in, w1, w2d, pe)


def _cmp_kernel(q_ref, kc_ref, vc_ref, t_ref, ov_ref, o_ref, sb_ref, *, tq):
    i = pl.program_id(2)
    lo = _lo_mask()
    n_col, n_slc = kc_ref.shape[2], ov_ref.shape[0]

    def attend_and_rank(ncol, nsel):
        kcd = kc_ref[0, 0, :ncol, :]
        vcd = vc_ref[0, 0, :ncol, :]
        zero = jnp.zeros_like(vcd)
        vlo = jnp.where(lo, vcd, zero)
        vhi = jnp.where(lo, zero, vcd)
        psum = None
        for pr in range(R_B // 2):
            q = q_ref[0, :, pr * LANES:(pr + 1) * LANES]
            qz = jnp.zeros_like(q)
            acc = None
            for hh, (qm, vm) in enumerate(((jnp.where(lo, q, qz), vlo), (jnp.where(lo, qz, q), vhi))):
                s = _dot_nt(qm, kcd) + t_ref[2 * pr + hh, :, :ncol]
                m = jnp.max(s, axis=-1, keepdims=True)
                e = jnp.exp2(s - m)
                l = jnp.sum(e, axis=-1, keepdims=True)
                p = e * jnp.where(m > 0.5 * NEG, 1.0 / l, 0.0)
                psum = p if psum is None else psum + p
                pv = jnp.dot(p.astype(BF16), vm, preferred_element_type=F32)
                acc = pv if acc is None else acc + pv
            o_ref[0, :, pr * LANES:(pr + 1) * LANES] = acc.astype(o_ref.dtype)

        p_hi = psum.astype(BF16)
        p_lo = (psum - p_hi.astype(F32)).astype(BF16)
        ov = ov_ref[:nsel, :ncol]
        score = (_dot_nt(ov, p_hi) + _dot_nt(ov, p_lo))
        blk = lax.broadcasted_iota(jnp.int32, (nsel, 1), 0)
        pos = i * tq + lax.broadcasted_iota(jnp.int32, (1, tq), 1)
        cur = pos >> int(math.log2(SLC_BLOCK))
        forced = (blk == 0) | ((cur - blk >= 0) & (cur - blk < SLC_LOCAL))
        score = jnp.where(forced, 1e9, jnp.where(blk > cur, -1e9, score))
        sub = lax.broadcasted_iota(jnp.int32, (8, 1), 0)
        sel_rows = []
        for g8 in range(nsel // 8):
            sg = score[g8 * 8:(g8 + 1) * 8]
            rank = jnp.zeros(sg.shape, F32)
            for mp in range(nsel):
                row = score[mp:mp + 1]
                if mp < g8 * 8:
                    ahead = jnp.where(row >= sg, 1.0, 0.0)
                elif mp >= (g8 + 1) * 8:
                    ahead = jnp.where(row > sg, 1.0, 0.0)
                else:
                    ahead = jnp.where(sub > mp - g8 * 8, jnp.where(row >= sg, 1.0, 0.0), jnp.where(row > sg, 1.0, 0.0))
                rank = rank + ahead
            sel_rows.append(jnp.where(rank < float(min(SLC_TOPK, n_slc)), 0.0, NEG))
        if nsel < n_slc:
            sel_rows.append(jnp.full((n_slc - nsel, tq), NEG, F32))
        pad = LANES // n_slc
        selb = jnp.concatenate(sel_rows * pad, axis=0)
        sb_ref[0, 0] = selb.T.astype(BF16)

    n_tile = pl.num_programs(2)
    for k in range(4):
        ncol = max(LANES, -(-(n_col * (k + 1) // 4) // LANES) * LANES)
        in_quarter = (i * 4 >= k * n_tile) & (i * 4 < (k + 1) * n_tile)
        pl.when(in_quarter)(functools.partial(attend_and_rank, ncol, n_slc * (k + 1) // 4))


def _cmp_attention(qb, kcd, vcd, table, ov_t, tq=512):
    b, s, _ = qb.shape
    n_col = kcd.shape[2]
    gw = R_B * HEAD_DIM
    return pl.pallas_call(
        functools.partial(_cmp_kernel, tq=tq),
        grid=(b, G_B, s // tq),
        in_specs=[pl.BlockSpec((1, tq, gw), lambda i, g, j: (i, j, g)),
                  pl.BlockSpec((1, 1, n_col, LANES), lambda i, g, j: (i, g, 0, 0)),
                  pl.BlockSpec((1, 1, n_col, LANES), lambda i, g, j: (i, g, 0, 0)),
                  pl.BlockSpec((R_B, tq, n_col), lambda i, g, j: (g, j, 0)),
                  pl.BlockSpec(ov_t.shape, lambda i, g, j: (0, 0))],
        out_specs=[pl.BlockSpec((1, tq, gw), lambda i, g, j: (i, j, g)),
                   pl.BlockSpec((1, 1, tq, LANES), lambda i, g, j: (i, g, j, 0))],
        out_shape=[jax.ShapeDtypeStruct((b, s, D_B), BF16),
                   jax.ShapeDtypeStruct((b, G_B, s, LANES), BF16)],
        compiler_params=_cparams(("parallel", "parallel", "parallel")),
        name="nsa_cmp_topk",
    )(qb, kcd, vcd, table, ov_t)


def _sel_kernel(q_ref, sb_ref, k_ref, v_ref, t_ref, o_ref, kaug, vp, m_ref, acc_ref, *, nblk):
    lo = _lo_mask()
    s_len = nblk * BLK
    n_diag = t_ref.shape[1] - 1
    h0 = pl.program_id(1) * R_B
    rowblk = lax.broadcasted_iota(jnp.int32, (s_len, LANES), 0) >> int(math.log2(SLC_BLOCK))
    lane = lax.broadcasted_iota(jnp.int32, (s_len, LANES), 1)
    onehot = jnp.where(lane - HEAD_DIM == rowblk, 1.0, 0.0).astype(BF16)
    kaug[...] = jnp.where(lo, k_ref[0], onehot)
    vp[...] = jnp.where(lo, v_ref[0], jnp.where(lane == HEAD_DIM, 1.0, 0.0).astype(BF16))
    m_ref[...] = jnp.full(m_ref.shape, NEG, F32)
    acc_ref[...] = jnp.zeros(acc_ref.shape, F32)
    kb = 4
    qb = 2
    sub = kb // qb
    kt_rows, qt_rows = kb * BLK, qb * BLK

    def ktile(t, carry):
        k0 = pl.multiple_of(t * kt_rows, kt_rows)
        kt = kaug[pl.ds(k0, kt_rows), :]
        vt = vp[pl.ds(k0, kt_rows), :]

        def qstep(mp, c):
            scores = []
            for u in range(sub):
                r0 = pl.multiple_of((sub * mp + u) * qt_rows, qt_rows)
                sbf = sb_ref[0, 0, pl.ds(r0, qt_rows), :].astype(F32)
                rows = []
                for pr in range(R_B // 2):
                    q = q_ref[0, pl.ds(r0, qt_rows), pr * LANES:(pr + 1) * LANES].astype(F32)
                    rows.append(jnp.where(lo, q, sbf))
                    rows.append(jnp.where(lo, pltpu.roll(q, HEAD_DIM, 1), sbf))
                qst = jnp.concatenate(rows, axis=0).astype(BF16)
                bias = jnp.concatenate([
                    jnp.concatenate([t_ref[h0 + h, jnp.clip(qb * (sub * mp + u) + a - kb * t - c + 1, 0, n_diag)]
                                     for c in range(kb)], axis=1)
                    for h in range(R_B) for a in range(qb)], axis=0)
                scores.append(_dot_nt(qst, kt) + bias)
            for u in range(sub):
                r0 = pl.multiple_of((sub * mp + u) * qt_rows, qt_rows)
                s = scores[u]
                ps, alphas = [], []
                for h in range(R_B):
                    for a in range(qb):
                        lo_r = h * qt_rows + a * BLK
                        st = pl.ds(r0 + a * BLK, BLK)
                        sh = s[lo_r:lo_r + BLK]
                        m_prev = m_ref[h, st, :]
                        m_new = jnp.maximum(m_prev, jnp.max(sh, axis=-1, keepdims=True))
                        alpha = jnp.exp2(m_prev - m_new)
                        p = jnp.exp2(sh - jnp.concatenate([m_new] * kb, axis=1))
                        m_ref[h, st, :] = m_new
                        ps.append(p.astype(BF16))
                        alphas.append(alpha)
                pv = jnp.dot(jnp.concatenate(ps, axis=0), vt, preferred_element_type=F32)
                for h in range(R_B):
                    for a in range(qb):
                        lo_r = h * qt_rows + a * BLK
                        st = pl.ds(r0 + a * BLK, BLK)
                        acc_ref[h, st, :] = alphas[qb * h + a] * acc_ref[h, st, :] + pv[lo_r:lo_r + BLK]
            return c

        lax.fori_loop(t, nblk // kb, qstep, 0)
        return carry

    lax.fori_loop(0, nblk // kb, ktile, 0)

    def finish(n, carry):
        st = pl.ds(pl.multiple_of(n * BLK, BLK), BLK)
        for pr in range(R_B // 2):
            even, odd = acc_ref[2 * pr, st, :], acc_ref[2 * pr + 1, st, :]
            even = even * (1.0 / even[:, HEAD_DIM:HEAD_DIM + 1])
            odd = odd * (1.0 / odd[:, HEAD_DIM:HEAD_DIM + 1])
            o = jnp.where(lo, even, pltpu.roll(odd, HEAD_DIM, 1))
            o_ref[0, st, pr * LANES:(pr + 1) * LANES] = o.astype(o_ref.dtype)
        return carry

    lax.fori_loop(0, nblk, finish, 0)


def _sel_attention(qb, selb, ks_dup, vs_dup, table):
    b, s, _ = qb.shape
    gw = R_B * HEAD_DIM
    return pl.pallas_call(
        functools.partial(_sel_kernel, nblk=s // BLK),
        grid=(b, G_B),
        in_specs=[pl.BlockSpec((1, s, gw), lambda i, g: (i, 0, g)),
                  pl.BlockSpec((1, 1, s, LANES), lambda i, g: (i, g, 0, 0)),
                  pl.BlockSpec((1, s, LANES), lambda i, g: (i, 0, g)),
                  pl.BlockSpec((1, s, LANES), lambda i, g: (i, 0, g)),
                  pl.BlockSpec(table.shape, lambda i, g: (0, 0, 0, 0), pipeline_mode=pl.Buffered(1))],
        out_specs=pl.BlockSpec((1, s, gw), lambda i, g: (i, 0, g)),
        out_shape=jax.ShapeDtypeStruct((b, s, D_B), BF16),
        scratch_shapes=[pltpu.VMEM((s, LANES), BF16)] * 2 + [pltpu.VMEM((R_B, s, LANES), F32)] * 2,
        compiler_params=_cparams(("parallel", "parallel")),
        name="nsa_selected",
    )(qb, selb, ks_dup, vs_dup, table)


def _combine_kernel(x_ref, oa, za, oc, osl, ow, gb, zb, wo, ex, gate, lng, lnb, out_ref, *, alpha):
    f = lambda r: r[0].astype(F32)
    z = f(za)
    mix_a = f(oa) * (z * jax.nn.sigmoid(z))
    g = jax.nn.sigmoid(gb[0])
    g_hi = g.astype(BF16)
    g_lo = (g - g_hi.astype(F32)).astype(BF16)
    gx = (jnp.dot(g_hi, ex[...], preferred_element_type=F32)
          + jnp.dot(g_lo, ex[...], preferred_element_type=F32))
    ob = gx[:, :D_B] * f(oc) + gx[:, D_B:2 * D_B] * f(osl) + gx[:, 2 * D_B:] * f(ow)
    z = f(zb)
    mix_b = ob * (z * jax.nn.sigmoid(z))
    y = (jnp.dot(mix_a.astype(BF16), wo[:D_A], preferred_element_type=F32)
         + jnp.dot(mix_b.astype(BF16), wo[D_A:], preferred_element_type=F32))
    r = alpha * x_ref[0] + (1.0 + gate[0]) * y
    mu = jnp.mean(r, axis=-1, keepdims=True)
    rc = r - mu
    var = jnp.mean(rc * rc, axis=-1, keepdims=True)
    out_ref[0] = rc * lax.rsqrt(var + LN_EPS) * lng[...] + lnb[...]


def _combine(x, o_a, za, o_cmp, o_slc, o_win, gb, zb, w_out, expand, gate, ln_g, ln_b, alpha, layer, tm=512):
    b, s, d = x.shape
    bs = lambda n: pl.BlockSpec((1, tm, n), lambda i, j: (i, j, 0))
    full = lambda a: pl.BlockSpec(a.shape, lambda i, j: (0,) * a.ndim)
    return pl.pallas_call(
        functools.partial(_combine_kernel, alpha=alpha),
        grid=(b, s // tm),
        in_specs=[bs(d)] + [bs(D_A)] * 2 + [bs(D_B)] * 3 + [bs(LANES), bs(D_B),
                  pl.BlockSpec((None,) + w_out.shape[1:], lambda i, j: (layer, 0, 0)), full(expand),
                  pl.BlockSpec((1, 1, d), lambda i, j: (i, 0, 0)), full(ln_g), full(ln_b)],
        out_specs=bs(d),
        out_shape=jax.ShapeDtypeStruct((b, s, d), F32),
        compiler_params=_cparams(("parallel", "parallel")),
        name="merge_out_proj_ln",
    )(x, o_a, za, o_cmp, o_slc, o_win, gb, zb, w_out, expand, gate, ln_g, ln_b)


def _pack_w_in(w_in):
    n_gb = 3 * H_B
    main = w_in[..., :_N_MAIN]
    gbw = w_in[..., _N_MAIN:_N_MAIN + n_gb]
    zbw = w_in[..., _N_MAIN + n_gb:]
    pad = jnp.zeros(w_in.shape[:-1] + (LANES - n_gb,), w_in.dtype)
    return jnp.concatenate([main, gbw, pad, zbw], axis=-1).astype(BF16)


def _gate_expand():
    e = np.zeros((LANES, 3 * D_B), np.float32)
    for i in range(3):
        for h in range(H_B):
            e[i * H_B + h, i * D_B + h * HEAD_DIM:i * D_B + (h + 1) * HEAD_DIM] = 1.0
    return jnp.asarray(e, BF16)


def _overlap_t(n_col, n_slc):
    cs = np.arange(n_col)[None, :] * CMP_STRIDE
    ss = np.arange(n_slc)[:, None] * SLC_BLOCK
    return jnp.asarray(((cs < ss + SLC_BLOCK) & (cs + CMP_LEN > ss)).astype(np.float32), BF16)


def _layer(x, shift, scl, gate, w_in_p, w_out_b, pe_b, w1_b, w2d_b, ln_g, ln_b, tables, consts, alpha, layer):
    b, s, d = x.shape
    (qa, ka, va, za, qb, kc_in, vc_in, ks, vs, kw, vw, zb, gb) = _project(x, shift, scl, w_in_p, layer)

    o_a = _dilated_attention(qa, ka, va, tables["dilated"])

    kcd, vcd = _compress(kc_in, vc_in, w1_b, w2d_b, pe_b, layer)
    o_cmp, selb = _cmp_attention(qb, kcd, vcd, tables["cmp"], consts["overlap_t"])
    o_slc = _sel_attention(qb, selb, ks, vs, tables["sel"])
    o_win = _win_attention(qb, kw, vw, tables["win"], n_prev=-(-(WIN - 1) // BLK))

    return _combine(x, o_a, za, o_cmp, o_slc, o_win, gb, zb, w_out_b, consts["expand"], gate,
                    ln_g, ln_b, alpha, layer)


def kernel(x, c, w_in, w_out, cmp_pe, cmp_w1, cmp_w2, w_ada, b_ada, ln_g, ln_b, rel_bias):
    b, s, d = x.shape
    depth = w_in.shape[0]
    alpha = (2 * depth) ** 0.25
    n_pair = D_A // LANES

    bb = rel_bias.astype(F32) * LOG2E
    def dilated_table(window, dil):
        t = _band_table(bb[:, :H_A], 1, window // dil, dil).reshape(n_pair, 2 * BLK, 2 * BLK)
        first = t.at[..., :BLK].set(NEG)
        return jnp.stack([t, first], axis=1)

    n_prev_win = -(-(WIN - 1) // BLK)
    tables = {
        "dilated": [dilated_table(w, dl) for w, dl in DILATED],
        "win": _band_table(bb[:, H_A:], n_prev_win, WIN - 1, 1).reshape(G_B, R_B * BLK, (n_prev_win + 1) * BLK),
        "cmp": _cmp_table(bb[:, H_A:], s),
        "sel": _diag_table(bb[:, H_A:], 13),
    }
    consts = {"expand": _gate_expand(), "overlap_t": _overlap_t(s // CMP_STRIDE, s // SLC_BLOCK)}

    mod = _ada_mod(c, w_ada, b_ada)
    w_in_p = _pack_w_in(w_in)
    w_out_b = w_out.astype(BF16)
    w1_b = cmp_w1.astype(BF16)
    w2d_b = jnp.concatenate([cmp_w2, cmp_w2], axis=-1).astype(BF16)
    pe_b = jnp.broadcast_to(cmp_pe.reshape(depth, 2, 1, CMP_LEN * HEAD_DIM), (depth, 2, 8, CMP_LEN * HEAD_DIM)).astype(BF16)

    for l in range(depth):
        m = mod[l, :b]
        shift, scl, gate = (m[:, i * d:(i + 1) * d].reshape(b, 1, d) for i in range(3))
        x = _layer(x, shift, scl, gate, w_in_p, w_out_b, pe_b, w1_b, w2d_b,
                   ln_g[l].reshape(1, d), ln_b[l].reshape(1, d), tables, consts, alpha, l)
    return x
```

```python
import functools
import math

import numpy as np
import jax
import jax.numpy as jnp
from jax import lax
from jax.experimental import pallas as pl
from jax.experimental.pallas import tpu as pltpu

F32 = jnp.float32
BF16 = jnp.bfloat16

D_MODEL = 1024
HEAD_DIM = 64
H_A = 8
H_B = 8
G_B = 2
R_B = H_B // G_B
D_A = H_A * HEAD_DIM
D_B = H_B * HEAD_DIM
DILATED = ((128, 1), (512, 4), (2048, 16))
CMP_LEN = 32
CMP_STRIDE = 16
CMP_HIDDEN = 256
SLC_BLOCK = 64
SLC_TOPK = 16
SLC_LOCAL = 2
WIN = 512
N_BUCKETS = 32
MAX_DIST = 2048
LN_EPS = 1e-5
NEG = -1e30
LOG2E = 1.4426950408889634

LANES = 128
BLK = 128
VMEM_BYTES_V7X = 64 * 1024 * 1024
VMEM_LIMIT = VMEM_BYTES_V7X - 8 * 1024 * 1024

_C_QA, _C_KA, _C_VA, _C_ZA, _C_QB = 0, 512, 1024, 1536, 2048
_C_NARROW = 2560
_C_ZB, _C_END = 3456, 3968
_N_MAIN = 3328


def _cparams(sem):
    return pltpu.CompilerParams(dimension_semantics=sem, vmem_limit_bytes=VMEM_LIMIT)


def _lo_mask():
    return lax.broadcasted_iota(jnp.int32, (1, LANES), 1) < HEAD_DIM


def _dot_nt(a, b):
    return lax.dot_general(a, b, (((1,), (1,)), ((), ())), preferred_element_type=F32)


def _bucket_np(dist):
    d = np.maximum(dist, 0)
    max_exact = N_BUCKETS // 2
    large = max_exact + (np.log(np.maximum(d, 1).astype(np.float32) / np.float32(max_exact))
                         / np.float32(math.log(MAX_DIST / max_exact))
                         * np.float32(N_BUCKETS - max_exact)).astype(np.int32)
    return np.where(d < max_exact, d, np.minimum(large, N_BUCKETS - 1)).astype(np.int32)


def _masked_bias(bias_by_bucket, dist, valid):
    onehot = np.eye(N_BUCKETS, dtype=np.float32)[_bucket_np(dist)]
    t = jnp.einsum("...k,kh->h...", onehot, bias_by_bucket, precision=lax.Precision.HIGHEST)
    return jnp.where(jnp.asarray(valid)[None], t, NEG)


def _toeplitz_kernel(v_ref, o_ref):
    cols = o_ref.shape[-1]
    for e in range(o_ref.shape[1]):
        x = jnp.broadcast_to(v_ref[0, e:e + 1, :], (BLK, v_ref.shape[-1]))
        o_ref[0, e] = pltpu.roll(x, 0, 1, stride=1, stride_axis=0)[:, :cols]


def _toeplitz(h, cols):
    n_h, n_e, n = h.shape
    width = 1 << (n - 1).bit_length()
    vec = jnp.concatenate([h[..., BLK - 1:], jnp.zeros((n_h, n_e, width - n), F32), h[..., :BLK - 1]], axis=-1)
    return pl.pallas_call(
        _toeplitz_kernel,
        grid=(n_h,),
        in_specs=[pl.BlockSpec((1, n_e, width), lambda a: (a, 0, 0))],
        out_specs=pl.BlockSpec((1, n_e, BLK, cols), lambda a: (a, 0, 0, 0)),
        out_shape=jax.ShapeDtypeStruct((n_h, n_e, BLK, cols), F32),
        compiler_params=_cparams(("parallel",)),
        name="toeplitz_table",
    )(vec)


def _band_table(bias_by_bucket, n_prev, window, scale):
    w = (n_prev + 1) * BLK
    dist = n_prev * BLK - (np.arange(BLK - 1 + w) - (BLK - 1))
    h = _masked_bias(bias_by_bucket, dist * scale, (dist >= 0) & (dist <= window))
    return _toeplitz(h[:, None], w)[:, 0]


def _diag_table(bias_by_bucket, n_diag):
    delta = np.arange(-1, n_diag + 1)[:, None]
    dist = delta * BLK - (np.arange(2 * BLK - 1)[None, :] - (BLK - 1))
    return _toeplitz(_masked_bias(bias_by_bucket, dist, dist >= 0), BLK)


def _cmp_table_kernel(base_ref, o_ref, *, rows):
    i = pl.program_id(1)
    n_col = o_ref.shape[-1]
    width = base_ref.shape[-1]
    base = base_ref[0]
    for j in range(rows // CMP_STRIDE):
        chunk = i * (rows // CMP_STRIDE) + j
        shift = (chunk + width - (n_col - 1)) % width
        o_ref[0, j * CMP_STRIDE:(j + 1) * CMP_STRIDE, :] = pltpu.roll(base, shift, 1)[:, :n_col]


def _cmp_table(bias_by_bucket, seq, rows=1024):
    n_col = seq // CMP_STRIDE
    t = np.arange(2 * n_col) - (n_col - 1)
    dist = -CMP_STRIDE * t[None, :] + np.arange(CMP_STRIDE)[:, None] - (CMP_LEN - 1)
    base = _masked_bias(bias_by_bucket, dist, dist >= 0)
    h = base.shape[0]
    return pl.pallas_call(
        functools.partial(_cmp_table_kernel, rows=rows),
        grid=(h, seq // rows),
        in_specs=[pl.BlockSpec((1, CMP_STRIDE, 2 * n_col), lambda a, i: (a, 0, 0))],
        out_specs=pl.BlockSpec((1, rows, n_col), lambda a, i: (a, i, 0)),
        out_shape=jax.ShapeDtypeStruct((h, seq, n_col), F32),
        compiler_params=_cparams(("parallel", "parallel")),
        name="cmp_bias_table",
    )(base)


def _ada_kernel(c_ref, w_ref, b_ref, o_ref):
    o_ref[0] = jnp.dot(c_ref[...], w_ref[0].astype(BF16), preferred_element_type=F32) + b_ref[0]


def _ada_mod(c, w_ada, b_ada):
    depth, d, d3 = w_ada.shape
    cp = jnp.zeros((8, d), BF16).at[:c.shape[0]].set(c.astype(BF16))
    nj = d3 // d
    return pl.pallas_call(
        _ada_kernel,
        grid=(depth, nj),
        in_specs=[pl.BlockSpec((8, d), lambda l, j: (0, 0)),
                  pl.BlockSpec((1, d, d), lambda l, j: (l, 0, j)),
                  pl.BlockSpec((1, 1, d), lambda l, j: (l, 0, j))],
        out_specs=pl.BlockSpec((1, 8, d), lambda l, j: (l, 0, j)),
        out_shape=jax.ShapeDtypeStruct((depth, 8, d3), F32),
        compiler_params=_cparams(("parallel", "parallel")),
        name="ada_mod",
    )(cp, w_ada, b_ada.reshape(depth, 1, d3))


def _proj_kernel(x_ref, sh_ref, sc_ref, w_ref, *refs, dils):
    n_lay = len(dils)
    qa, ka, va = refs[:n_lay], refs[n_lay:2 * n_lay], refs[2 * n_lay:3 * n_lay]
    za, qb, kc, vc, ks, vs, kw, vw, zb, gb, h_ref, xs = refs[3 * n_lay:]
    tm = h_ref.shape[0]
    h_ref[...] = (x_ref[0] * (1.0 + sc_ref[0]) + sh_ref[0]).astype(BF16)

    def mm(c0, n):
        return jnp.dot(h_ref[...], w_ref[:, c0:c0 + n], preferred_element_type=F32)

    def emit_regrouped(a, outs):
        outs[0][0] = a.astype(BF16)
        n_slab = a.shape[1] // LANES
        for c in range(n_slab):
            xs[0, c] = a[:, c * LANES:(c + 1) * LANES]
        d_prev = 1
        for lvl, (ref, dil) in enumerate(zip(outs[1:], dils[1:])):
            ratio, len_prev, len_new = dil // d_prev, tm // d_prev, tm // dil
            src, dst = xs.at[lvl % 2], xs.at[(lvl + 1) % 2]
            keep = lvl + 2 < len(dils)
            for rp in range(d_prev):
                for rs in range(ratio):
                    r = rp + d_prev * rs
                    for c in range(n_slab):
                        rows = src[c, pl.ds(rp * len_prev + rs, len_new, stride=ratio), :]
                        ref[0, r, :, c * LANES:(c + 1) * LANES] = rows.astype(BF16)
                        if keep:
                            dst[c, r * len_new:(r + 1) * len_new, :] = rows
            d_prev = dil

    qs = HEAD_DIM ** -0.5 * LOG2E
    emit_regrouped(mm(_C_QA, D_A) * qs, qa)
    emit_regrouped(mm(_C_KA, D_A), ka)
    emit_regrouped(mm(_C_VA, D_A), va)
    za[0] = mm(_C_ZA, D_A).astype(BF16)
    qb[0] = (mm(_C_QB, D_B) * qs).astype(BF16)
    lo = _lo_mask()
    narrow = mm(_C_NARROW, 7 * LANES)
    piece = lambda i: narrow[:, i * LANES:(i + 1) * LANES]
    for ref, i in ((kc, 0), (vc, 1)):
        xs[0, 0] = piece(i)
        for t in range(0, CMP_STRIDE, 2):
            x0 = xs[0, 0, pl.ds(t, tm // CMP_STRIDE, stride=CMP_STRIDE), :]
            x1 = xs[0, 0, pl.ds(t + 1, tm // CMP_STRIDE, stride=CMP_STRIDE), :]
            cols = slice(t * HEAD_DIM, (t + 2) * HEAD_DIM)
            ref[0, 0, :, cols] = jnp.where(lo, x0, pltpu.roll(x1, HEAD_DIM, 1)).astype(BF16)
            ref[0, 1, :, cols] = jnp.where(lo, pltpu.roll(x0, HEAD_DIM, 1), x1).astype(BF16)
    for ref, i in ((ks, 2), (vs, 3), (kw, 4), (vw, 5)):
        a = piece(i)
        r = pltpu.roll(a, HEAD_DIM, 1)
        ref[0, :, :LANES] = jnp.where(lo, a, r).astype(BF16)
        ref[0, :, LANES:] = jnp.where(lo, r, a).astype(BF16)
    gb[0] = piece(6)
    zb[0] = mm(_C_ZB, D_B).astype(BF16)


def _project(x, shift, scl, w, layer, tm=512):
    b, s, d = x.shape
    dils = tuple(dl for _, dl in DILATED)
    bs = lambda n: pl.BlockSpec((1, tm, n), lambda i, j: (i, j, 0))
    gs = pl.BlockSpec((1, G_B, tm // CMP_STRIDE, CMP_STRIDE * HEAD_DIM), lambda i, j: (i, 0, j, 0))
    sd = lambda n, dt=BF16: jax.ShapeDtypeStruct((b, s, n), dt)
    gd = jax.ShapeDtypeStruct((b, G_B, s // CMP_STRIDE, CMP_STRIDE * HEAD_DIM), BF16)
    mod = pl.BlockSpec((1, 1, d), lambda i, j: (i, 0, 0))
    lay_specs = [bs(D_A)] + [pl.BlockSpec((1, dl, tm // dl, D_A), lambda i, j: (i, 0, j, 0)) for dl in dils[1:]]
    lay_shapes = [sd(D_A)] + [jax.ShapeDtypeStruct((b, dl, s // dl, D_A), BF16) for dl in dils[1:]]
    outs = pl.pallas_call(
        functools.partial(_proj_kernel, dils=dils),
        grid=(b, s // tm),
        in_specs=[pl.BlockSpec((1, tm, d), lambda i, j: (i, j, 0)), mod, mod,
                  pl.BlockSpec((None, d, _C_END), lambda i, j: (layer, 0, 0))],
        out_specs=lay_specs * 3 + [bs(D_A), bs(D_B), gs, gs,
                                   bs(2 * LANES), bs(2 * LANES), bs(2 * LANES), bs(2 * LANES), bs(D_B), bs(LANES)],
        out_shape=lay_shapes * 3 + [sd(D_A), sd(D_B), gd, gd,
                                    sd(2 * LANES), sd(2 * LANES), sd(2 * LANES), sd(2 * LANES), sd(D_B), sd(LANES, F32)],
        scratch_shapes=[pltpu.VMEM((tm, d), BF16), pltpu.VMEM((2, D_A // LANES, tm, LANES), F32)],
        compiler_params=_cparams(("parallel", "parallel")),
        name="in_proj",
    )(x, shift, scl, w)
    n = len(dils)
    return (outs[:n], outs[n:2 * n], outs[2 * n:3 * n], *outs[3 * n:])


def _dilated_kernel(*refs, seq, unroll):
    n_pat = len(DILATED)
    qkv = refs[:3 * n_pat]
    t_refs = refs[3 * n_pat:4 * n_pat]
    o_ref = refs[4 * n_pat]
    qlo, qhi, kp, vp, o_r, m_r, l_r, o_t, m_t, l_t, m_acc, n_acc, d_acc = refs[4 * n_pat + 1:]
    lo = _lo_mask()
    nblk = seq // BLK
    chunk = 512
    kp[:BLK, :] = jnp.zeros((BLK, LANES), BF16)
    vp[:BLK, :] = jnp.zeros((BLK, LANES), BF16)

    for pi, ((_, dil), t_ref) in enumerate(zip(DILATED, t_refs)):
        q_ref, k_ref, v_ref = qkv[pi], qkv[n_pat + pi], qkv[2 * n_pat + pi]
        rl = seq // dil
        nb = rl // BLK
        for r in range(dil):
            idx = (0,) if dil == 1 else (0, r)
            q = q_ref[idx]
            zero = jnp.zeros_like(q)
            qlo[r * rl:(r + 1) * rl, :] = jnp.where(lo, q, zero)
            qhi[r * rl:(r + 1) * rl, :] = jnp.where(lo, zero, q)
            kp[BLK + r * rl:BLK + (r + 1) * rl, :] = k_ref[idx]
            vp[BLK + r * rl:BLK + (r + 1) * rl, :] = v_ref[idx]
        o_dst, m_dst, l_dst = (n_acc, m_acc, d_acc) if dil == 1 else (o_r, m_r, l_r)

        def blocks(it, carry, t_ref=t_ref, nb=nb, o_dst=o_dst, m_dst=m_dst, l_dst=l_dst):
            for u in range(unroll):
                g = it * unroll + u
                first = jnp.where((g & (nb - 1)) == 0, 1, 0)
                q0 = pl.multiple_of(g * BLK, BLK)
                qs = jnp.concatenate([qlo[pl.ds(q0, BLK), :], qhi[pl.ds(q0, BLK), :]], axis=0)
                s = _dot_nt(qs, kp[pl.ds(q0, 2 * BLK), :]) + t_ref[0, first]
                m = jnp.max(s, axis=-1, keepdims=True)
                p = jnp.exp2(s - m)
                l = jnp.sum(p, axis=-1, keepdims=True)
                pv = jnp.dot(p.astype(BF16), vp[pl.ds(q0, 2 * BLK), :], preferred_element_type=F32)
                o_dst[pl.ds(q0, BLK), :] = jnp.where(lo, pv[:BLK], pv[BLK:])
                m_dst[pl.ds(q0, BLK), :] = jnp.where(lo, m[:BLK], m[BLK:])
                l_dst[pl.ds(q0, BLK), :] = jnp.where(lo, l[:BLK], l[BLK:])
            return carry

        lax.fori_loop(0, nblk // unroll, blocks, 0)
        if dil == 1:
            continue

        for r in range(dil):
            for src, dst in ((o_r, o_t), (m_r, m_t), (l_r, l_t)):
                dst[pl.ds(r, rl, stride=dil), :] = src[r * rl:(r + 1) * rl, :]

        def fold(c, carry, last=(pi == n_pat - 1)):
            rows = pl.ds(pl.multiple_of(c * chunk, chunk), chunk)
            m_old, m_new = m_acc[rows, :], m_t[rows, :]
            mx = jnp.maximum(m_old, m_new)
            a, b_ = jnp.exp2(m_old - mx), jnp.exp2(m_new - mx)
            n_new = a * n_acc[rows, :] + b_ * o_t[rows, :]
            d_new = a * d_acc[rows, :] + b_ * l_t[rows, :]
            if last:
                o_ref[0, rows, :] = (n_new * (1.0 / d_new)).astype(o_ref.dtype)
            else:
                m_acc[rows, :] = mx
                n_acc[rows, :] = n_new
                d_acc[rows, :] = d_new
            return carry

        lax.fori_loop(0, seq // chunk, fold, 0)


def _dilated_attention(qa, ka, va, tables, unroll=32):
    assert DILATED[0][1] == 1 and all(dl > 1 for _, dl in DILATED[1:])
    b, s, c = qa[0].shape
    xs = pl.BlockSpec((1, s, LANES), lambda i, j: (i, 0, j))
    lay = [xs] + [pl.BlockSpec((1, dl, s // dl, LANES), lambda i, j: (i, 0, 0, j)) for _, dl in DILATED[1:]]
    ts = pl.BlockSpec((1, 2, 2 * BLK, 2 * BLK), lambda i, j: (j, 0, 0, 0))
    big = lambda rows, dt: pltpu.VMEM((rows, LANES), dt)
    return pl.pallas_call(
        functools.partial(_dilated_kernel, seq=s, unroll=unroll),
        grid=(b, c // LANES),
        in_specs=lay * 3 + [ts] * len(DILATED),
        out_specs=xs,
        out_shape=jax.ShapeDtypeStruct((b, s, c), BF16),
        scratch_shapes=[big(s, BF16)] * 2 + [big(s + BLK, BF16)] * 2 + [big(s, F32)] * 9,
        compiler_params=_cparams(("parallel", "parallel")),
        name="dilated_attn",
    )(*qa, *ka, *va, *tables)


def _win_kernel(q_ref, k_ref, v_ref, t_ref, o_ref, qst, kaug, vp, *, seq, n_prev, unroll):
    lo = _lo_mask()
    lane = lax.broadcasted_iota(jnp.int32, (1, LANES), 1)
    flag = lane == HEAD_DIM
    pad = n_prev * BLK
    span = (n_prev + 1) * BLK
    negflag = jnp.where(flag, NEG, 0.0)
    for pr in range(R_B // 2):
        q = q_ref[0, :, pr * LANES:(pr + 1) * LANES].astype(F32)
        qst[2 * pr] = jnp.where(lo, q, negflag).astype(BF16)
        qst[2 * pr + 1] = jnp.where(lo, pltpu.roll(q, HEAD_DIM, 1), negflag).astype(BF16)
    k = k_ref[0]
    kaug[:pad, :] = jnp.broadcast_to(jnp.where(flag, 1.0, 0.0), (pad, LANES)).astype(BF16)
    kaug[pad:, :] = jnp.where(lo, k, jnp.zeros_like(k))
    vp[:pad, :] = jnp.zeros((pad, LANES), BF16)
    vp[pad:, :] = jnp.where(lo, v_ref[0], jnp.ones((), BF16))

    def blocks(it, carry):
        for u in range(unroll):
            n = it * unroll + u
            r0 = pl.multiple_of(n * BLK, BLK)
            qs = jnp.concatenate([qst[h, pl.ds(r0, BLK), :] for h in range(R_B)], axis=0)
            s = _dot_nt(qs, kaug[pl.ds(r0, span), :]) + t_ref[0]
            m = jnp.max(s, axis=-1, keepdims=True)
            p = jnp.exp2(s - m)
            pv = jnp.dot(p.astype(BF16), vp[pl.ds(r0, span), :], preferred_element_type=F32)
            for pr in range(R_B // 2):
                even, odd = pv[2 * pr * BLK:(2 * pr + 1) * BLK], pv[(2 * pr + 1) * BLK:(2 * pr + 2) * BLK]
                o = jnp.where(lo, even * (1.0 / pltpu.roll(even, HEAD_DIM, 1)), pltpu.roll(odd, HEAD_DIM, 1) * (1.0 / odd))
                o_ref[0, pl.ds(r0, BLK), pr * LANES:(pr + 1) * LANES] = o.astype(o_ref.dtype)
        return carry

    lax.fori_loop(0, seq // BLK // unroll, blocks, 0)


def _win_attention(qb, kw_dup, vw_dup, table, n_prev, unroll=8):
    b, s, _ = qb.shape
    gw = R_B * HEAD_DIM
    w = table.shape[-1]
    return pl.pallas_call(
        functools.partial(_win_kernel, seq=s, n_prev=n_prev, unroll=unroll),
        grid=(b, G_B),
        in_specs=[pl.BlockSpec((1, s, gw), lambda i, g: (i, 0, g)),
                  pl.BlockSpec((1, s, LANES), lambda i, g: (i, 0, g)),
                  pl.BlockSpec((1, s, LANES), lambda i, g: (i, 0, g)),
                  pl.BlockSpec((1, R_B * BLK, w), lambda i, g: (g, 0, 0))],
        out_specs=pl.BlockSpec((1, s, gw), lambda i, g: (i, 0, g)),
        out_shape=jax.ShapeDtypeStruct((b, s, D_B), BF16),
        scratch_shapes=[pltpu.VMEM((R_B, s, LANES), BF16)] + [pltpu.VMEM((s + n_prev * BLK, LANES), BF16)] * 2,
        compiler_params=_cparams(("parallel", "parallel")),
        name="nsa_window",
    )(qb, kw_dup, vw_dup, table)


def _compress_kernel(kin, vin, w1, w2, pe, kout, vout):
    n_chunk = kin.shape[2]
    half = w1.shape[1] // 2
    rows = lax.broadcasted_iota(jnp.int32, (n_chunk, 1), 0)
    for t, (xin, out) in enumerate(((kin, kout), (vin, vout))):
        x = xin[0, 0]
        a = jnp.dot(x, w1[t, :half], preferred_element_type=F32)
        b = jnp.dot(x, w1[t, half:], preferred_element_type=F32)
        c = jnp.dot(pe[t], w1[t], preferred_element_type=F32)[0:1]
        hid = a + pltpu.roll(b, n_chunk - 1, 0) + c
        act = hid * jax.nn.sigmoid(hid)
        o = jnp.dot(act.astype(BF16), w2[t], preferred_element_type=F32)
        out[0, 0] = jnp.where(rows < n_chunk - 1, o, 0.0).astype(BF16)


def _compress(kc_in, vc_in, w1, w2d, pe, layer):
    b, g, n_chunk, f = kc_in.shape
    xs = pl.BlockSpec((1, 1, n_chunk, f), lambda i, j: (i, j, 0, 0))
    os_ = pl.BlockSpec((1, 1, n_chunk, LANES), lambda i, j: (i, j, 0, 0))
    full = lambda a: pl.BlockSpec((None,) + a.shape[1:], lambda i, j: (layer,) + (0,) * (a.ndim - 1))
    od = jax.ShapeDtypeStruct((b, g, n_chunk, LANES), BF16)
    return pl.pallas_call(
        _compress_kernel,
        grid=(b, g),
        in_specs=[xs, xs, full(w1), full(w2d), full(pe)],
        out_specs=[os_, os_],
        out_shape=[od, od],
        compiler_params=_cparams(("parallel", "parallel")),
        name="nsa_compress",
    )(kc_in, vc_in, w1, w2d, pe)


def _cmp_kernel(q_ref, kc_ref, vc_ref, t_ref, ov_ref, o_ref, sb_ref, *, tq):
    i = pl.program_id(2)
    lo = _lo_mask()
    n_col, n_slc = kc_ref.shape[2], ov_ref.shape[0]

    def attend_and_rank(ncol, nsel):
        kcd = kc_ref[0, 0, :ncol, :]
        vcd = vc_ref[0, 0, :ncol, :]
        zero = jnp.zeros_like(vcd)
        vlo = jnp.where(lo, vcd, zero)
        vhi = jnp.where(lo, zero, vcd)
        psum = None
        for pr in range(R_B // 2):
            q = q_ref[0, :, pr * LANES:(pr + 1) * LANES]
            qz = jnp.zeros_like(q)
            acc = None
            for hh, (qm, vm) in enumerate(((jnp.where(lo, q, qz), vlo), (jnp.where(lo, qz, q), vhi))):
                s = _dot_nt(qm, kcd) + t_ref[2 * pr + hh, :, :ncol]
                m = jnp.max(s, axis=-1, keepdims=True)
                e = jnp.exp2(s - m)
                l = jnp.sum(e, axis=-1, keepdims=True)
                p = e * jnp.where(m > 0.5 * NEG, 1.0 / l, 0.0)
                psum = p if psum is None else psum + p
                pv = jnp.dot(p.astype(BF16), vm, preferred_element_type=F32)
                acc = pv if acc is None else acc + pv
            o_ref[0, :, pr * LANES:(pr + 1) * LANES] = acc.astype(o_ref.dtype)

        p_hi = psum.astype(BF16)
        p_lo = (psum - p_hi.astype(F32)).astype(BF16)
        ov = ov_ref[:nsel, :ncol]
        score = (_dot_nt(ov, p_hi) + _dot_nt(ov, p_lo))
        blk = lax.broadcasted_iota(jnp.int32, (nsel, 1), 0)
        pos = i * tq + lax.broadcasted_iota(jnp.int32, (1, tq), 1)
        cur = pos >> int(math.log2(SLC_BLOCK))
        forced = (blk == 0) | ((cur - blk >= 0) & (cur - blk < SLC_LOCAL))
        score = jnp.where(forced, 1e9, jnp.where(blk > cur, -1e9, score))
        sub = lax.broadcasted_iota(jnp.int32, (8, 1), 0)
        sel_rows = []
        for g8 in range(nsel // 8):
            sg = score[g8 * 8:(g8 + 1) * 8]
            rank = jnp.zeros(sg.shape, F32)
            for mp in range(nsel):
                row = score[mp:mp + 1]
                if mp < g8 * 8:
                    ahead = jnp.where(row >= sg, 1.0, 0.0)
                elif mp >= (g8 + 1) * 8:
                    ahead = jnp.where(row > sg, 1.0, 0.0)
                else:
                    ahead = jnp.where(sub > mp - g8 * 8, jnp.where(row >= sg, 1.0, 0.0), jnp.where(row > sg, 1.0, 0.0))
                rank = rank + ahead
            sel_rows.append(jnp.where(rank < float(min(SLC_TOPK, n_slc)), 0.0, NEG))
        if nsel < n_slc:
            sel_rows.append(jnp.full((n_slc - nsel, tq), NEG, F32))
        pad = LANES // n_slc
        selb = jnp.concatenate(sel_rows * pad, axis=0)
        sb_ref[0, 0] = selb.T.astype(BF16)

    n_tile = pl.num_programs(2)
    for k in range(4):
        ncol = max(LANES, -(-(n_col * (k + 1) // 4) // LANES) * LANES)
        in_quarter = (i * 4 >= k * n_tile) & (i * 4 < (k + 1) * n_tile)
        pl.when(in_quarter)(functools.partial(attend_and_rank, ncol, n_slc * (k + 1) // 4))


def _cmp_attention(qb, kcd, vcd, table, ov_t, tq=512):
    b, s, _ = qb.shape
    n_col = kcd.shape[2]
    gw = R_B * HEAD_DIM
    return pl.pallas_call(
        functools.partial(_cmp_kernel, tq=tq),
        grid=(b, G_B, s // tq),
        in_specs=[pl.BlockSpec((1, tq, gw), lambda i, g, j: (i, j, g)),
                  pl.BlockSpec((1, 1, n_col, LANES), lambda i, g, j: (i, g, 0, 0)),
                  pl.BlockSpec((1, 1, n_col, LANES), lambda i, g, j: (i, g, 0, 0)),
                  pl.BlockSpec((R_B, tq, n_col), lambda i, g, j: (g, j, 0)),
                  pl.BlockSpec(ov_t.shape, lambda i, g, j: (0, 0))],
        out_specs=[pl.BlockSpec((1, tq, gw), lambda i, g, j: (i, j, g)),
                   pl.BlockSpec((1, 1, tq, LANES), lambda i, g, j: (i, g, j, 0))],
        out_shape=[jax.ShapeDtypeStruct((b, s, D_B), BF16),
                   jax.ShapeDtypeStruct((b, G_B, s, LANES), BF16)],
        compiler_params=_cparams(("parallel", "parallel", "parallel")),
        name="nsa_cmp_topk",
    )(qb, kcd, vcd, table, ov_t)


def _sel_kernel(q_ref, sb_ref, k_ref, v_ref, t_ref, o_ref, kaug, vp, m_ref, acc_ref, *, nblk):
    lo = _lo_mask()
    s_len = nblk * BLK
    n_diag = t_ref.shape[1] - 1
    h0 = pl.program_id(1) * R_B
    rowblk = lax.broadcasted_iota(jnp.int32, (s_len, LANES), 0) >> int(math.log2(SLC_BLOCK))
    lane = lax.broadcasted_iota(jnp.int32, (s_len, LANES), 1)
    onehot = jnp.where(lane - HEAD_DIM == rowblk, 1.0, 0.0).astype(BF16)
    kaug[...] = jnp.where(lo, k_ref[0], onehot)
    vp[...] = jnp.where(lo, v_ref[0], jnp.ones((), BF16))
    m_ref[...] = jnp.full(m_ref.shape, NEG, F32)
    acc_ref[...] = jnp.zeros(acc_ref.shape, F32)
    kb = 4
    qb = 2
    sub = kb // qb
    kt_rows, qt_rows = kb * BLK, qb * BLK

    def ktile(t, carry):
        k0 = pl.multiple_of(t * kt_rows, kt_rows)
        kt = kaug[pl.ds(k0, kt_rows), :]
        vt = vp[pl.ds(k0, kt_rows), :]

        def qstep(mp, c):
            scores = []
            for u in range(sub):
                r0 = pl.multiple_of((sub * mp + u) * qt_rows, qt_rows)
                sbf = sb_ref[0, 0, pl.ds(r0, qt_rows), :].astype(F32)
                rows = []
                for pr in range(R_B // 2):
                    q = q_ref[0, pl.ds(r0, qt_rows), pr * LANES:(pr + 1) * LANES].astype(F32)
                    rows.append(jnp.where(lo, q, sbf))
                    rows.append(jnp.where(lo, pltpu.roll(q, HEAD_DIM, 1), sbf))
                qst = jnp.concatenate(rows, axis=0).astype(BF16)
                bias = jnp.concatenate([
                    jnp.concatenate([t_ref[h0 + h, jnp.clip(qb * (sub * mp + u) + a - kb * t - c + 1, 0, n_diag)]
                                     for c in range(kb)], axis=1)
                    for h in range(R_B) for a in range(qb)], axis=0)
                scores.append(_dot_nt(qst, kt) + bias)
            for u in range(sub):
                r0 = pl.multiple_of((sub * mp + u) * qt_rows, qt_rows)
                s = scores[u]
                ps, alphas = [], []
                for h in range(R_B):
                    for a in range(qb):
                        lo_r = h * qt_rows + a * BLK
                        st = pl.ds(r0 + a * BLK, BLK)
                        sh = s[lo_r:lo_r + BLK]
                        m_prev = m_ref[h, st, :]
                        m_new = jnp.maximum(m_prev, jnp.max(sh, axis=-1, keepdims=True))
                        alpha = jnp.exp2(m_prev - m_new)
                        p = jnp.exp2(sh - jnp.concatenate([m_new] * kb, axis=1))
                        m_ref[h, st, :] = m_new
                        ps.append(p.astype(BF16))
                        alphas.append(alpha)
                pv = jnp.dot(jnp.concatenate(ps, axis=0), vt, preferred_element_type=F32)
                for h in range(R_B):
                    for a in range(qb):
                        lo_r = h * qt_rows + a * BLK
                        st = pl.ds(r0 + a * BLK, BLK)
                        acc_ref[h, st, :] = alphas[qb * h + a] * acc_ref[h, st, :] + pv[lo_r:lo_r + BLK]
            return c

        lax.fori_loop(t, nblk // kb, qstep, 0)
        return carry

    lax.fori_loop(0, nblk // kb, ktile, 0)

    def finish(n, carry):
        st = pl.ds(pl.multiple_of(n * BLK, BLK), BLK)
        for pr in range(R_B // 2):
            even, odd = acc_ref[2 * pr, st, :], acc_ref[2 * pr + 1, st, :]
            even = even * (1.0 / pltpu.roll(even, HEAD_DIM, 1))
            odd = pltpu.roll(odd, HEAD_DIM, 1) * (1.0 / odd)
            o = jnp.where(lo, even, odd)
            o_ref[0, st, pr * LANES:(pr + 1) * LANES] = o.astype(o_ref.dtype)
        return carry

    lax.fori_loop(0, nblk, finish, 0)


def _sel_attention(qb, selb, ks_dup, vs_dup, table):
    b, s, _ = qb.shape
    gw = R_B * HEAD_DIM
    return pl.pallas_call(
        functools.partial(_sel_kernel, nblk=s // BLK),
        grid=(b, G_B),
        in_specs=[pl.BlockSpec((1, s, gw), lambda i, g: (i, 0, g)),
                  pl.BlockSpec((1, 1, s, LANES), lambda i, g: (i, g, 0, 0)),
                  pl.BlockSpec((1, s, LANES), lambda i, g: (i, 0, g)),
                  pl.BlockSpec((1, s, LANES), lambda i, g: (i, 0, g)),
                  pl.BlockSpec(table.shape, lambda i, g: (0, 0, 0, 0), pipeline_mode=pl.Buffered(1))],
        out_specs=pl.BlockSpec((1, s, gw), lambda i, g: (i, 0, g)),
        out_shape=jax.ShapeDtypeStruct((b, s, D_B), BF16),
        scratch_shapes=[pltpu.VMEM((s, LANES), BF16)] * 2 + [pltpu.VMEM((R_B, s, LANES), F32)] * 2,
        compiler_params=_cparams(("parallel", "parallel")),
        name="nsa_selected",
    )(qb, selb, ks_dup, vs_dup, table)


def _combine_kernel(x_ref, oa, za, oc, osl, ow, gb, zb, wo, ex, gate, lng, lnb, out_ref, *, alpha):
    f = lambda r: r[0].astype(F32)
    z = f(za)
    mix_a = f(oa) * (z * jax.nn.sigmoid(z))
    g = jax.nn.sigmoid(gb[0])
    g_hi = g.astype(BF16)
    g_lo = (g - g_hi.astype(F32)).astype(BF16)
    gx = (jnp.dot(g_hi, ex[...], preferred_element_type=F32)
          + jnp.dot(g_lo, ex[...], preferred_element_type=F32))
    ob = gx[:, :D_B] * f(oc) + gx[:, D_B:2 * D_B] * f(osl) + gx[:, 2 * D_B:] * f(ow)
    z = f(zb)
    mix_b = ob * (z * jax.nn.sigmoid(z))
    y = (jnp.dot(mix_a.astype(BF16), wo[:D_A], preferred_element_type=F32)
         + jnp.dot(mix_b.astype(BF16), wo[D_A:], preferred_element_type=F32))
    r = alpha * x_ref[0] + (1.0 + gate[0]) * y
    mu = jnp.mean(r, axis=-1, keepdims=True)
    rc = r - mu
    var = jnp.mean(rc * rc, axis=-1, keepdims=True)
    out_ref[0] = rc * lax.rsqrt(var + LN_EPS) * lng[...] + lnb[...]


def _combine(x, o_a, za, o_cmp, o_slc, o_win, gb, zb, w_out, expand, gate, ln_g, ln_b, alpha, layer, tm=512):
    b, s, d = x.shape
    bs = lambda n: pl.BlockSpec((1, tm, n), lambda i, j: (i, j, 0))
    full = lambda a: pl.BlockSpec(a.shape, lambda i, j: (0,) * a.ndim)
    return pl.pallas_call(
        functools.partial(_combine_kernel, alpha=alpha),
        grid=(b, s // tm),
        in_specs=[bs(d)] + [bs(D_A)] * 2 + [bs(D_B)] * 3 + [bs(LANES), bs(D_B),
                  pl.BlockSpec((None,) + w_out.shape[1:], lambda i, j: (layer, 0, 0)), full(expand),
                  pl.BlockSpec((1, 1, d), lambda i, j: (i, 0, 0)), full(ln_g), full(ln_b)],
        out_specs=bs(d),
        out_shape=jax.ShapeDtypeStruct((b, s, d), F32),
        compiler_params=_cparams(("parallel", "parallel")),
        name="merge_out_proj_ln",
    )(x, o_a, za, o_cmp, o_slc, o_win, gb, zb, w_out, expand, gate, ln_g, ln_b)


def _pack_w_in(w_in):
    n_gb = 3 * H_B
    main = w_in[..., :_N_MAIN]
    gbw = w_in[..., _N_MAIN:_N_MAIN + n_gb]
    zbw = w_in[..., _N_MAIN + n_gb:]
    pad = jnp.zeros(w_in.shape[:-1] + (LANES - n_gb,), w_in.dtype)
    return jnp.concatenate([main, gbw, pad, zbw], axis=-1).astype(BF16)


def _gate_expand():
    e = np.zeros((LANES, 3 * D_B), np.float32)
    for i in range(3):
        for h in range(H_B):
            e[i * H_B + h, i * D_B + h * HEAD_DIM:i * D_B + (h + 1) * HEAD_DIM] = 1.0
    return jnp.asarray(e, BF16)


def _overlap_t(n_col, n_slc):
    cs = np.arange(n_col)[None, :] * CMP_STRIDE
    ss = np.arange(n_slc)[:, None] * SLC_BLOCK
    return jnp.asarray(((cs < ss + SLC_BLOCK) & (cs + CMP_LEN > ss)).astype(np.float32), BF16)


def _layer(x, shift, scl, gate, w_in_p, w_out_b, pe_b, w1_b, w2d_b, ln_g, ln_b, tables, consts, alpha, layer):
    b, s, d = x.shape
    (qa, ka, va, za, qb, kc_in, vc_in, ks, vs, kw, vw, zb, gb) = _project(x, shift, scl, w_in_p, layer)

    o_a = _dilated_attention(qa, ka, va, tables["dilated"])

    kcd, vcd = _compress(kc_in, vc_in, w1_b, w2d_b, pe_b, layer)
    o_cmp, selb = _cmp_attention(qb, kcd, vcd, tables["cmp"], consts["overlap_t"])
    o_slc = _sel_attention(qb, selb, ks, vs, tables["sel"])
    o_win = _win_attention(qb, kw, vw, tables["win"], n_prev=-(-(WIN - 1) // BLK))

    return _combine(x, o_a, za, o_cmp, o_slc, o_win, gb, zb, w_out_b, consts["expand"], gate,
                    ln_g, ln_b, alpha, layer)


def kernel(x, c, w_in, w_out, cmp_pe, cmp_w1, cmp_w2, w_ada, b_ada, ln_g, ln_b, rel_bias):
    b, s, d = x.shape
    depth = w_in.shape[0]
    alpha = (2 * depth) ** 0.25
    n_pair = D_A // LANES

    bb = rel_bias.astype(F32) * LOG2E
    def dilated_table(window, dil):
        t = _band_table(bb[:, :H_A], 1, window // dil, dil).reshape(n_pair, 2 * BLK, 2 * BLK)
        first = t.at[..., :BLK].set(NEG)
        return jnp.stack([t, first], axis=1)

    n_prev_win = -(-(WIN - 1) // BLK)
    tables = {
        "dilated": [dilated_table(w, dl) for w, dl in DILATED],
        "win": _band_table(bb[:, H_A:], n_prev_win, WIN - 1, 1).reshape(G_B, R_B * BLK, (n_prev_win + 1) * BLK),
        "cmp": _cmp_table(bb[:, H_A:], s),
        "sel": _diag_table(bb[:, H_A:], 13),
    }
    consts = {"expand": _gate_expand(), "overlap_t": _overlap_t(s // CMP_STRIDE, s // SLC_BLOCK)}

    mod = _ada_mod(c, w_ada, b_ada)
    w_in_p = _pack_w_in(w_in)
    w_out_b = w_out.astype(BF16)
    w1_b = cmp_w1.astype(BF16)
    w2d_b = jnp.concatenate([cmp_w2, cmp_w2], axis=-1).astype(BF16)
    pe_b = jnp.broadcast_to(cmp_pe.reshape(depth, 2, 1, CMP_LEN * HEAD_DIM), (depth, 2, 8, CMP_LEN * HEAD_DIM)).astype(BF16)

    for l in range(depth):
        m = mod[l, :b]
        shift, scl, gate = (m[:, i * d:(i + 1) * d].reshape(b, 1, d) for i in range(3))
        x = _layer(x, shift, scl, gate, w_in_p, w_out_b, pe_b, w1_b, w2d_b,
                   ln_g[l].reshape(1, d), ln_b[l].reshape(1, d), tables, consts, alpha, l)
    return x
```

```python
import functools
import math

import numpy as np
import jax
import jax.numpy as jnp
from jax import lax
from jax.experimental import pallas as pl
from jax.experimental.pallas import tpu as pltpu

F32 = jnp.float32
BF16 = jnp.bfloat16

D_MODEL = 1024
HEAD_DIM = 64
H_A = 8
H_B = 8
G_B = 2
R_B = H_B // G_B
D_A = H_A * HEAD_DIM
D_B = H_B * HEAD_DIM
DILATED = ((128, 1), (512, 4), (2048, 16))
CMP_LEN = 32
CMP_STRIDE = 16
CMP_HIDDEN = 256
SLC_BLOCK = 64
SLC_TOPK = 16
SLC_LOCAL = 2
WIN = 512
N_BUCKETS = 32
MAX_DIST = 2048
LN_EPS = 1e-5
NEG = -1e30
LOG2E = 1.4426950408889634

LANES = 128
BLK = 128
VMEM_BYTES_V7X = 64 * 1024 * 1024
VMEM_LIMIT = VMEM_BYTES_V7X - 8 * 1024 * 1024

_C_QA, _C_KA, _C_VA, _C_ZA, _C_QB = 0, 512, 1024, 1536, 2048
_C_NARROW = 2560
_C_ZB = 3328 + 3 * H_B


def _cparams(sem):
    return pltpu.CompilerParams(dimension_semantics=sem, vmem_limit_bytes=VMEM_LIMIT)


def _lo_mask():
    return lax.broadcasted_iota(jnp.int32, (1, LANES), 1) < HEAD_DIM


def _dot_nt(a, b):
    return lax.dot_general(a, b, (((1,), (1,)), ((), ())), preferred_element_type=F32)


def _bucket_np(dist):
    d = np.maximum(dist, 0)
    max_exact = N_BUCKETS // 2
    large = max_exact + (np.log(np.maximum(d, 1).astype(np.float32) / np.float32(max_exact))
                         / np.float32(math.log(MAX_DIST / max_exact))
                         * np.float32(N_BUCKETS - max_exact)).astype(np.int32)
    return np.where(d < max_exact, d, np.minimum(large, N_BUCKETS - 1)).astype(np.int32)


def _masked_bias(bias_by_bucket, dist, valid):
    onehot = np.eye(N_BUCKETS, dtype=np.float32)[_bucket_np(dist)]
    t = jnp.einsum("...k,kh->h...", onehot, bias_by_bucket, precision=lax.Precision.HIGHEST)
    return jnp.where(jnp.asarray(valid)[None], t, NEG)


def _toeplitz_kernel(v_ref, o_ref):
    cols = o_ref.shape[-1]
    for e in range(o_ref.shape[1]):
        x = jnp.broadcast_to(v_ref[0, e:e + 1, :], (BLK, v_ref.shape[-1]))
        o_ref[0, e] = pltpu.roll(x, 0, 1, stride=1, stride_axis=0)[:, :cols]


def _toeplitz(h, cols):
    n_h, n_e, n = h.shape
    width = 1 << (n - 1).bit_length()
    vec = jnp.concatenate([h[..., BLK - 1:], jnp.zeros((n_h, n_e, width - n), F32), h[..., :BLK - 1]], axis=-1)
    return pl.pallas_call(
        _toeplitz_kernel,
        grid=(n_h,),
        in_specs=[pl.BlockSpec((1, n_e, width), lambda a: (a, 0, 0))],
        out_specs=pl.BlockSpec((1, n_e, BLK, cols), lambda a: (a, 0, 0, 0)),
        out_shape=jax.ShapeDtypeStruct((n_h, n_e, BLK, cols), F32),
        compiler_params=_cparams(("parallel",)),
        name="toeplitz_table",
    )(vec)


def _band_table(bias_by_bucket, n_prev, window, scale):
    w = (n_prev + 1) * BLK
    dist = n_prev * BLK - (np.arange(BLK - 1 + w) - (BLK - 1))
    h = _masked_bias(bias_by_bucket, dist * scale, (dist >= 0) & (dist <= window))
    return _toeplitz(h[:, None], w)[:, 0]


def _diag_table(bias_by_bucket, n_diag):
    delta = np.arange(-1, n_diag + 1)[:, None]
    dist = delta * BLK - (np.arange(2 * BLK - 1)[None, :] - (BLK - 1))
    return _toeplitz(_masked_bias(bias_by_bucket, dist, dist >= 0), BLK)


def _cmp_table_kernel(base_ref, o_ref, *, rows):
    i = pl.program_id(1)
    n_col = o_ref.shape[-1]
    width = base_ref.shape[-1]
    base = base_ref[0]
    for j in range(rows // CMP_STRIDE):
        chunk = i * (rows // CMP_STRIDE) + j
        shift = (chunk + width - (n_col - 1)) % width
        o_ref[0, j * CMP_STRIDE:(j + 1) * CMP_STRIDE, :] = pltpu.roll(base, shift, 1)[:, :n_col]


def _cmp_table(bias_by_bucket, seq, rows=1024):
    n_col = seq // CMP_STRIDE
    t = np.arange(2 * n_col) - (n_col - 1)
    dist = -CMP_STRIDE * t[None, :] + np.arange(CMP_STRIDE)[:, None] - (CMP_LEN - 1)
    base = _masked_bias(bias_by_bucket, dist, dist >= 0)
    h = base.shape[0]
    return pl.pallas_call(
        functools.partial(_cmp_table_kernel, rows=rows),
        grid=(h, seq // rows),
        in_specs=[pl.BlockSpec((1, CMP_STRIDE, 2 * n_col), lambda a, i: (a, 0, 0))],
        out_specs=pl.BlockSpec((1, rows, n_col), lambda a, i: (a, i, 0)),
        out_shape=jax.ShapeDtypeStruct((h, seq, n_col), F32),
        compiler_params=_cparams(("parallel", "parallel")),
        name="cmp_bias_table",
    )(base)


def _ada_kernel(c_ref, w_ref, b_ref, o_ref):
    o_ref[0] = jnp.dot(c_ref[...], w_ref[0].astype(BF16), preferred_element_type=F32) + b_ref[0]


def _ada_mod(c, w_ada, b_ada):
    depth, d, d3 = w_ada.shape
    cp = jnp.zeros((8, d), BF16).at[:c.shape[0]].set(c.astype(BF16))
    nj = d3 // d
    return pl.pallas_call(
        _ada_kernel,
        grid=(depth, nj),
        in_specs=[pl.BlockSpec((8, d), lambda l, j: (0, 0)),
                  pl.BlockSpec((1, d, d), lambda l, j: (l, 0, j)),
                  pl.BlockSpec((1, 1, d), lambda l, j: (l, 0, j))],
        out_specs=pl.BlockSpec((1, 8, d), lambda l, j: (l, 0, j)),
        out_shape=jax.ShapeDtypeStruct((depth, 8, d3), F32),
        compiler_params=_cparams(("parallel", "parallel")),
        name="ada_mod",
    )(cp, w_ada, b_ada.reshape(depth, 1, d3))


def _proj_kernel(x_ref, sh_ref, sc_ref, w_ref, wz_ref, *refs, dils):
    n_lay = len(dils)
    qa, ka, va = refs[:n_lay], refs[n_lay:2 * n_lay], refs[2 * n_lay:3 * n_lay]
    za, qb, kc, vc, ks, vs, kw, vw, zb, gb, h_ref, xs = refs[3 * n_lay:]
    tm = h_ref.shape[0]
    h_ref[...] = (x_ref[0] * (1.0 + sc_ref[0]) + sh_ref[0]).astype(BF16)

    def mm(c0, n):
        return jnp.dot(h_ref[...], w_ref[:, c0:c0 + n], preferred_element_type=F32)

    def emit_regrouped(a, outs):
        outs[0][0] = a.astype(BF16)
        n_slab = a.shape[1] // LANES
        for c in range(n_slab):
            xs[0, c] = a[:, c * LANES:(c + 1) * LANES]
        d_prev = 1
        for lvl, (ref, dil) in enumerate(zip(outs[1:], dils[1:])):
            ratio, len_prev, len_new = dil // d_prev, tm // d_prev, tm // dil
            src, dst = xs.at[lvl % 2], xs.at[(lvl + 1) % 2]
            keep = lvl + 2 < len(dils)
            for rp in range(d_prev):
                for rs in range(ratio):
                    r = rp + d_prev * rs
                    for c in range(n_slab):
                        rows = src[c, pl.ds(rp * len_prev + rs, len_new, stride=ratio), :]
                        ref[0, r, :, c * LANES:(c + 1) * LANES] = rows.astype(BF16)
                        if keep:
                            dst[c, r * len_new:(r + 1) * len_new, :] = rows
            d_prev = dil

    qs = HEAD_DIM ** -0.5 * LOG2E
    emit_regrouped(mm(_C_QA, D_A) * qs, qa)
    emit_regrouped(mm(_C_KA, D_A), ka)
    emit_regrouped(mm(_C_VA, D_A), va)
    za[0] = mm(_C_ZA, D_A).astype(BF16)
    qb[0] = (mm(_C_QB, D_B) * qs).astype(BF16)
    lo = _lo_mask()
    narrow = mm(_C_NARROW, 7 * LANES)
    piece = lambda i: narrow[:, i * LANES:(i + 1) * LANES]
    for ref, i in ((kc, 0), (vc, 1)):
        xs[0, 0] = piece(i)
        for t in range(0, CMP_STRIDE, 2):
            x0 = xs[0, 0, pl.ds(t, tm // CMP_STRIDE, stride=CMP_STRIDE), :]
            x1 = xs[0, 0, pl.ds(t + 1, tm // CMP_STRIDE, stride=CMP_STRIDE), :]
            cols = slice(t * HEAD_DIM, (t + 2) * HEAD_DIM)
            ref[0, 0, :, cols] = jnp.where(lo, x0, pltpu.roll(x1, HEAD_DIM, 1)).astype(BF16)
            ref[0, 1, :, cols] = jnp.where(lo, pltpu.roll(x0, HEAD_DIM, 1), x1).astype(BF16)
    for ref, i in ((ks, 2), (vs, 3), (kw, 4), (vw, 5)):
        a = piece(i)
        r = pltpu.roll(a, HEAD_DIM, 1)
        ref[0, :, :LANES] = jnp.where(lo, a, r).astype(BF16)
        ref[0, :, LANES:] = jnp.where(lo, r, a).astype(BF16)
    gb[0] = piece(6)
    zb[0] = jnp.dot(h_ref[...], wz_ref[...], preferred_element_type=F32).astype(BF16)


def _project(x, shift, scl, w, w_zb, layer, tm=512):
    b, s, d = x.shape
    dils = tuple(dl for _, dl in DILATED)
    bs = lambda n: pl.BlockSpec((1, tm, n), lambda i, j: (i, j, 0))
    gs = pl.BlockSpec((1, G_B, tm // CMP_STRIDE, CMP_STRIDE * HEAD_DIM), lambda i, j: (i, 0, j, 0))
    sd = lambda n, dt=BF16: jax.ShapeDtypeStruct((b, s, n), dt)
    gd = jax.ShapeDtypeStruct((b, G_B, s // CMP_STRIDE, CMP_STRIDE * HEAD_DIM), BF16)
    mod = pl.BlockSpec((1, 1, d), lambda i, j: (i, 0, 0))
    lay_specs = [bs(D_A)] + [pl.BlockSpec((1, dl, tm // dl, D_A), lambda i, j: (i, 0, j, 0)) for dl in dils[1:]]
    lay_shapes = [sd(D_A)] + [jax.ShapeDtypeStruct((b, dl, s // dl, D_A), BF16) for dl in dils[1:]]
    outs = pl.pallas_call(
        functools.partial(_proj_kernel, dils=dils),
        grid=(b, s // tm),
        in_specs=[pl.BlockSpec((1, tm, d), lambda i, j: (i, j, 0)), mod, mod,
                  pl.BlockSpec((None, d, w.shape[-1]), lambda i, j: (layer, 0, 0)),
                  pl.BlockSpec((None, d, D_B), lambda i, j: (layer, 0, 0))],
        out_specs=lay_specs * 3 + [bs(D_A), bs(D_B), gs, gs,
                                   bs(2 * LANES), bs(2 * LANES), bs(2 * LANES), bs(2 * LANES), bs(D_B), bs(LANES)],
        out_shape=lay_shapes * 3 + [sd(D_A), sd(D_B), gd, gd,
                                    sd(2 * LANES), sd(2 * LANES), sd(2 * LANES), sd(2 * LANES), sd(D_B), sd(LANES, F32)],
        scratch_shapes=[pltpu.VMEM((tm, d), BF16), pltpu.VMEM((2, D_A // LANES, tm, LANES), F32)],
        compiler_params=_cparams(("parallel", "parallel")),
        name="in_proj",
    )(x, shift, scl, w, w_zb)
    n = len(dils)
    return (outs[:n], outs[n:2 * n], outs[2 * n:3 * n], *outs[3 * n:])


def _dilated_kernel(*refs, seq, unroll):
    n_pat = len(DILATED)
    qkv = refs[:3 * n_pat]
    t_refs = refs[3 * n_pat:4 * n_pat]
    o_ref = refs[4 * n_pat]
    qlo, qhi, kp, vp, o_r, m_r, l_r, o_t, m_t, l_t, m_acc, n_acc, d_acc = refs[4 * n_pat + 1:]
    lo = _lo_mask()
    nblk = seq // BLK
    chunk = 512
    kp[:BLK, :] = jnp.zeros((BLK, LANES), BF16)
    vp[:BLK, :] = jnp.zeros((BLK, LANES), BF16)

    for pi, ((_, dil), t_ref) in enumerate(zip(DILATED, t_refs)):
        q_ref, k_ref, v_ref = qkv[pi], qkv[n_pat + pi], qkv[2 * n_pat + pi]
        rl = seq // dil
        nb = rl // BLK
        for r in range(dil):
            idx = (0,) if dil == 1 else (0, r)
            q = q_ref[idx]
            zero = jnp.zeros_like(q)
            qlo[r * rl:(r + 1) * rl, :] = jnp.where(lo, q, zero)
            qhi[r * rl:(r + 1) * rl, :] = jnp.where(lo, zero, q)
            kp[BLK + r * rl:BLK + (r + 1) * rl, :] = k_ref[idx]
            vp[BLK + r * rl:BLK + (r + 1) * rl, :] = v_ref[idx]
        o_dst, m_dst, l_dst = (n_acc, m_acc, d_acc) if dil == 1 else (o_r, m_r, l_r)

        def blocks(it, carry, t_ref=t_ref, nb=nb, o_dst=o_dst, m_dst=m_dst, l_dst=l_dst):
            for u in range(unroll):
                g = it * unroll + u
                first = jnp.where((g & (nb - 1)) == 0, 1, 0)
                q0 = pl.multiple_of(g * BLK, BLK)
                qs = jnp.concatenate([qlo[pl.ds(q0, BLK), :], qhi[pl.ds(q0, BLK), :]], axis=0)
                s = _dot_nt(qs, kp[pl.ds(q0, 2 * BLK), :]) + t_ref[0, first]
                m = jnp.max(s, axis=-1, keepdims=True)
                p = jnp.exp2(s - m)
                l = jnp.sum(p, axis=-1, keepdims=True)
                pv = jnp.dot(p.astype(BF16), vp[pl.ds(q0, 2 * BLK), :], preferred_element_type=F32)
                o_dst[pl.ds(q0, BLK), :] = jnp.where(lo, pv[:BLK], pv[BLK:])
                m_dst[pl.ds(q0, BLK), :] = jnp.where(lo, m[:BLK], m[BLK:])
                l_dst[pl.ds(q0, BLK), :] = jnp.where(lo, l[:BLK], l[BLK:])
            return carry

        lax.fori_loop(0, nblk // unroll, blocks, 0)
        if dil == 1:
            continue

        for r in range(dil):
            for src, dst in ((o_r, o_t), (m_r, m_t), (l_r, l_t)):
                dst[pl.ds(r, rl, stride=dil), :] = src[r * rl:(r + 1) * rl, :]

        def fold(c, carry, last=(pi == n_pat - 1)):
            rows = pl.ds(pl.multiple_of(c * chunk, chunk), chunk)
            m_old, m_new = m_acc[rows, :], m_t[rows, :]
            mx = jnp.maximum(m_old, m_new)
            a, b_ = jnp.exp2(m_old - mx), jnp.exp2(m_new - mx)
            n_new = a * n_acc[rows, :] + b_ * o_t[rows, :]
            d_new = a * d_acc[rows, :] + b_ * l_t[rows, :]
            if last:
                o_ref[0, rows, :] = (n_new * (1.0 / d_new)).astype(o_ref.dtype)
            else:
                m_acc[rows, :] = mx
                n_acc[rows, :] = n_new
                d_acc[rows, :] = d_new
            return carry

        lax.fori_loop(0, seq // chunk, fold, 0)


def _dilated_attention(qa, ka, va, tables, unroll=32):
    assert DILATED[0][1] == 1 and all(dl > 1 for _, dl in DILATED[1:])
    b, s, c = qa[0].shape
    xs = pl.BlockSpec((1, s, LANES), lambda i, j: (i, 0, j))
    lay = [xs] + [pl.BlockSpec((1, dl, s // dl, LANES), lambda i, j: (i, 0, 0, j)) for _, dl in DILATED[1:]]
    ts = pl.BlockSpec((1, 2, 2 * BLK, 2 * BLK), lambda i, j: (j, 0, 0, 0))
    big = lambda rows, dt: pltpu.VMEM((rows, LANES), dt)
    return pl.pallas_call(
        functools.partial(_dilated_kernel, seq=s, unroll=unroll),
        grid=(b, c // LANES),
        in_specs=lay * 3 + [ts] * len(DILATED),
        out_specs=xs,
        out_shape=jax.ShapeDtypeStruct((b, s, c), BF16),
        scratch_shapes=[big(s, BF16)] * 2 + [big(s + BLK, BF16)] * 2 + [big(s, F32)] * 9,
        compiler_params=_cparams(("parallel", "parallel")),
        name="dilated_attn",
    )(*qa, *ka, *va, *tables)


def _win_kernel(q_ref, k_ref, v_ref, t_ref, o_ref, qst, kaug, vp, *, seq, n_prev, unroll):
    lo = _lo_mask()
    lane = lax.broadcasted_iota(jnp.int32, (1, LANES), 1)
    flag = lane == HEAD_DIM
    pad = n_prev * BLK
    span = (n_prev + 1) * BLK
    negflag = jnp.where(flag, NEG, 0.0)
    for pr in range(R_B // 2):
        q = q_ref[0, :, pr * LANES:(pr + 1) * LANES].astype(F32)
        qst[2 * pr] = jnp.where(lo, q, negflag).astype(BF16)
        qst[2 * pr + 1] = jnp.where(lo, pltpu.roll(q, HEAD_DIM, 1), negflag).astype(BF16)
    k = k_ref[0]
    kaug[:pad, :] = jnp.broadcast_to(jnp.where(flag, 1.0, 0.0), (pad, LANES)).astype(BF16)
    kaug[pad:, :] = jnp.where(lo, k, jnp.zeros_like(k))
    vp[:pad, :] = jnp.zeros((pad, LANES), BF16)
    vp[pad:, :] = jnp.where(lo, v_ref[0], jnp.ones((), BF16))

    def blocks(it, carry):
        for u in range(unroll):
            n = it * unroll + u
            r0 = pl.multiple_of(n * BLK, BLK)
            qs = jnp.concatenate([qst[h, pl.ds(r0, BLK), :] for h in range(R_B)], axis=0)
            s = _dot_nt(qs, kaug[pl.ds(r0, span), :]) + t_ref[0]
            m = jnp.max(s, axis=-1, keepdims=True)
            p = jnp.exp2(s - m)
            pv = jnp.dot(p.astype(BF16), vp[pl.ds(r0, span), :], preferred_element_type=F32)
            for pr in range(R_B // 2):
                even, odd = pv[2 * pr * BLK:(2 * pr + 1) * BLK], pv[(2 * pr + 1) * BLK:(2 * pr + 2) * BLK]
                o = jnp.where(lo, even * (1.0 / pltpu.roll(even, HEAD_DIM, 1)), pltpu.roll(odd, HEAD_DIM, 1) * (1.0 / odd))
                o_ref[0, pl.ds(r0, BLK), pr * LANES:(pr + 1) * LANES] = o.astype(o_ref.dtype)
        return carry

    lax.fori_loop(0, seq // BLK // unroll, blocks, 0)


def _win_attention(qb, kw_dup, vw_dup, table, n_prev, unroll=8):
    b, s, _ = qb.shape
    gw = R_B * HEAD_DIM
    w = table.shape[-1]
    return pl.pallas_call(
        functools.partial(_win_kernel, seq=s, n_prev=n_prev, unroll=unroll),
        grid=(b, G_B),
        in_specs=[pl.BlockSpec((1, s, gw), lambda i, g: (i, 0, g)),
                  pl.BlockSpec((1, s, LANES), lambda i, g: (i, 0, g)),
                  pl.BlockSpec((1, s, LANES), lambda i, g: (i, 0, g)),
                  pl.BlockSpec((1, R_B * BLK, w), lambda i, g: (g, 0, 0))],
        out_specs=pl.BlockSpec((1, s, gw), lambda i, g: (i, 0, g)),
        out_shape=jax.ShapeDtypeStruct((b, s, D_B), BF16),
        scratch_shapes=[pltpu.VMEM((R_B, s, LANES), BF16)] + [pltpu.VMEM((s + n_prev * BLK, LANES), BF16)] * 2,
        compiler_params=_cparams(("parallel", "parallel")),
        name="nsa_window",
    )(qb, kw_dup, vw_dup, table)


def _compress_kernel(kin, vin, w1, w2, pe, kout, vout):
    n_chunk = kin.shape[2]
    half = w1.shape[1] // 2
    rows = lax.broadcasted_iota(jnp.int32, (n_chunk, 1), 0)
    for t, (xin, out) in enumerate(((kin, kout), (vin, vout))):
        x = xin[0, 0]
        a = jnp.dot(x, w1[t, :half], preferred_element_type=F32)
        b = jnp.dot(x, w1[t, half:], preferred_element_type=F32)
        c = jnp.dot(pe[t], w1[t], preferred_element_type=F32)[0:1]
        hid = a + pltpu.roll(b, n_chunk - 1, 0) + c
        act = hid * jax.nn.sigmoid(hid)
        o = jnp.dot(act.astype(BF16), w2[t], preferred_element_type=F32)
        out[0, 0] = jnp.where(rows < n_chunk - 1, o, 0.0).astype(BF16)


def _compress(kc_in, vc_in, w1, w2d, pe, layer):
    b, g, n_chunk, f = kc_in.shape
    xs = pl.BlockSpec((1, 1, n_chunk, f), lambda i, j: (i, j, 0, 0))
    os_ = pl.BlockSpec((1, 1, n_chunk, LANES), lambda i, j: (i, j, 0, 0))
    full = lambda a: pl.BlockSpec((None,) + a.shape[1:], lambda i, j: (layer,) + (0,) * (a.ndim - 1))
    od = jax.ShapeDtypeStruct((b, g, n_chunk, LANES), BF16)
    return pl.pallas_call(
        _compress_kernel,
        grid=(b, g),
        in_specs=[xs, xs, full(w1), full(w2d), full(pe)],
        out_specs=[os_, os_],
        out_shape=[od, od],
        compiler_params=_cparams(("parallel", "parallel")),
        name="nsa_compress",
    )(kc_in, vc_in, w1, w2d, pe)


def _cmp_kernel(q_ref, kc_ref, vc_ref, t_ref, ov_ref, o_ref, sb_ref, *, tq):
    i = pl.program_id(2)
    lo = _lo_mask()
    n_col, n_slc = kc_ref.shape[2], ov_ref.shape[0]

    def attend_and_rank(ncol, nsel):
        kcd = kc_ref[0, 0, :ncol, :]
        vcd = vc_ref[0, 0, :ncol, :]
        zero = jnp.zeros_like(vcd)
        vlo = jnp.where(lo, vcd, zero)
        vhi = jnp.where(lo, zero, vcd)
        psum = None
        for pr in range(R_B // 2):
            q = q_ref[0, :, pr * LANES:(pr + 1) * LANES]
            qz = jnp.zeros_like(q)
            acc = None
            for hh, (qm, vm) in enumerate(((jnp.where(lo, q, qz), vlo), (jnp.where(lo, qz, q), vhi))):
                s = _dot_nt(qm, kcd) + t_ref[2 * pr + hh, :, :ncol]
                m = jnp.max(s, axis=-1, keepdims=True)
                e = jnp.exp2(s - m)
                l = jnp.sum(e, axis=-1, keepdims=True)
                p = e * jnp.where(m > 0.5 * NEG, 1.0 / l, 0.0)
                psum = p if psum is None else psum + p
                pv = jnp.dot(p.astype(BF16), vm, preferred_element_type=F32)
                acc = pv if acc is None else acc + pv
            o_ref[0, :, pr * LANES:(pr + 1) * LANES] = acc.astype(o_ref.dtype)

        p_hi = psum.astype(BF16)
        p_lo = (psum - p_hi.astype(F32)).astype(BF16)
        ov = ov_ref[:nsel, :ncol]
        score = (_dot_nt(ov, p_hi) + _dot_nt(ov, p_lo))
        blk = lax.broadcasted_iota(jnp.int32, (nsel, 1), 0)
        pos = i * tq + lax.broadcasted_iota(jnp.int32, (1, tq), 1)
        cur = pos >> int(math.log2(SLC_BLOCK))
        forced = (blk == 0) | ((cur - blk >= 0) & (cur - blk < SLC_LOCAL))
        score = jnp.where(forced, 1e9, jnp.where(blk > cur, -1e9, score))
        sub = lax.broadcasted_iota(jnp.int32, (8, 1), 0)
        sel_rows = []
        for g8 in range(nsel // 8):
            sg = score[g8 * 8:(g8 + 1) * 8]
            rank = jnp.zeros(sg.shape, F32)
            for mp in range(nsel):
                row = score[mp:mp + 1]
                if mp < g8 * 8:
                    ahead = jnp.where(row >= sg, 1.0, 0.0)
                elif mp >= (g8 + 1) * 8:
                    ahead = jnp.where(row > sg, 1.0, 0.0)
                else:
                    ahead = jnp.where(sub > mp - g8 * 8, jnp.where(row >= sg, 1.0, 0.0), jnp.where(row > sg, 1.0, 0.0))
                rank = rank + ahead
            sel_rows.append(jnp.where(rank < float(min(SLC_TOPK, n_slc)), 0.0, NEG))
        if nsel < n_slc:
            sel_rows.append(jnp.full((n_slc - nsel, tq), NEG, F32))
        pad = LANES // n_slc
        selb = jnp.concatenate(sel_rows * pad, axis=0)
        sb_ref[0, 0] = selb.T.astype(BF16)

    n_tile = pl.num_programs(2)
    for k in range(4):
        ncol = max(LANES, -(-(n_col * (k + 1) // 4) // LANES) * LANES)
        in_quarter = (i * 4 >= k * n_tile) & (i * 4 < (k + 1) * n_tile)
        pl.when(in_quarter)(functools.partial(attend_and_rank, ncol, n_slc * (k + 1) // 4))


def _cmp_attention(qb, kcd, vcd, table, ov_t, tq=512):
    b, s, _ = qb.shape
    n_col = kcd.shape[2]
    gw = R_B * HEAD_DIM
    return pl.pallas_call(
        functools.partial(_cmp_kernel, tq=tq),
        grid=(b, G_B, s // tq),
        in_specs=[pl.BlockSpec((1, tq, gw), lambda i, g, j: (i, j, g)),
                  pl.BlockSpec((1, 1, n_col, LANES), lambda i, g, j: (i, g, 0, 0)),
                  pl.BlockSpec((1, 1, n_col, LANES), lambda i, g, j: (i, g, 0, 0)),
                  pl.BlockSpec((R_B, tq, n_col), lambda i, g, j: (g, j, 0)),
                  pl.BlockSpec(ov_t.shape, lambda i, g, j: (0, 0))],
        out_specs=[pl.BlockSpec((1, tq, gw), lambda i, g, j: (i, j, g)),
                   pl.BlockSpec((1, 1, tq, LANES), lambda i, g, j: (i, g, j, 0))],
        out_shape=[jax.ShapeDtypeStruct((b, s, D_B), BF16),
                   jax.ShapeDtypeStruct((b, G_B, s, LANES), BF16)],
        compiler_params=_cparams(("parallel", "parallel", "parallel")),
        name="nsa_cmp_topk",
    )(qb, kcd, vcd, table, ov_t)


def _sel_kernel(q_ref, sb_ref, k_ref, v_ref, t_ref, o_ref, kaug, vp, m_ref, acc_ref, *, nblk):
    lo = _lo_mask()
    s_len = nblk * BLK
    n_diag = t_ref.shape[1] - 1
    h0 = pl.program_id(1) * R_B
    rowblk = lax.broadcasted_iota(jnp.int32, (s_len, LANES), 0) >> int(math.log2(SLC_BLOCK))
    lane = lax.broadcasted_iota(jnp.int32, (s_len, LANES), 1)
    onehot = jnp.where(lane - HEAD_DIM == rowblk, 1.0, 0.0).astype(BF16)
    kaug[...] = jnp.where(lo, k_ref[0], onehot)
    vp[...] = jnp.where(lo, v_ref[0], jnp.ones((), BF16))
    m_ref[...] = jnp.full(m_ref.shape, NEG, F32)
    acc_ref[...] = jnp.zeros(acc_ref.shape, F32)
    kb = 4
    qb = 2
    sub = kb // qb
    kt_rows, qt_rows = kb * BLK, qb * BLK

    def ktile(t, carry):
        k0 = pl.multiple_of(t * kt_rows, kt_rows)
        kt = kaug[pl.ds(k0, kt_rows), :]
        vt = vp[pl.ds(k0, kt_rows), :]

        def qstep(mp, c):
            scores = []
            for u in range(sub):
                r0 = pl.multiple_of((sub * mp + u) * qt_rows, qt_rows)
                sbf = sb_ref[0, 0, pl.ds(r0, qt_rows), :].astype(F32)
                rows = []
                for pr in range(R_B // 2):
                    q = q_ref[0, pl.ds(r0, qt_rows), pr * LANES:(pr + 1) * LANES].astype(F32)
                    rows.append(jnp.where(lo, q, sbf))
                    rows.append(jnp.where(lo, pltpu.roll(q, HEAD_DIM, 1), sbf))
                qst = jnp.concatenate(rows, axis=0).astype(BF16)
                bias = jnp.concatenate([
                    jnp.concatenate([t_ref[h0 + h, jnp.clip(qb * (sub * mp + u) + a - kb * t - c + 1, 0, n_diag)]
                                     for c in range(kb)], axis=1)
                    for h in range(R_B) for a in range(qb)], axis=0)
                scores.append(_dot_nt(qst, kt) + bias)
            for u in range(sub):
                r0 = pl.multiple_of((sub * mp + u) * qt_rows, qt_rows)
                s = scores[u]
                ps, alphas = [], []
                for h in range(R_B):
                    for a in range(qb):
                        lo_r = h * qt_rows + a * BLK
                        st = pl.ds(r0 + a * BLK, BLK)
                        sh = s[lo_r:lo_r + BLK]
                        m_prev = m_ref[h, st, :]
                        m_new = jnp.maximum(m_prev, jnp.max(sh, axis=-1, keepdims=True))
                        alpha = jnp.exp2(m_prev - m_new)
                        p = jnp.exp2(sh - jnp.concatenate([m_new] * kb, axis=1))
                        m_ref[h, st, :] = m_new
                        ps.append(p.astype(BF16))
                        alphas.append(alpha)
                pv = jnp.dot(jnp.concatenate(ps, axis=0), vt, preferred_element_type=F32)
                for h in range(R_B):
                    for a in range(qb):
                        lo_r = h * qt_rows + a * BLK
                        st = pl.ds(r0 + a * BLK, BLK)
                        acc_ref[h, st, :] = alphas[qb * h + a] * acc_ref[h, st, :] + pv[lo_r:lo_r + BLK]
            return c

        lax.fori_loop(t, nblk // kb, qstep, 0)
        return carry

    lax.fori_loop(0, nblk // kb, ktile, 0)

    def finish(n, carry):
        st = pl.ds(pl.multiple_of(n * BLK, BLK), BLK)
        for pr in range(R_B // 2):
            even, odd = acc_ref[2 * pr, st, :], acc_ref[2 * pr + 1, st, :]
            even = even * (1.0 / pltpu.roll(even, HEAD_DIM, 1))
            odd = pltpu.roll(odd, HEAD_DIM, 1) * (1.0 / odd)
            o = jnp.where(lo, even, odd)
            o_ref[0, st, pr * LANES:(pr + 1) * LANES] = o.astype(o_ref.dtype)
        return carry

    lax.fori_loop(0, nblk, finish, 0)


def _sel_attention(qb, selb, ks_dup, vs_dup, table):
    b, s, _ = qb.shape
    gw = R_B * HEAD_DIM
    return pl.pallas_call(
        functools.partial(_sel_kernel, nblk=s // BLK),
        grid=(b, G_B),
        in_specs=[pl.BlockSpec((1, s, gw), lambda i, g: (i, 0, g)),
                  pl.BlockSpec((1, 1, s, LANES), lambda i, g: (i, g, 0, 0)),
                  pl.BlockSpec((1, s, LANES), lambda i, g: (i, 0, g)),
                  pl.BlockSpec((1, s, LANES), lambda i, g: (i, 0, g)),
                  pl.BlockSpec(table.shape, lambda i, g: (0, 0, 0, 0), pipeline_mode=pl.Buffered(1))],
        out_specs=pl.BlockSpec((1, s, gw), lambda i, g: (i, 0, g)),
        out_shape=jax.ShapeDtypeStruct((b, s, D_B), BF16),
        scratch_shapes=[pltpu.VMEM((s, LANES), BF16)] * 2 + [pltpu.VMEM((R_B, s, LANES), F32)] * 2,
        compiler_params=_cparams(("parallel", "parallel")),
        name="nsa_selected",
    )(qb, selb, ks_dup, vs_dup, table)


def _combine_kernel(x_ref, oa, za, oc, osl, ow, gb, zb, wo, ex, gate, lng, lnb, out_ref, *, alpha):
    f = lambda r: r[0].astype(F32)
    z = f(za)
    mix_a = f(oa) * (z * jax.nn.sigmoid(z))
    g = jax.nn.sigmoid(gb[0])
    g_hi = g.astype(BF16)
    g_lo = (g - g_hi.astype(F32)).astype(BF16)
    gx = (jnp.dot(g_hi, ex[...], preferred_element_type=F32)
          + jnp.dot(g_lo, ex[...], preferred_element_type=F32))
    ob = gx[:, :D_B] * f(oc) + gx[:, D_B:2 * D_B] * f(osl) + gx[:, 2 * D_B:] * f(ow)
    z = f(zb)
    mix_b = ob * (z * jax.nn.sigmoid(z))
    y = (jnp.dot(mix_a.astype(BF16), wo[:D_A], preferred_element_type=F32)
         + jnp.dot(mix_b.astype(BF16), wo[D_A:], preferred_element_type=F32))
    r = alpha * x_ref[0] + (1.0 + gate[0]) * y
    mu = jnp.mean(r, axis=-1, keepdims=True)
    rc = r - mu
    var = jnp.mean(rc * rc, axis=-1, keepdims=True)
    out_ref[0] = rc * lax.rsqrt(var + LN_EPS) * lng[...] + lnb[...]


def _combine(x, o_a, za, o_cmp, o_slc, o_win, gb, zb, w_out, expand, gate, ln_g, ln_b, alpha, layer, tm=512):
    b, s, d = x.shape
    bs = lambda n: pl.BlockSpec((1, tm, n), lambda i, j: (i, j, 0))
    full = lambda a: pl.BlockSpec(a.shape, lambda i, j: (0,) * a.ndim)
    return pl.pallas_call(
        functools.partial(_combine_kernel, alpha=alpha),
        grid=(b, s // tm),
        in_specs=[bs(d)] + [bs(D_A)] * 2 + [bs(D_B)] * 3 + [bs(LANES), bs(D_B),
                  pl.BlockSpec((None,) + w_out.shape[1:], lambda i, j: (layer, 0, 0)), full(expand),
                  pl.BlockSpec((1, 1, d), lambda i, j: (i, 0, 0)), full(ln_g), full(ln_b)],
        out_specs=bs(d),
        out_shape=jax.ShapeDtypeStruct((b, s, d), F32),
        compiler_params=_cparams(("parallel", "parallel")),
        name="merge_out_proj_ln",
    )(x, o_a, za, o_cmp, o_slc, o_win, gb, zb, w_out, expand, gate, ln_g, ln_b)


def _gate_expand():
    e = np.zeros((LANES, 3 * D_B), np.float32)
    for i in range(3):
        for h in range(H_B):
            e[i * H_B + h, i * D_B + h * HEAD_DIM:i * D_B + (h + 1) * HEAD_DIM] = 1.0
    return jnp.asarray(e, BF16)


def _overlap_t(n_col, n_slc):
    cs = np.arange(n_col)[None, :] * CMP_STRIDE
    ss = np.arange(n_slc)[:, None] * SLC_BLOCK
    return jnp.asarray(((cs < ss + SLC_BLOCK) & (cs + CMP_LEN > ss)).astype(np.float32), BF16)


def _layer(x, shift, scl, gate, w_in_b, w_zb, w_out_b, pe_b, w1_b, w2d_b, ln_g, ln_b, tables, consts, alpha, layer):
    b, s, d = x.shape
    (qa, ka, va, za, qb, kc_in, vc_in, ks, vs, kw, vw, zb, gb) = _project(x, shift, scl, w_in_b, w_zb, layer)

    o_a = _dilated_attention(qa, ka, va, tables["dilated"])

    kcd, vcd = _compress(kc_in, vc_in, w1_b, w2d_b, pe_b, layer)
    o_cmp, selb = _cmp_attention(qb, kcd, vcd, tables["cmp"], consts["overlap_t"])
    o_slc = _sel_attention(qb, selb, ks, vs, tables["sel"])
    o_win = _win_attention(qb, kw, vw, tables["win"], n_prev=-(-(WIN - 1) // BLK))

    return _combine(x, o_a, za, o_cmp, o_slc, o_win, gb, zb, w_out_b, consts["expand"], gate,
                    ln_g, ln_b, alpha, layer)


def kernel(x, c, w_in, w_out, cmp_pe, cmp_w1, cmp_w2, w_ada, b_ada, ln_g, ln_b, rel_bias):
    b, s, d = x.shape
    depth = w_in.shape[0]
    alpha = (2 * depth) ** 0.25
    n_pair = D_A // LANES

    bb = rel_bias.astype(F32) * LOG2E
    def dilated_table(window, dil):
        t = _band_table(bb[:, :H_A], 1, window // dil, dil).reshape(n_pair, 2 * BLK, 2 * BLK)
        first = t.at[..., :BLK].set(NEG)
        return jnp.stack([t, first], axis=1)

    n_prev_win = -(-(WIN - 1) // BLK)
    tables = {
        "dilated": [dilated_table(w, dl) for w, dl in DILATED],
        "win": _band_table(bb[:, H_A:], n_prev_win, WIN - 1, 1).reshape(G_B, R_B * BLK, (n_prev_win + 1) * BLK),
        "cmp": _cmp_table(bb[:, H_A:], s),
        "sel": _diag_table(bb[:, H_A:], 13),
    }
    consts = {"expand": _gate_expand(), "overlap_t": _overlap_t(s // CMP_STRIDE, s // SLC_BLOCK)}

    mod = _ada_mod(c, w_ada, b_ada)
    w_in_b = w_in.astype(BF16)
    w_zb = w_in[..., _C_ZB:].astype(BF16)
    w_out_b = w_out.astype(BF16)
    w1_b = cmp_w1.astype(BF16)
    w2d_b = jnp.concatenate([cmp_w2, cmp_w2], axis=-1).astype(BF16)
    pe_b = jnp.broadcast_to(cmp_pe.reshape(depth, 2, 1, CMP_LEN * HEAD_DIM), (depth, 2, 8, CMP_LEN * HEAD_DIM)).astype(BF16)

    for l in range(depth):
        m = mod[l, :b]
        shift, scl, gate = (m[:, i * d:(i + 1) * d].reshape(b, 1, d) for i in range(3))
        x = _layer(x, shift, scl, gate, w_in_b, w_zb, w_out_b, pe_b, w1_b, w2d_b,
                   ln_g[l].reshape(1, d), ln_b[l].reshape(1, d), tables, consts, alpha, l)
    return x
```

```python
import functools
import math

import numpy as np
import jax
import jax.numpy as jnp
from jax import lax
from jax.experimental import pallas as pl
from jax.experimental.pallas import tpu as pltpu

F32 = jnp.float32
BF16 = jnp.bfloat16

D_MODEL = 1024
HEAD_DIM = 64
H_A = 8
H_B = 8
G_B = 2
R_B = H_B // G_B
D_A = H_A * HEAD_DIM
D_B = H_B * HEAD_DIM
DILATED = ((128, 1), (512, 4), (2048, 16))
CMP_LEN = 32
CMP_STRIDE = 16
CMP_HIDDEN = 256
SLC_BLOCK = 64
SLC_TOPK = 16
SLC_LOCAL = 2
WIN = 512
N_BUCKETS = 32
MAX_DIST = 2048
LN_EPS = 1e-5
NEG = -1e30
LOG2E = 1.4426950408889634

LANES = 128
BLK = 128
VMEM_BYTES_V7X = 64 * 1024 * 1024
VMEM_LIMIT = VMEM_BYTES_V7X - 8 * 1024 * 1024

_C_QA, _C_KA, _C_VA, _C_ZA, _C_QB = 0, 512, 1024, 1536, 2048
_C_NARROW = 2560
_C_ZB = 3328 + 3 * H_B


def _cparams(sem):
    return pltpu.CompilerParams(dimension_semantics=sem, vmem_limit_bytes=VMEM_LIMIT)


def _lo_mask():
    return lax.broadcasted_iota(jnp.int32, (1, LANES), 1) < HEAD_DIM


def _normalise_pair(even, odd):
    lo = _lo_mask()
    straight = jnp.where(lo, even, odd)
    swapped = pltpu.roll(jnp.where(lo, odd, even), HEAD_DIM, 1)
    return jnp.where(lo, straight, swapped) * (1.0 / jnp.where(lo, swapped, straight))


def _dot_nt(a, b):
    return lax.dot_general(a, b, (((1,), (1,)), ((), ())), preferred_element_type=F32)


def _bucket_np(dist):
    d = np.maximum(dist, 0)
    max_exact = N_BUCKETS // 2
    large = max_exact + (np.log(np.maximum(d, 1).astype(np.float32) / np.float32(max_exact))
                         / np.float32(math.log(MAX_DIST / max_exact))
                         * np.float32(N_BUCKETS - max_exact)).astype(np.int32)
    return np.where(d < max_exact, d, np.minimum(large, N_BUCKETS - 1)).astype(np.int32)


def _masked_bias(bias_by_bucket, dist, valid):
    onehot = np.eye(N_BUCKETS, dtype=np.float32)[_bucket_np(dist)]
    t = jnp.einsum("...k,kh->h...", onehot, bias_by_bucket, precision=lax.Precision.HIGHEST)
    return jnp.where(jnp.asarray(valid)[None], t, NEG)


def _toeplitz_kernel(v_ref, o_ref):
    cols = o_ref.shape[-1]
    for e in range(o_ref.shape[1]):
        x = jnp.broadcast_to(v_ref[0, e:e + 1, :], (BLK, v_ref.shape[-1]))
        o_ref[0, e] = pltpu.roll(x, 0, 1, stride=1, stride_axis=0)[:, :cols]


def _toeplitz(h, cols):
    n_h, n_e, n = h.shape
    width = 1 << (n - 1).bit_length()
    vec = jnp.concatenate([h[..., BLK - 1:], jnp.zeros((n_h, n_e, width - n), F32), h[..., :BLK - 1]], axis=-1)
    return pl.pallas_call(
        _toeplitz_kernel,
        grid=(n_h,),
        in_specs=[pl.BlockSpec((1, n_e, width), lambda a: (a, 0, 0))],
        out_specs=pl.BlockSpec((1, n_e, BLK, cols), lambda a: (a, 0, 0, 0)),
        out_shape=jax.ShapeDtypeStruct((n_h, n_e, BLK, cols), F32),
        compiler_params=_cparams(("parallel",)),
        name="toeplitz_table",
    )(vec)


def _band_table(bias_by_bucket, n_prev, window, scale):
    w = (n_prev + 1) * BLK
    dist = n_prev * BLK - (np.arange(BLK - 1 + w) - (BLK - 1))
    h = _masked_bias(bias_by_bucket, dist * scale, (dist >= 0) & (dist <= window))
    return _toeplitz(h[:, None], w)[:, 0]


def _diag_table(bias_by_bucket, n_diag):
    delta = np.arange(-1, n_diag + 1)[:, None]
    dist = delta * BLK - (np.arange(2 * BLK - 1)[None, :] - (BLK - 1))
    return _toeplitz(_masked_bias(bias_by_bucket, dist, dist >= 0), BLK)


def _cmp_table_kernel(base_ref, o_ref, *, rows):
    i = pl.program_id(1)
    n_col = o_ref.shape[-1]
    width = base_ref.shape[-1]
    base = base_ref[0]
    for j in range(rows // CMP_STRIDE):
        chunk = i * (rows // CMP_STRIDE) + j
        shift = (chunk + width - (n_col - 1)) % width
        o_ref[0, j * CMP_STRIDE:(j + 1) * CMP_STRIDE, :] = pltpu.roll(base, shift, 1)[:, :n_col]


def _cmp_table(bias_by_bucket, seq, rows=1024):
    n_col = seq // CMP_STRIDE
    t = np.arange(2 * n_col) - (n_col - 1)
    dist = -CMP_STRIDE * t[None, :] + np.arange(CMP_STRIDE)[:, None] - (CMP_LEN - 1)
    base = _masked_bias(bias_by_bucket, dist, dist >= 0)
    h = base.shape[0]
    return pl.pallas_call(
        functools.partial(_cmp_table_kernel, rows=rows),
        grid=(h, seq // rows),
        in_specs=[pl.BlockSpec((1, CMP_STRIDE, 2 * n_col), lambda a, i: (a, 0, 0))],
        out_specs=pl.BlockSpec((1, rows, n_col), lambda a, i: (a, i, 0)),
        out_shape=jax.ShapeDtypeStruct((h, seq, n_col), F32),
        compiler_params=_cparams(("parallel", "parallel")),
        name="cmp_bias_table",
    )(base)


def _ada_kernel(c_ref, w_ref, b_ref, o_ref):
    o_ref[0] = jnp.dot(c_ref[...], w_ref[0].astype(BF16), preferred_element_type=F32) + b_ref[0]


def _ada_mod(c, w_ada, b_ada):
    depth, d, d3 = w_ada.shape
    cp = jnp.zeros((8, d), BF16).at[:c.shape[0]].set(c.astype(BF16))
    nj = d3 // d
    return pl.pallas_call(
        _ada_kernel,
        grid=(depth, nj),
        in_specs=[pl.BlockSpec((8, d), lambda l, j: (0, 0)),
                  pl.BlockSpec((1, d, d), lambda l, j: (l, 0, j)),
                  pl.BlockSpec((1, 1, d), lambda l, j: (l, 0, j))],
        out_specs=pl.BlockSpec((1, 8, d), lambda l, j: (l, 0, j)),
        out_shape=jax.ShapeDtypeStruct((depth, 8, d3), F32),
        compiler_params=_cparams(("parallel", "parallel")),
        name="ada_mod",
    )(cp, w_ada, b_ada.reshape(depth, 1, d3))


def _proj_kernel(x_ref, sh_ref, sc_ref, w_ref, wz_ref, *refs, dils):
    n_lay = len(dils)
    qa, ka, va = refs[:n_lay], refs[n_lay:2 * n_lay], refs[2 * n_lay:3 * n_lay]
    za, qb, kc, vc, ks, vs, kw, vw, zb, gb, h_ref, xs = refs[3 * n_lay:]
    tm = h_ref.shape[0]
    h_ref[...] = (x_ref[0] * (1.0 + sc_ref[0]) + sh_ref[0]).astype(BF16)

    def mm(c0, n):
        return jnp.dot(h_ref[...], w_ref[:, c0:c0 + n], preferred_element_type=F32)

    def emit_regrouped(a, outs):
        outs[0][0] = a.astype(BF16)
        n_slab = a.shape[1] // LANES
        for c in range(n_slab):
            xs[0, c] = a[:, c * LANES:(c + 1) * LANES]
        d_prev = 1
        for lvl, (ref, dil) in enumerate(zip(outs[1:], dils[1:])):
            ratio, len_prev, len_new = dil // d_prev, tm // d_prev, tm // dil
            src, dst = xs.at[lvl % 2], xs.at[(lvl + 1) % 2]
            keep = lvl + 2 < len(dils)
            for rp in range(d_prev):
                for rs in range(ratio):
                    r = rp + d_prev * rs
                    for c in range(n_slab):
                        rows = src[c, pl.ds(rp * len_prev + rs, len_new, stride=ratio), :]
                        ref[0, r, :, c * LANES:(c + 1) * LANES] = rows.astype(BF16)
                        if keep:
                            dst[c, r * len_new:(r + 1) * len_new, :] = rows
            d_prev = dil

    qs = HEAD_DIM ** -0.5 * LOG2E
    emit_regrouped(mm(_C_QA, D_A) * qs, qa)
    emit_regrouped(mm(_C_KA, D_A), ka)
    emit_regrouped(mm(_C_VA, D_A), va)
    za[0] = mm(_C_ZA, D_A).astype(BF16)
    qb[0] = (mm(_C_QB, D_B) * qs).astype(BF16)
    lo = _lo_mask()
    narrow = mm(_C_NARROW, 7 * LANES)
    piece = lambda i: narrow[:, i * LANES:(i + 1) * LANES]
    for ref, i in ((kc, 0), (vc, 1)):
        xs[0, 0] = piece(i)
        for t in range(0, CMP_STRIDE, 2):
            x0 = xs[0, 0, pl.ds(t, tm // CMP_STRIDE, stride=CMP_STRIDE), :]
            x1 = xs[0, 0, pl.ds(t + 1, tm // CMP_STRIDE, stride=CMP_STRIDE), :]
            cols = slice(t * HEAD_DIM, (t + 2) * HEAD_DIM)
            ref[0, 0, :, cols] = jnp.where(lo, x0, pltpu.roll(x1, HEAD_DIM, 1)).astype(BF16)
            ref[0, 1, :, cols] = jnp.where(lo, pltpu.roll(x0, HEAD_DIM, 1), x1).astype(BF16)
    for ref, i in ((ks, 2), (vs, 3), (kw, 4), (vw, 5)):
        a = piece(i)
        r = pltpu.roll(a, HEAD_DIM, 1)
        ref[0, :, :LANES] = jnp.where(lo, a, r).astype(BF16)
        ref[0, :, LANES:] = jnp.where(lo, r, a).astype(BF16)
    gb[0] = piece(6)
    zb[0] = jnp.dot(h_ref[...], wz_ref[...], preferred_element_type=F32).astype(BF16)


def _project(x, shift, scl, w, w_zb, layer, tm=512):
    b, s, d = x.shape
    dils = tuple(dl for _, dl in DILATED)
    bs = lambda n: pl.BlockSpec((1, tm, n), lambda i, j: (i, j, 0))
    gs = pl.BlockSpec((1, G_B, tm // CMP_STRIDE, CMP_STRIDE * HEAD_DIM), lambda i, j: (i, 0, j, 0))
    sd = lambda n, dt=BF16: jax.ShapeDtypeStruct((b, s, n), dt)
    gd = jax.ShapeDtypeStruct((b, G_B, s // CMP_STRIDE, CMP_STRIDE * HEAD_DIM), BF16)
    mod = pl.BlockSpec((1, 1, d), lambda i, j: (i, 0, 0))
    lay_specs = [bs(D_A)] + [pl.BlockSpec((1, dl, tm // dl, D_A), lambda i, j: (i, 0, j, 0)) for dl in dils[1:]]
    lay_shapes = [sd(D_A)] + [jax.ShapeDtypeStruct((b, dl, s // dl, D_A), BF16) for dl in dils[1:]]
    outs = pl.pallas_call(
        functools.partial(_proj_kernel, dils=dils),
        grid=(b, s // tm),
        in_specs=[pl.BlockSpec((1, tm, d), lambda i, j: (i, j, 0)), mod, mod,
                  pl.BlockSpec((None, d, w.shape[-1]), lambda i, j: (layer, 0, 0)),
                  pl.BlockSpec((None, d, D_B), lambda i, j: (layer, 0, 0))],
        out_specs=lay_specs * 3 + [bs(D_A), bs(D_B), gs, gs,
                                   bs(2 * LANES), bs(2 * LANES), bs(2 * LANES), bs(2 * LANES), bs(D_B), bs(LANES)],
        out_shape=lay_shapes * 3 + [sd(D_A), sd(D_B), gd, gd,
                                    sd(2 * LANES), sd(2 * LANES), sd(2 * LANES), sd(2 * LANES), sd(D_B), sd(LANES, F32)],
        scratch_shapes=[pltpu.VMEM((tm, d), BF16), pltpu.VMEM((2, D_A // LANES, tm, LANES), F32)],
        compiler_params=_cparams(("parallel", "parallel")),
        name="in_proj",
    )(x, shift, scl, w, w_zb)
    n = len(dils)
    return (outs[:n], outs[n:2 * n], outs[2 * n:3 * n], *outs[3 * n:])


def _dilated_kernel(*refs, seq, unroll):
    n_pat = len(DILATED)
    qkv = refs[:3 * n_pat]
    t_refs = refs[3 * n_pat:4 * n_pat]
    o_ref = refs[4 * n_pat]
    qlo, qhi, kp, vp, o_r, m_r, l_r, o_t, m_t, l_t, m_acc, n_acc, d_acc = refs[4 * n_pat + 1:]
    lo = _lo_mask()
    nblk = seq // BLK
    chunk = 512
    kp[:BLK, :] = jnp.zeros((BLK, LANES), BF16)
    vp[:BLK, :] = jnp.zeros((BLK, LANES), BF16)

    for pi, ((_, dil), t_ref) in enumerate(zip(DILATED, t_refs)):
        q_ref, k_ref, v_ref = qkv[pi], qkv[n_pat + pi], qkv[2 * n_pat + pi]
        rl = seq // dil
        nb = rl // BLK
        for r in range(dil):
            idx = (0,) if dil == 1 else (0, r)
            q = q_ref[idx]
            zero = jnp.zeros_like(q)
            qlo[r * rl:(r + 1) * rl, :] = jnp.where(lo, q, zero)
            qhi[r * rl:(r + 1) * rl, :] = jnp.where(lo, zero, q)
            kp[BLK + r * rl:BLK + (r + 1) * rl, :] = k_ref[idx]
            vp[BLK + r * rl:BLK + (r + 1) * rl, :] = v_ref[idx]
        o_dst, m_dst, l_dst = (n_acc, m_acc, d_acc) if dil == 1 else (o_r, m_r, l_r)

        def blocks(it, carry, t_ref=t_ref, nb=nb, o_dst=o_dst, m_dst=m_dst, l_dst=l_dst):
            for u in range(unroll):
                g = it * unroll + u
                first = jnp.where((g & (nb - 1)) == 0, 1, 0)
                q0 = pl.multiple_of(g * BLK, BLK)
                qs = jnp.concatenate([qlo[pl.ds(q0, BLK), :], qhi[pl.ds(q0, BLK), :]], axis=0)
                s = _dot_nt(qs, kp[pl.ds(q0, 2 * BLK), :]) + t_ref[0, first]
                m = jnp.max(s, axis=-1, keepdims=True)
                p = jnp.exp2(s - m)
                l = jnp.sum(p, axis=-1, keepdims=True)
                pv = jnp.dot(p.astype(BF16), vp[pl.ds(q0, 2 * BLK), :], preferred_element_type=F32)
                o_dst[pl.ds(q0, BLK), :] = jnp.where(lo, pv[:BLK], pv[BLK:])
                m_dst[pl.ds(q0, BLK), :] = jnp.where(lo, m[:BLK], m[BLK:])
                l_dst[pl.ds(q0, BLK), :] = jnp.where(lo, l[:BLK], l[BLK:])
            return carry

        lax.fori_loop(0, nblk // unroll, blocks, 0)
        if dil == 1:
            continue

        for r in range(dil):
            for src, dst in ((o_r, o_t), (m_r, m_t), (l_r, l_t)):
                dst[pl.ds(r, rl, stride=dil), :] = src[r * rl:(r + 1) * rl, :]

        def fold(c, carry, last=(pi == n_pat - 1)):
            rows = pl.ds(pl.multiple_of(c * chunk, chunk), chunk)
            m_old, m_new = m_acc[rows, :], m_t[rows, :]
            mx = jnp.maximum(m_old, m_new)
            a, b_ = jnp.exp2(m_old - mx), jnp.exp2(m_new - mx)
            n_new = a * n_acc[rows, :] + b_ * o_t[rows, :]
            d_new = a * d_acc[rows, :] + b_ * l_t[rows, :]
            if last:
                o_ref[0, rows, :] = (n_new * (1.0 / d_new)).astype(o_ref.dtype)
            else:
                m_acc[rows, :] = mx
                n_acc[rows, :] = n_new
                d_acc[rows, :] = d_new
            return carry

        lax.fori_loop(0, seq // chunk, fold, 0)


def _dilated_attention(qa, ka, va, tables, unroll=32):
    assert DILATED[0][1] == 1 and all(dl > 1 for _, dl in DILATED[1:])
    b, s, c = qa[0].shape
    xs = pl.BlockSpec((1, s, LANES), lambda i, j: (i, 0, j))
    lay = [xs] + [pl.BlockSpec((1, dl, s // dl, LANES), lambda i, j: (i, 0, 0, j)) for _, dl in DILATED[1:]]
    ts = pl.BlockSpec((1, 2, 2 * BLK, 2 * BLK), lambda i, j: (j, 0, 0, 0))
    big = lambda rows, dt: pltpu.VMEM((rows, LANES), dt)
    return pl.pallas_call(
        functools.partial(_dilated_kernel, seq=s, unroll=unroll),
        grid=(b, c // LANES),
        in_specs=lay * 3 + [ts] * len(DILATED),
        out_specs=xs,
        out_shape=jax.ShapeDtypeStruct((b, s, c), BF16),
        scratch_shapes=[big(s, BF16)] * 2 + [big(s + BLK, BF16)] * 2 + [big(s, F32)] * 9,
        compiler_params=_cparams(("parallel", "parallel")),
        name="dilated_attn",
    )(*qa, *ka, *va, *tables)


def _win_kernel(q_ref, k_ref, v_ref, t_ref, o_ref, qst, kaug, vp, *, seq, n_prev, unroll):
    lo = _lo_mask()
    lane = lax.broadcasted_iota(jnp.int32, (1, LANES), 1)
    flag = lane == HEAD_DIM
    pad = n_prev * BLK
    span = (n_prev + 1) * BLK
    negflag = jnp.where(flag, NEG, 0.0)
    for pr in range(R_B // 2):
        q = q_ref[0, :, pr * LANES:(pr + 1) * LANES].astype(F32)
        qst[2 * pr] = jnp.where(lo, q, negflag).astype(BF16)
        qst[2 * pr + 1] = jnp.where(lo, pltpu.roll(q, HEAD_DIM, 1), negflag).astype(BF16)
    k = k_ref[0]
    kaug[:pad, :] = jnp.broadcast_to(jnp.where(flag, 1.0, 0.0), (pad, LANES)).astype(BF16)
    kaug[pad:, :] = jnp.where(lo, k, jnp.zeros_like(k))
    vp[:pad, :] = jnp.zeros((pad, LANES), BF16)
    vp[pad:, :] = jnp.where(lo, v_ref[0], jnp.ones((), BF16))

    def blocks(it, carry):
        for u in range(unroll):
            n = it * unroll + u
            r0 = pl.multiple_of(n * BLK, BLK)
            qs = jnp.concatenate([qst[h, pl.ds(r0, BLK), :] for h in range(R_B)], axis=0)
            s = _dot_nt(qs, kaug[pl.ds(r0, span), :]) + t_ref[0]
            m = jnp.max(s, axis=-1, keepdims=True)
            p = jnp.exp2(s - m)
            pv = jnp.dot(p.astype(BF16), vp[pl.ds(r0, span), :], preferred_element_type=F32)
            for pr in range(R_B // 2):
                o = _normalise_pair(pv[2 * pr * BLK:(2 * pr + 1) * BLK], pv[(2 * pr + 1) * BLK:(2 * pr + 2) * BLK])
                o_ref[0, pl.ds(r0, BLK), pr * LANES:(pr + 1) * LANES] = o.astype(o_ref.dtype)
        return carry

    lax.fori_loop(0, seq // BLK // unroll, blocks, 0)


def _win_attention(qb, kw_dup, vw_dup, table, n_prev, unroll=8):
    b, s, _ = qb.shape
    gw = R_B * HEAD_DIM
    w = table.shape[-1]
    return pl.pallas_call(
        functools.partial(_win_kernel, seq=s, n_prev=n_prev, unroll=unroll),
        grid=(b, G_B),
        in_specs=[pl.BlockSpec((1, s, gw), lambda i, g: (i, 0, g)),
                  pl.BlockSpec((1, s, LANES), lambda i, g: (i, 0, g)),
                  pl.BlockSpec((1, s, LANES), lambda i, g: (i, 0, g)),
                  pl.BlockSpec((1, R_B * BLK, w), lambda i, g: (g, 0, 0))],
        out_specs=pl.BlockSpec((1, s, gw), lambda i, g: (i, 0, g)),
        out_shape=jax.ShapeDtypeStruct((b, s, D_B), BF16),
        scratch_shapes=[pltpu.VMEM((R_B, s, LANES), BF16)] + [pltpu.VMEM((s + n_prev * BLK, LANES), BF16)] * 2,
        compiler_params=_cparams(("parallel", "parallel")),
        name="nsa_window",
    )(qb, kw_dup, vw_dup, table)


def _compress_kernel(kin, vin, w1, w2, pe, kout, vout):
    n_chunk = kin.shape[2]
    half = w1.shape[1] // 2
    rows = lax.broadcasted_iota(jnp.int32, (n_chunk, 1), 0)
    for t, (xin, out) in enumerate(((kin, kout), (vin, vout))):
        x = xin[0, 0]
        a = jnp.dot(x, w1[t, :half], preferred_element_type=F32)
        b = jnp.dot(x, w1[t, half:], preferred_element_type=F32)
        c = jnp.dot(pe[t], w1[t], preferred_element_type=F32)[0:1]
        hid = a + pltpu.roll(b, n_chunk - 1, 0) + c
        act = hid * jax.nn.sigmoid(hid)
        o = jnp.dot(act.astype(BF16), w2[t], preferred_element_type=F32)
        out[0, 0] = jnp.where(rows < n_chunk - 1, o, 0.0).astype(BF16)


def _compress(kc_in, vc_in, w1, w2d, pe, layer):
    b, g, n_chunk, f = kc_in.shape
    xs = pl.BlockSpec((1, 1, n_chunk, f), lambda i, j: (i, j, 0, 0))
    os_ = pl.BlockSpec((1, 1, n_chunk, LANES), lambda i, j: (i, j, 0, 0))
    full = lambda a: pl.BlockSpec((None,) + a.shape[1:], lambda i, j: (layer,) + (0,) * (a.ndim - 1))
    od = jax.ShapeDtypeStruct((b, g, n_chunk, LANES), BF16)
    return pl.pallas_call(
        _compress_kernel,
        grid=(b, g),
        in_specs=[xs, xs, full(w1), full(w2d), full(pe)],
        out_specs=[os_, os_],
        out_shape=[od, od],
        compiler_params=_cparams(("parallel", "parallel")),
        name="nsa_compress",
    )(kc_in, vc_in, w1, w2d, pe)


def _cmp_kernel(q_ref, kc_ref, vc_ref, t_ref, ov_ref, o_ref, sb_ref, *, tq):
    i = pl.program_id(2)
    lo = _lo_mask()
    n_col, n_slc = kc_ref.shape[2], ov_ref.shape[0]

    def attend_and_rank(ncol, nsel):
        kcd = kc_ref[0, 0, :ncol, :]
        vcd = vc_ref[0, 0, :ncol, :]
        zero = jnp.zeros_like(vcd)
        vlo = jnp.where(lo, vcd, zero)
        vhi = jnp.where(lo, zero, vcd)
        psum = None
        for pr in range(R_B // 2):
            q = q_ref[0, :, pr * LANES:(pr + 1) * LANES]
            qz = jnp.zeros_like(q)
            acc = None
            for hh, (qm, vm) in enumerate(((jnp.where(lo, q, qz), vlo), (jnp.where(lo, qz, q), vhi))):
                s = _dot_nt(qm, kcd) + t_ref[2 * pr + hh, :, :ncol]
                m = jnp.max(s, axis=-1, keepdims=True)
                e = jnp.exp2(s - m)
                l = jnp.sum(e, axis=-1, keepdims=True)
                p = e * jnp.where(m > 0.5 * NEG, 1.0 / l, 0.0)
                psum = p if psum is None else psum + p
                pv = jnp.dot(p.astype(BF16), vm, preferred_element_type=F32)
                acc = pv if acc is None else acc + pv
            o_ref[0, :, pr * LANES:(pr + 1) * LANES] = acc.astype(o_ref.dtype)

        p_hi = psum.astype(BF16)
        p_lo = (psum - p_hi.astype(F32)).astype(BF16)
        ov = ov_ref[:nsel, :ncol]
        score = (_dot_nt(ov, p_hi) + _dot_nt(ov, p_lo))
        blk = lax.broadcasted_iota(jnp.int32, (nsel, 1), 0)
        pos = i * tq + lax.broadcasted_iota(jnp.int32, (1, tq), 1)
        cur = pos >> int(math.log2(SLC_BLOCK))
        forced = (blk == 0) | ((cur - blk >= 0) & (cur - blk < SLC_LOCAL))
        score = jnp.where(forced, 1e9, jnp.where(blk > cur, -1e9, score))
        sub = lax.broadcasted_iota(jnp.int32, (8, 1), 0)
        sel_rows = []
        for g8 in range(nsel // 8):
            sg = score[g8 * 8:(g8 + 1) * 8]
            rank = jnp.zeros(sg.shape, F32)
            for mp in range(nsel):
                row = score[mp:mp + 1]
                if mp < g8 * 8:
                    ahead = jnp.where(row >= sg, 1.0, 0.0)
                elif mp >= (g8 + 1) * 8:
                    ahead = jnp.where(row > sg, 1.0, 0.0)
                else:
                    ahead = jnp.where(sub > mp - g8 * 8, jnp.where(row >= sg, 1.0, 0.0), jnp.where(row > sg, 1.0, 0.0))
                rank = rank + ahead
            sel_rows.append(jnp.where(rank < float(min(SLC_TOPK, n_slc)), 0.0, NEG))
        if nsel < n_slc:
            sel_rows.append(jnp.full((n_slc - nsel, tq), NEG, F32))
        pad = LANES // n_slc
        selb = jnp.concatenate(sel_rows * pad, axis=0)
        sb_ref[0, 0] = selb.T.astype(BF16)

    n_tile = pl.num_programs(2)
    for k in range(4):
        ncol = max(LANES, -(-(n_col * (k + 1) // 4) // LANES) * LANES)
        in_quarter = (i * 4 >= k * n_tile) & (i * 4 < (k + 1) * n_tile)
        pl.when(in_quarter)(functools.partial(attend_and_rank, ncol, n_slc * (k + 1) // 4))


def _cmp_attention(qb, kcd, vcd, table, ov_t, tq=512):
    b, s, _ = qb.shape
    n_col = kcd.shape[2]
    gw = R_B * HEAD_DIM
    return pl.pallas_call(
        functools.partial(_cmp_kernel, tq=tq),
        grid=(b, G_B, s // tq),
        in_specs=[pl.BlockSpec((1, tq, gw), lambda i, g, j: (i, j, g)),
                  pl.BlockSpec((1, 1, n_col, LANES), lambda i, g, j: (i, g, 0, 0)),
                  pl.BlockSpec((1, 1, n_col, LANES), lambda i, g, j: (i, g, 0, 0)),
                  pl.BlockSpec((R_B, tq, n_col), lambda i, g, j: (g, j, 0)),
                  pl.BlockSpec(ov_t.shape, lambda i, g, j: (0, 0))],
        out_specs=[pl.BlockSpec((1, tq, gw), lambda i, g, j: (i, j, g)),
                   pl.BlockSpec((1, 1, tq, LANES), lambda i, g, j: (i, g, j, 0))],
        out_shape=[jax.ShapeDtypeStruct((b, s, D_B), BF16),
                   jax.ShapeDtypeStruct((b, G_B, s, LANES), BF16)],
        compiler_params=_cparams(("parallel", "parallel", "parallel")),
        name="nsa_cmp_topk",
    )(qb, kcd, vcd, table, ov_t)


def _sel_kernel(q_ref, sb_ref, k_ref, v_ref, t_ref, o_ref, kaug, vp, m_ref, acc_ref, *, nblk):
    lo = _lo_mask()
    s_len = nblk * BLK
    n_diag = t_ref.shape[1] - 1
    h0 = pl.program_id(1) * R_B
    rowblk = lax.broadcasted_iota(jnp.int32, (s_len, LANES), 0) >> int(math.log2(SLC_BLOCK))
    lane = lax.broadcasted_iota(jnp.int32, (s_len, LANES), 1)
    onehot = jnp.where(lane - HEAD_DIM == rowblk, 1.0, 0.0).astype(BF16)
    kaug[...] = jnp.where(lo, k_ref[0], onehot)
    vp[...] = jnp.where(lo, v_ref[0], jnp.ones((), BF16))
    m_ref[...] = jnp.full(m_ref.shape, NEG, F32)
    acc_ref[...] = jnp.zeros(acc_ref.shape, F32)
    kb = 4
    qb = 2
    sub = kb // qb
    kt_rows, qt_rows = kb * BLK, qb * BLK

    def ktile(t, carry):
        k0 = pl.multiple_of(t * kt_rows, kt_rows)
        kt = kaug[pl.ds(k0, kt_rows), :]
        vt = vp[pl.ds(k0, kt_rows), :]

        def qstep(mp, c):
            scores = []
            for u in range(sub):
                r0 = pl.multiple_of((sub * mp + u) * qt_rows, qt_rows)
                sbf = sb_ref[0, 0, pl.ds(r0, qt_rows), :].astype(F32)
                rows = []
                for pr in range(R_B // 2):
                    q = q_ref[0, pl.ds(r0, qt_rows), pr * LANES:(pr + 1) * LANES].astype(F32)
                    rows.append(jnp.where(lo, q, sbf))
                    rows.append(jnp.where(lo, pltpu.roll(q, HEAD_DIM, 1), sbf))
                qst = jnp.concatenate(rows, axis=0).astype(BF16)
                bias = jnp.concatenate([
                    jnp.concatenate([t_ref[h0 + h, jnp.clip(qb * (sub * mp + u) + a - kb * t - c + 1, 0, n_diag)]
                                     for c in range(kb)], axis=1)
                    for h in range(R_B) for a in range(qb)], axis=0)
                scores.append(_dot_nt(qst, kt) + bias)
            for u in range(sub):
                r0 = pl.multiple_of((sub * mp + u) * qt_rows, qt_rows)
                s = scores[u]
                ps, alphas = [], []
                for h in range(R_B):
                    for a in range(qb):
                        lo_r = h * qt_rows + a * BLK
                        st = pl.ds(r0 + a * BLK, BLK)
                        sh = s[lo_r:lo_r + BLK]
                        m_prev = m_ref[h, st, :]
                        m_new = jnp.maximum(m_prev, jnp.max(sh, axis=-1, keepdims=True))
                        alpha = jnp.exp2(m_prev - m_new)
                        p = jnp.exp2(sh - jnp.concatenate([m_new] * kb, axis=1))
                        m_ref[h, st, :] = m_new
                        ps.append(p.astype(BF16))
                        alphas.append(alpha)
                pv = jnp.dot(jnp.concatenate(ps, axis=0), vt, preferred_element_type=F32)
                for h in range(R_B):
                    for a in range(qb):
                        lo_r = h * qt_rows + a * BLK
                        st = pl.ds(r0 + a * BLK, BLK)
                        acc_ref[h, st, :] = alphas[qb * h + a] * acc_ref[h, st, :] + pv[lo_r:lo_r + BLK]
            return c

        lax.fori_loop(t, nblk // kb, qstep, 0)
        return carry

    lax.fori_loop(0, nblk // kb, ktile, 0)

    def finish(n, carry):
        st = pl.ds(pl.multiple_of(n * BLK, BLK), BLK)
        for pr in range(R_B // 2):
            o = _normalise_pair(acc_ref[2 * pr, st, :], acc_ref[2 * pr + 1, st, :])
            o_ref[0, st, pr * LANES:(pr + 1) * LANES] = o.astype(o_ref.dtype)
        return carry

    lax.fori_loop(0, nblk, finish, 0, unroll=8)


def _sel_attention(qb, selb, ks_dup, vs_dup, table):
    b, s, _ = qb.shape
    gw = R_B * HEAD_DIM
    return pl.pallas_call(
        functools.partial(_sel_kernel, nblk=s // BLK),
        grid=(b, G_B),
        in_specs=[pl.BlockSpec((1, s, gw), lambda i, g: (i, 0, g)),
                  pl.BlockSpec((1, 1, s, LANES), lambda i, g: (i, g, 0, 0)),
                  pl.BlockSpec((1, s, LANES), lambda i, g: (i, 0, g)),
                  pl.BlockSpec((1, s, LANES), lambda i, g: (i, 0, g)),
                  pl.BlockSpec(table.shape, lambda i, g: (0, 0, 0, 0), pipeline_mode=pl.Buffered(1))],
        out_specs=pl.BlockSpec((1, s, gw), lambda i, g: (i, 0, g)),
        out_shape=jax.ShapeDtypeStruct((b, s, D_B), BF16),
        scratch_shapes=[pltpu.VMEM((s, LANES), BF16)] * 2 + [pltpu.VMEM((R_B, s, LANES), F32)] * 2,
        compiler_params=_cparams(("parallel", "parallel")),
        name="nsa_selected",
    )(qb, selb, ks_dup, vs_dup, table)


def _combine_kernel(x_ref, oa, za, oc, osl, ow, gb, zb, wo, ex, gate, lng, lnb, out_ref, *, alpha):
    f = lambda r: r[0].astype(F32)
    z = f(za)
    mix_a = f(oa) * (z * jax.nn.sigmoid(z))
    g = jax.nn.sigmoid(gb[0])
    g_hi = g.astype(BF16)
    g_lo = (g - g_hi.astype(F32)).astype(BF16)
    gx = (jnp.dot(g_hi, ex[...], preferred_element_type=F32)
          + jnp.dot(g_lo, ex[...], preferred_element_type=F32))
    ob = gx[:, :D_B] * f(oc) + gx[:, D_B:2 * D_B] * f(osl) + gx[:, 2 * D_B:] * f(ow)
    z = f(zb)
    mix_b = ob * (z * jax.nn.sigmoid(z))
    y = (jnp.dot(mix_a.astype(BF16), wo[:D_A], preferred_element_type=F32)
         + jnp.dot(mix_b.astype(BF16), wo[D_A:], preferred_element_type=F32))
    r = alpha * x_ref[0] + (1.0 + gate[0]) * y
    mu = jnp.mean(r, axis=-1, keepdims=True)
    rc = r - mu
    var = jnp.mean(rc * rc, axis=-1, keepdims=True)
    out_ref[0] = rc * lax.rsqrt(var + LN_EPS) * lng[...] + lnb[...]


def _combine(x, o_a, za, o_cmp, o_slc, o_win, gb, zb, w_out, expand, gate, ln_g, ln_b, alpha, layer, tm=512):
    b, s, d = x.shape
    bs = lambda n: pl.BlockSpec((1, tm, n), lambda i, j: (i, j, 0))
    full = lambda a: pl.BlockSpec(a.shape, lambda i, j: (0,) * a.ndim)
    return pl.pallas_call(
        functools.partial(_combine_kernel, alpha=alpha),
        grid=(b, s // tm),
        in_specs=[bs(d)] + [bs(D_A)] * 2 + [bs(D_B)] * 3 + [bs(LANES), bs(D_B),
                  pl.BlockSpec((None,) + w_out.shape[1:], lambda i, j: (layer, 0, 0)), full(expand),
                  pl.BlockSpec((1, 1, d), lambda i, j: (i, 0, 0)), full(ln_g), full(ln_b)],
        out_specs=bs(d),
        out_shape=jax.ShapeDtypeStruct((b, s, d), F32),
        compiler_params=_cparams(("parallel", "parallel")),
        name="merge_out_proj_ln",
    )(x, o_a, za, o_cmp, o_slc, o_win, gb, zb, w_out, expand, gate, ln_g, ln_b)


def _gate_expand():
    e = np.zeros((LANES, 3 * D_B), np.float32)
    for i in range(3):
        for h in range(H_B):
            e[i * H_B + h, i * D_B + h * HEAD_DIM:i * D_B + (h + 1) * HEAD_DIM] = 1.0
    return jnp.asarray(e, BF16)


def _overlap_t(n_col, n_slc):
    cs = np.arange(n_col)[None, :] * CMP_STRIDE
    ss = np.arange(n_slc)[:, None] * SLC_BLOCK
    return jnp.asarray(((cs < ss + SLC_BLOCK) & (cs + CMP_LEN > ss)).astype(np.float32), BF16)


def _layer(x, shift, scl, gate, w_in_b, w_zb, w_out_b, pe_b, w1_b, w2d_b, ln_g, ln_b, tables, consts, alpha, layer):
    b, s, d = x.shape
    (qa, ka, va, za, qb, kc_in, vc_in, ks, vs, kw, vw, zb, gb) = _project(x, shift, scl, w_in_b, w_zb, layer)

    o_a = _dilated_attention(qa, ka, va, tables["dilated"])

    kcd, vcd = _compress(kc_in, vc_in, w1_b, w2d_b, pe_b, layer)
    o_cmp, selb = _cmp_attention(qb, kcd, vcd, tables["cmp"], consts["overlap_t"])
    o_slc = _sel_attention(qb, selb, ks, vs, tables["sel"])
    o_win = _win_attention(qb, kw, vw, tables["win"], n_prev=-(-(WIN - 1) // BLK))

    return _combine(x, o_a, za, o_cmp, o_slc, o_win, gb, zb, w_out_b, consts["expand"], gate,
                    ln_g, ln_b, alpha, layer)


def kernel(x, c, w_in, w_out, cmp_pe, cmp_w1, cmp_w2, w_ada, b_ada, ln_g, ln_b, rel_bias):
    b, s, d = x.shape
    depth = w_in.shape[0]
    alpha = (2 * depth) ** 0.25
    n_pair = D_A // LANES

    bb = rel_bias.astype(F32) * LOG2E
    def dilated_table(window, dil):
        t = _band_table(bb[:, :H_A], 1, window // dil, dil).reshape(n_pair, 2 * BLK, 2 * BLK)
        first = t.at[..., :BLK].set(NEG)
        return jnp.stack([t, first], axis=1)

    n_prev_win = -(-(WIN - 1) // BLK)
    tables = {
        "dilated": [dilated_table(w, dl) for w, dl in DILATED],
        "win": _band_table(bb[:, H_A:], n_prev_win, WIN - 1, 1).reshape(G_B, R_B * BLK, (n_prev_win + 1) * BLK),
        "cmp": _cmp_table(bb[:, H_A:], s),
        "sel": _diag_table(bb[:, H_A:], 13),
    }
    consts = {"expand": _gate_expand(), "overlap_t": _overlap_t(s // CMP_STRIDE, s // SLC_BLOCK)}

    mod = _ada_mod(c, w_ada, b_ada)
    w_in_b = w_in.astype(BF16)
    w_zb = w_in[..., _C_ZB:].astype(BF16)
    w_out_b = w_out.astype(BF16)
    w1_b = cmp_w1.astype(BF16)
    w2d_b = jnp.concatenate([cmp_w2, cmp_w2], axis=-1).astype(BF16)
    pe_b = jnp.broadcast_to(cmp_pe.reshape(depth, 2, 1, CMP_LEN * HEAD_DIM), (depth, 2, 8, CMP_LEN * HEAD_DIM)).astype(BF16)

    for l in range(depth):
        m = mod[l, :b]
        shift, scl, gate = (m[:, i * d:(i + 1) * d].reshape(b, 1, d) for i in range(3))
        x = _layer(x, shift, scl, gate, w_in_b, w_zb, w_out_b, pe_b, w1_b, w2d_b,
                   ln_g[l].reshape(1, d), ln_b[l].reshape(1, d), tables, consts, alpha, l)
    return x
```

```python
import functools
import math

import numpy as np
import jax
import jax.numpy as jnp
from jax import lax
from jax.experimental import pallas as pl
from jax.experimental.pallas import tpu as pltpu

F32 = jnp.float32
BF16 = jnp.bfloat16

D_MODEL = 1024
HEAD_DIM = 64
H_A = 8
H_B = 8
G_B = 2
R_B = H_B // G_B
D_A = H_A * HEAD_DIM
D_B = H_B * HEAD_DIM
DILATED = ((128, 1), (512, 4), (2048, 16))
CMP_LEN = 32
CMP_STRIDE = 16
CMP_HIDDEN = 256
SLC_BLOCK = 64
SLC_TOPK = 16
SLC_LOCAL = 2
WIN = 512
N_BUCKETS = 32
MAX_DIST = 2048
LN_EPS = 1e-5
NEG = -1e30
LOG2E = 1.4426950408889634

LANES = 128
BLK = 128
VMEM_BYTES_V7X = 64 * 1024 * 1024
VMEM_LIMIT = VMEM_BYTES_V7X - 8 * 1024 * 1024

_C_QA, _C_KA, _C_VA, _C_ZA, _C_QB = 0, 512, 1024, 1536, 2048
_C_NARROW = 2560
_C_ZB = 3328 + 3 * H_B


def _cparams(sem):
    return pltpu.CompilerParams(dimension_semantics=sem, vmem_limit_bytes=VMEM_LIMIT)


def _lo_mask():
    return lax.broadcasted_iota(jnp.int32, (1, LANES), 1) < HEAD_DIM


def _normalise_pair(even, odd):
    lo = _lo_mask()
    straight = jnp.where(lo, even, odd)
    swapped = pltpu.roll(jnp.where(lo, odd, even), HEAD_DIM, 1)
    return jnp.where(lo, straight, swapped) * (1.0 / jnp.where(lo, swapped, straight))


def _dot_nt(a, b):
    return lax.dot_general(a, b, (((1,), (1,)), ((), ())), preferred_element_type=F32)


def _bucket_np(dist):
    d = np.maximum(dist, 0)
    max_exact = N_BUCKETS // 2
    large = max_exact + (np.log(np.maximum(d, 1).astype(np.float32) / np.float32(max_exact))
                         / np.float32(math.log(MAX_DIST / max_exact))
                         * np.float32(N_BUCKETS - max_exact)).astype(np.int32)
    return np.where(d < max_exact, d, np.minimum(large, N_BUCKETS - 1)).astype(np.int32)


def _masked_bias(bias_by_bucket, dist, valid):
    onehot = np.eye(N_BUCKETS, dtype=np.float32)[_bucket_np(dist)]
    t = jnp.einsum("...k,kh->h...", onehot, bias_by_bucket, precision=lax.Precision.HIGHEST)
    return jnp.where(jnp.asarray(valid)[None], t, NEG)


def _toeplitz_kernel(v_ref, o_ref):
    cols = o_ref.shape[-1]
    for e in range(o_ref.shape[1]):
        x = jnp.broadcast_to(v_ref[0, e:e + 1, :], (BLK, v_ref.shape[-1]))
        o_ref[0, e] = pltpu.roll(x, 0, 1, stride=1, stride_axis=0)[:, :cols]


def _toeplitz(h, cols):
    n_h, n_e, n = h.shape
    width = 1 << (n - 1).bit_length()
    vec = jnp.concatenate([h[..., BLK - 1:], jnp.zeros((n_h, n_e, width - n), F32), h[..., :BLK - 1]], axis=-1)
    return pl.pallas_call(
        _toeplitz_kernel,
        grid=(n_h,),
        in_specs=[pl.BlockSpec((1, n_e, width), lambda a: (a, 0, 0))],
        out_specs=pl.BlockSpec((1, n_e, BLK, cols), lambda a: (a, 0, 0, 0)),
        out_shape=jax.ShapeDtypeStruct((n_h, n_e, BLK, cols), F32),
        compiler_params=_cparams(("parallel",)),
        name="toeplitz_table",
    )(vec)


def _band_table(bias_by_bucket, n_prev, window, scale):
    w = (n_prev + 1) * BLK
    dist = n_prev * BLK - (np.arange(BLK - 1 + w) - (BLK - 1))
    h = _masked_bias(bias_by_bucket, dist * scale, (dist >= 0) & (dist <= window))
    return _toeplitz(h[:, None], w)[:, 0]


def _diag_table(bias_by_bucket, n_diag):
    delta = np.arange(-1, n_diag + 1)[:, None]
    dist = delta * BLK - (np.arange(2 * BLK - 1)[None, :] - (BLK - 1))
    return _toeplitz(_masked_bias(bias_by_bucket, dist, dist >= 0), BLK)


def _cmp_table_kernel(base_ref, o_ref, *, rows):
    i = pl.program_id(1)
    n_col = o_ref.shape[-1]
    width = base_ref.shape[-1]
    base = base_ref[0]
    for j in range(rows // CMP_STRIDE):
        chunk = i * (rows // CMP_STRIDE) + j
        shift = (chunk + width - (n_col - 1)) % width
        o_ref[0, j * CMP_STRIDE:(j + 1) * CMP_STRIDE, :] = pltpu.roll(base, shift, 1)[:, :n_col]


def _cmp_table(bias_by_bucket, seq, rows=1024):
    n_col = seq // CMP_STRIDE
    t = np.arange(2 * n_col) - (n_col - 1)
    dist = -CMP_STRIDE * t[None, :] + np.arange(CMP_STRIDE)[:, None] - (CMP_LEN - 1)
    base = _masked_bias(bias_by_bucket, dist, dist >= 0)
    h = base.shape[0]
    return pl.pallas_call(
        functools.partial(_cmp_table_kernel, rows=rows),
        grid=(h, seq // rows),
        in_specs=[pl.BlockSpec((1, CMP_STRIDE, 2 * n_col), lambda a, i: (a, 0, 0))],
        out_specs=pl.BlockSpec((1, rows, n_col), lambda a, i: (a, i, 0)),
        out_shape=jax.ShapeDtypeStruct((h, seq, n_col), F32),
        compiler_params=_cparams(("parallel", "parallel")),
        name="cmp_bias_table",
    )(base)


def _ada_kernel(c_ref, w_ref, b_ref, o_ref):
    o_ref[0] = jnp.dot(c_ref[...], w_ref[0].astype(BF16), preferred_element_type=F32) + b_ref[0]


def _ada_mod(c, w_ada, b_ada):
    depth, d, d3 = w_ada.shape
    cp = jnp.zeros((8, d), BF16).at[:c.shape[0]].set(c.astype(BF16))
    nj = d3 // d
    return pl.pallas_call(
        _ada_kernel,
        grid=(depth, nj),
        in_specs=[pl.BlockSpec((8, d), lambda l, j: (0, 0)),
                  pl.BlockSpec((1, d, d), lambda l, j: (l, 0, j)),
                  pl.BlockSpec((1, 1, d), lambda l, j: (l, 0, j))],
        out_specs=pl.BlockSpec((1, 8, d), lambda l, j: (l, 0, j)),
        out_shape=jax.ShapeDtypeStruct((depth, 8, d3), F32),
        compiler_params=_cparams(("parallel", "parallel")),
        name="ada_mod",
    )(cp, w_ada, b_ada.reshape(depth, 1, d3))


def _proj_kernel(x_ref, sh_ref, sc_ref, w_ref, wz_ref, *refs, dils):
    n_lay = len(dils)
    qa, ka, va = refs[:n_lay], refs[n_lay:2 * n_lay], refs[2 * n_lay:3 * n_lay]
    za, qb, kc, vc, ks, vs, kw, vw, zb, gb, h_ref, xs = refs[3 * n_lay:]
    tm = h_ref.shape[0]
    h_ref[...] = (x_ref[0] * (1.0 + sc_ref[0]) + sh_ref[0]).astype(BF16)

    def mm(c0, n):
        return jnp.dot(h_ref[...], w_ref[:, c0:c0 + n], preferred_element_type=F32)

    def emit_regrouped(a, outs):
        outs[0][0] = a.astype(BF16)
        n_slab = a.shape[1] // LANES
        for c in range(n_slab):
            xs[0, c] = a[:, c * LANES:(c + 1) * LANES]
        d_prev = 1
        for lvl, (ref, dil) in enumerate(zip(outs[1:], dils[1:])):
            ratio, len_prev, len_new = dil // d_prev, tm // d_prev, tm // dil
            src, dst = xs.at[lvl % 2], xs.at[(lvl + 1) % 2]
            keep = lvl + 2 < len(dils)
            for rp in range(d_prev):
                for rs in range(ratio):
                    r = rp + d_prev * rs
                    for c in range(n_slab):
                        rows = src[c, pl.ds(rp * len_prev + rs, len_new, stride=ratio), :]
                        ref[0, r, :, c * LANES:(c + 1) * LANES] = rows.astype(BF16)
                        if keep:
                            dst[c, r * len_new:(r + 1) * len_new, :] = rows
            d_prev = dil

    qs = HEAD_DIM ** -0.5 * LOG2E
    emit_regrouped(mm(_C_QA, D_A) * qs, qa)
    emit_regrouped(mm(_C_KA, D_A), ka)
    emit_regrouped(mm(_C_VA, D_A), va)
    za[0] = mm(_C_ZA, D_A).astype(BF16)
    qb[0] = (mm(_C_QB, D_B) * qs).astype(BF16)
    lo = _lo_mask()
    narrow = mm(_C_NARROW, 7 * LANES)
    piece = lambda i: narrow[:, i * LANES:(i + 1) * LANES]
    for ref, i in ((kc, 0), (vc, 1)):
        xs[0, 0] = piece(i)
        for t in range(0, CMP_STRIDE, 2):
            x0 = xs[0, 0, pl.ds(t, tm // CMP_STRIDE, stride=CMP_STRIDE), :]
            x1 = xs[0, 0, pl.ds(t + 1, tm // CMP_STRIDE, stride=CMP_STRIDE), :]
            cols = slice(t * HEAD_DIM, (t + 2) * HEAD_DIM)
            ref[0, 0, :, cols] = jnp.where(lo, x0, pltpu.roll(x1, HEAD_DIM, 1)).astype(BF16)
            ref[0, 1, :, cols] = jnp.where(lo, pltpu.roll(x0, HEAD_DIM, 1), x1).astype(BF16)
    for ref, i in ((ks, 2), (vs, 3), (kw, 4), (vw, 5)):
        a = piece(i)
        r = pltpu.roll(a, HEAD_DIM, 1)
        ref[0, :, :LANES] = jnp.where(lo, a, r).astype(BF16)
        ref[0, :, LANES:] = jnp.where(lo, r, a).astype(BF16)
    gb[0] = piece(6)
    zb[0] = jnp.dot(h_ref[...], wz_ref[...], preferred_element_type=F32).astype(BF16)


def _project(x, shift, scl, w, w_zb, layer, tm=1024):
    b, s, d = x.shape
    dils = tuple(dl for _, dl in DILATED)
    bs = lambda n: pl.BlockSpec((1, tm, n), lambda i, j: (i, j, 0))
    gs = pl.BlockSpec((1, G_B, tm // CMP_STRIDE, CMP_STRIDE * HEAD_DIM), lambda i, j: (i, 0, j, 0))
    sd = lambda n, dt=BF16: jax.ShapeDtypeStruct((b, s, n), dt)
    gd = jax.ShapeDtypeStruct((b, G_B, s // CMP_STRIDE, CMP_STRIDE * HEAD_DIM), BF16)
    mod = pl.BlockSpec((1, 1, d), lambda i, j: (i, 0, 0))
    lay_specs = [bs(D_A)] + [pl.BlockSpec((1, dl, tm // dl, D_A), lambda i, j: (i, 0, j, 0)) for dl in dils[1:]]
    lay_shapes = [sd(D_A)] + [jax.ShapeDtypeStruct((b, dl, s // dl, D_A), BF16) for dl in dils[1:]]
    outs = pl.pallas_call(
        functools.partial(_proj_kernel, dils=dils),
        grid=(b, s // tm),
        in_specs=[pl.BlockSpec((1, tm, d), lambda i, j: (i, j, 0)), mod, mod,
                  pl.BlockSpec((None, d, w.shape[-1]), lambda i, j: (layer, 0, 0)),
                  pl.BlockSpec((None, d, D_B), lambda i, j: (layer, 0, 0))],
        out_specs=lay_specs * 3 + [bs(D_A), bs(D_B), gs, gs,
                                   bs(2 * LANES), bs(2 * LANES), bs(2 * LANES), bs(2 * LANES), bs(D_B), bs(LANES)],
        out_shape=lay_shapes * 3 + [sd(D_A), sd(D_B), gd, gd,
                                    sd(2 * LANES), sd(2 * LANES), sd(2 * LANES), sd(2 * LANES), sd(D_B), sd(LANES, F32)],
        scratch_shapes=[pltpu.VMEM((tm, d), BF16), pltpu.VMEM((2, D_A // LANES, tm, LANES), F32)],
        compiler_params=_cparams(("parallel", "parallel")),
        name="in_proj",
    )(x, shift, scl, w, w_zb)
    n = len(dils)
    return (outs[:n], outs[n:2 * n], outs[2 * n:3 * n], *outs[3 * n:])


def _dilated_kernel(*refs, seq, unroll):
    n_pat = len(DILATED)
    qkv = refs[:3 * n_pat]
    t_refs = refs[3 * n_pat:4 * n_pat]
    o_ref = refs[4 * n_pat]
    qlo, qhi, kp, vp, o_r, m_r, l_r, o_t, m_t, l_t, m_acc, n_acc, d_acc = refs[4 * n_pat + 1:]
    lo = _lo_mask()
    nblk = seq // BLK
    chunk = 512
    kp[:BLK, :] = jnp.zeros((BLK, LANES), BF16)
    vp[:BLK, :] = jnp.zeros((BLK, LANES), BF16)

    for pi, ((_, dil), t_ref) in enumerate(zip(DILATED, t_refs)):
        q_ref, k_ref, v_ref = qkv[pi], qkv[n_pat + pi], qkv[2 * n_pat + pi]
        rl = seq // dil
        nb = rl // BLK
        for r in range(dil):
            idx = (0,) if dil == 1 else (0, r)
            q = q_ref[idx]
            zero = jnp.zeros_like(q)
            qlo[r * rl:(r + 1) * rl, :] = jnp.where(lo, q, zero)
            qhi[r * rl:(r + 1) * rl, :] = jnp.where(lo, zero, q)
            kp[BLK + r * rl:BLK + (r + 1) * rl, :] = k_ref[idx]
            vp[BLK + r * rl:BLK + (r + 1) * rl, :] = v_ref[idx]
        o_dst, m_dst, l_dst = (n_acc, m_acc, d_acc) if dil == 1 else (o_r, m_r, l_r)

        def blocks(it, carry, t_ref=t_ref, nb=nb, o_dst=o_dst, m_dst=m_dst, l_dst=l_dst):
            for u in range(unroll):
                g = it * unroll + u
                first = jnp.where((g & (nb - 1)) == 0, 1, 0)
                q0 = pl.multiple_of(g * BLK, BLK)
                qs = jnp.concatenate([qlo[pl.ds(q0, BLK), :], qhi[pl.ds(q0, BLK), :]], axis=0)
                s = _dot_nt(qs, kp[pl.ds(q0, 2 * BLK), :]) + t_ref[0, first]
                m = jnp.max(s, axis=-1, keepdims=True)
                p = jnp.exp2(s - m)
                l = jnp.sum(p, axis=-1, keepdims=True)
                pv = jnp.dot(p.astype(BF16), vp[pl.ds(q0, 2 * BLK), :], preferred_element_type=F32)
                o_dst[pl.ds(q0, BLK), :] = jnp.where(lo, pv[:BLK], pv[BLK:])
                m_dst[pl.ds(q0, BLK), :] = jnp.where(lo, m[:BLK], m[BLK:])
                l_dst[pl.ds(q0, BLK), :] = jnp.where(lo, l[:BLK], l[BLK:])
            return carry

        lax.fori_loop(0, nblk // unroll, blocks, 0)
        if dil == 1:
            continue

        for r in range(dil):
            for src, dst in ((o_r, o_t), (m_r, m_t), (l_r, l_t)):
                dst[pl.ds(r, rl, stride=dil), :] = src[r * rl:(r + 1) * rl, :]

        def fold(c, carry, last=(pi == n_pat - 1)):
            rows = pl.ds(pl.multiple_of(c * chunk, chunk), chunk)
            m_old, m_new = m_acc[rows, :], m_t[rows, :]
            mx = jnp.maximum(m_old, m_new)
            a, b_ = jnp.exp2(m_old - mx), jnp.exp2(m_new - mx)
            n_new = a * n_acc[rows, :] + b_ * o_t[rows, :]
            d_new = a * d_acc[rows, :] + b_ * l_t[rows, :]
            if last:
                o_ref[0, rows, :] = (n_new * (1.0 / d_new)).astype(o_ref.dtype)
            else:
                m_acc[rows, :] = mx
                n_acc[rows, :] = n_new
                d_acc[rows, :] = d_new
            return carry

        lax.fori_loop(0, seq // chunk, fold, 0)


def _dilated_attention(qa, ka, va, tables, unroll=32):
    assert DILATED[0][1] == 1 and all(dl > 1 for _, dl in DILATED[1:])
    b, s, c = qa[0].shape
    xs = pl.BlockSpec((1, s, LANES), lambda i, j: (i, 0, j))
    lay = [xs] + [pl.BlockSpec((1, dl, s // dl, LANES), lambda i, j: (i, 0, 0, j)) for _, dl in DILATED[1:]]
    ts = pl.BlockSpec((1, 2, 2 * BLK, 2 * BLK), lambda i, j: (j, 0, 0, 0))
    big = lambda rows, dt: pltpu.VMEM((rows, LANES), dt)
    return pl.pallas_call(
        functools.partial(_dilated_kernel, seq=s, unroll=unroll),
        grid=(b, c // LANES),
        in_specs=lay * 3 + [ts] * len(DILATED),
        out_specs=xs,
        out_shape=jax.ShapeDtypeStruct((b, s, c), BF16),
        scratch_shapes=[big(s, BF16)] * 2 + [big(s + BLK, BF16)] * 2 + [big(s, F32)] * 9,
        compiler_params=_cparams(("parallel", "parallel")),
        name="dilated_attn",
    )(*qa, *ka, *va, *tables)


def _win_kernel(q_ref, k_ref, v_ref, t_ref, o_ref, qst, kaug, vp, *, seq, n_prev, unroll):
    lo = _lo_mask()
    lane = lax.broadcasted_iota(jnp.int32, (1, LANES), 1)
    flag = lane == HEAD_DIM
    pad = n_prev * BLK
    span = (n_prev + 1) * BLK
    negflag = jnp.where(flag, NEG, 0.0)
    for pr in range(R_B // 2):
        q = q_ref[0, :, pr * LANES:(pr + 1) * LANES].astype(F32)
        qst[2 * pr] = jnp.where(lo, q, negflag).astype(BF16)
        qst[2 * pr + 1] = jnp.where(lo, pltpu.roll(q, HEAD_DIM, 1), negflag).astype(BF16)
    k = k_ref[0]
    kaug[:pad, :] = jnp.broadcast_to(jnp.where(flag, 1.0, 0.0), (pad, LANES)).astype(BF16)
    kaug[pad:, :] = jnp.where(lo, k, jnp.zeros_like(k))
    vp[:pad, :] = jnp.zeros((pad, LANES), BF16)
    vp[pad:, :] = jnp.where(lo, v_ref[0], jnp.ones((), BF16))

    def blocks(it, carry):
        for u in range(unroll):
            n = it * unroll + u
            r0 = pl.multiple_of(n * BLK, BLK)
            qs = jnp.concatenate([qst[h, pl.ds(r0, BLK), :] for h in range(R_B)], axis=0)
            s = _dot_nt(qs, kaug[pl.ds(r0, span), :]) + t_ref[0]
            m = jnp.max(s, axis=-1, keepdims=True)
            p = jnp.exp2(s - m)
            pv = jnp.dot(p.astype(BF16), vp[pl.ds(r0, span), :], preferred_element_type=F32)
            for pr in range(R_B // 2):
                o = _normalise_pair(pv[2 * pr * BLK:(2 * pr + 1) * BLK], pv[(2 * pr + 1) * BLK:(2 * pr + 2) * BLK])
                o_ref[0, pl.ds(r0, BLK), pr * LANES:(pr + 1) * LANES] = o.astype(o_ref.dtype)
        return carry

    lax.fori_loop(0, seq // BLK // unroll, blocks, 0)


def _win_attention(qb, kw_dup, vw_dup, table, n_prev, unroll=8):
    b, s, _ = qb.shape
    gw = R_B * HEAD_DIM
    w = table.shape[-1]
    return pl.pallas_call(
        functools.partial(_win_kernel, seq=s, n_prev=n_prev, unroll=unroll),
        grid=(b, G_B),
        in_specs=[pl.BlockSpec((1, s, gw), lambda i, g: (i, 0, g)),
                  pl.BlockSpec((1, s, LANES), lambda i, g: (i, 0, g)),
                  pl.BlockSpec((1, s, LANES), lambda i, g: (i, 0, g)),
                  pl.BlockSpec((1, R_B * BLK, w), lambda i, g: (g, 0, 0))],
        out_specs=pl.BlockSpec((1, s, gw), lambda i, g: (i, 0, g)),
        out_shape=jax.ShapeDtypeStruct((b, s, D_B), BF16),
        scratch_shapes=[pltpu.VMEM((R_B, s, LANES), BF16)] + [pltpu.VMEM((s + n_prev * BLK, LANES), BF16)] * 2,
        compiler_params=_cparams(("parallel", "parallel")),
        name="nsa_window",
    )(qb, kw_dup, vw_dup, table)


def _compress_kernel(kin, vin, w1, w2, pe, kout, vout):
    n_chunk = kin.shape[2]
    half = w1.shape[1] // 2
    rows = lax.broadcasted_iota(jnp.int32, (n_chunk, 1), 0)
    for t, (xin, out) in enumerate(((kin, kout), (vin, vout))):
        x = xin[0, 0]
        a = jnp.dot(x, w1[t, :half], preferred_element_type=F32)
        b = jnp.dot(x, w1[t, half:], preferred_element_type=F32)
        c = jnp.dot(pe[t], w1[t], preferred_element_type=F32)[0:1]
        hid = a + pltpu.roll(b, n_chunk - 1, 0) + c
        act = hid * jax.nn.sigmoid(hid)
        o = jnp.dot(act.astype(BF16), w2[t], preferred_element_type=F32)
        out[0, 0] = jnp.where(rows < n_chunk - 1, o, 0.0).astype(BF16)


def _compress(kc_in, vc_in, w1, w2d, pe, layer):
    b, g, n_chunk, f = kc_in.shape
    xs = pl.BlockSpec((1, 1, n_chunk, f), lambda i, j: (i, j, 0, 0))
    os_ = pl.BlockSpec((1, 1, n_chunk, LANES), lambda i, j: (i, j, 0, 0))
    full = lambda a: pl.BlockSpec((None,) + a.shape[1:], lambda i, j: (layer,) + (0,) * (a.ndim - 1))
    od = jax.ShapeDtypeStruct((b, g, n_chunk, LANES), BF16)
    return pl.pallas_call(
        _compress_kernel,
        grid=(b, g),
        in_specs=[xs, xs, full(w1), full(w2d), full(pe)],
        out_specs=[os_, os_],
        out_shape=[od, od],
        compiler_params=_cparams(("parallel", "parallel")),
        name="nsa_compress",
    )(kc_in, vc_in, w1, w2d, pe)


def _cmp_kernel(q_ref, kc_ref, vc_ref, t_ref, ov_ref, o_ref, sb_ref, *, tq):
    i = pl.program_id(2)
    lo = _lo_mask()
    n_col, n_slc = kc_ref.shape[2], ov_ref.shape[0]

    def attend_and_rank(ncol, nsel):
        kcd = kc_ref[0, 0, :ncol, :]
        vcd = vc_ref[0, 0, :ncol, :]
        zero = jnp.zeros_like(vcd)
        vlo = jnp.where(lo, vcd, zero)
        vhi = jnp.where(lo, zero, vcd)
        psum = None
        for pr in range(R_B // 2):
            q = q_ref[0, :, pr * LANES:(pr + 1) * LANES]
            qz = jnp.zeros_like(q)
            acc = None
            for hh, (qm, vm) in enumerate(((jnp.where(lo, q, qz), vlo), (jnp.where(lo, qz, q), vhi))):
                s = _dot_nt(qm, kcd) + t_ref[2 * pr + hh, :, :ncol]
                m = jnp.max(s, axis=-1, keepdims=True)
                e = jnp.exp2(s - m)
                l = jnp.sum(e, axis=-1, keepdims=True)
                p = e * jnp.where(m > 0.5 * NEG, 1.0 / l, 0.0)
                psum = p if psum is None else psum + p
                pv = jnp.dot(p.astype(BF16), vm, preferred_element_type=F32)
                acc = pv if acc is None else acc + pv
            o_ref[0, :, pr * LANES:(pr + 1) * LANES] = acc.astype(o_ref.dtype)

        p_hi = psum.astype(BF16)
        p_lo = (psum - p_hi.astype(F32)).astype(BF16)
        ov = ov_ref[:nsel, :ncol]
        score = (_dot_nt(ov, p_hi) + _dot_nt(ov, p_lo))
        blk = lax.broadcasted_iota(jnp.int32, (nsel, 1), 0)
        pos = i * tq + lax.broadcasted_iota(jnp.int32, (1, tq), 1)
        cur = pos >> int(math.log2(SLC_BLOCK))
        forced = (blk == 0) | ((cur - blk >= 0) & (cur - blk < SLC_LOCAL))
        score = jnp.where(forced, 1e9, jnp.where(blk > cur, -1e9, score))
        sub = lax.broadcasted_iota(jnp.int32, (8, 1), 0)
        sel_rows = []
        for g8 in range(nsel // 8):
            sg = score[g8 * 8:(g8 + 1) * 8]
            rank = jnp.zeros(sg.shape, F32)
            for mp in range(nsel):
                row = score[mp:mp + 1]
                if mp < g8 * 8:
                    ahead = jnp.where(row >= sg, 1.0, 0.0)
                elif mp >= (g8 + 1) * 8:
                    ahead = jnp.where(row > sg, 1.0, 0.0)
                else:
                    ahead = jnp.where(sub > mp - g8 * 8, jnp.where(row >= sg, 1.0, 0.0), jnp.where(row > sg, 1.0, 0.0))
                rank = rank + ahead
            sel_rows.append(jnp.where(rank < float(min(SLC_TOPK, n_slc)), 0.0, NEG))
        if nsel < n_slc:
            sel_rows.append(jnp.full((n_slc - nsel, tq), NEG, F32))
        pad = LANES // n_slc
        selb = jnp.concatenate(sel_rows * pad, axis=0)
        sb_ref[0, 0] = selb.T.astype(BF16)

    n_tile = pl.num_programs(2)
    for k in range(4):
        ncol = max(LANES, -(-(n_col * (k + 1) // 4) // LANES) * LANES)
        in_quarter = (i * 4 >= k * n_tile) & (i * 4 < (k + 1) * n_tile)
        pl.when(in_quarter)(functools.partial(attend_and_rank, ncol, n_slc * (k + 1) // 4))


def _cmp_attention(qb, kcd, vcd, table, ov_t, tq=512):
    b, s, _ = qb.shape
    n_col = kcd.shape[2]
    gw = R_B * HEAD_DIM
    return pl.pallas_call(
        functools.partial(_cmp_kernel, tq=tq),
        grid=(b, G_B, s // tq),
        in_specs=[pl.BlockSpec((1, tq, gw), lambda i, g, j: (i, j, g)),
                  pl.BlockSpec((1, 1, n_col, LANES), lambda i, g, j: (i, g, 0, 0)),
                  pl.BlockSpec((1, 1, n_col, LANES), lambda i, g, j: (i, g, 0, 0)),
                  pl.BlockSpec((R_B, tq, n_col), lambda i, g, j: (g, j, 0)),
                  pl.BlockSpec(ov_t.shape, lambda i, g, j: (0, 0))],
        out_specs=[pl.BlockSpec((1, tq, gw), lambda i, g, j: (i, j, g)),
                   pl.BlockSpec((1, 1, tq, LANES), lambda i, g, j: (i, g, j, 0))],
        out_shape=[jax.ShapeDtypeStruct((b, s, D_B), BF16),
                   jax.ShapeDtypeStruct((b, G_B, s, LANES), BF16)],
        compiler_params=_cparams(("parallel", "parallel", "parallel")),
        name="nsa_cmp_topk",
    )(qb, kcd, vcd, table, ov_t)


def _sel_kernel(q_ref, sb_ref, k_ref, v_ref, t_ref, o_ref, kaug, vp, m_ref, acc_ref, *, nblk):
    lo = _lo_mask()
    s_len = nblk * BLK
    n_diag = t_ref.shape[1] - 1
    h0 = pl.program_id(1) * R_B
    rowblk = lax.broadcasted_iota(jnp.int32, (s_len, LANES), 0) >> int(math.log2(SLC_BLOCK))
    lane = lax.broadcasted_iota(jnp.int32, (s_len, LANES), 1)
    onehot = jnp.where(lane - HEAD_DIM == rowblk, 1.0, 0.0).astype(BF16)
    kaug[...] = jnp.where(lo, k_ref[0], onehot)
    vp[...] = jnp.where(lo, v_ref[0], jnp.ones((), BF16))
    m_ref[...] = jnp.full(m_ref.shape, NEG, F32)
    acc_ref[...] = jnp.zeros(acc_ref.shape, F32)
    kb = 4
    qb = 2
    sub = kb // qb
    kt_rows, qt_rows = kb * BLK, qb * BLK

    def ktile(t, carry):
        k0 = pl.multiple_of(t * kt_rows, kt_rows)
        kt = kaug[pl.ds(k0, kt_rows), :]
        vt = vp[pl.ds(k0, kt_rows), :]

        def qstep(mp, c):
            scores = []
            for u in range(sub):
                r0 = pl.multiple_of((sub * mp + u) * qt_rows, qt_rows)
                sbf = sb_ref[0, 0, pl.ds(r0, qt_rows), :].astype(F32)
                rows = []
                for pr in range(R_B // 2):
                    q = q_ref[0, pl.ds(r0, qt_rows), pr * LANES:(pr + 1) * LANES].astype(F32)
                    rows.append(jnp.where(lo, q, sbf))
                    rows.append(jnp.where(lo, pltpu.roll(q, HEAD_DIM, 1), sbf))
                qst = jnp.concatenate(rows, axis=0).astype(BF16)
                bias = jnp.concatenate([
                    jnp.concatenate([t_ref[h0 + h, jnp.clip(qb * (sub * mp + u) + a - kb * t - c + 1, 0, n_diag)]
                                     for c in range(kb)], axis=1)
                    for h in range(R_B) for a in range(qb)], axis=0)
                scores.append(_dot_nt(qst, kt) + bias)
            for u in range(sub):
                r0 = pl.multiple_of((sub * mp + u) * qt_rows, qt_rows)
                s = scores[u]
                ps, alphas = [], []
                for h in range(R_B):
                    for a in range(qb):
                        lo_r = h * qt_rows + a * BLK
                        st = pl.ds(r0 + a * BLK, BLK)
                        sh = s[lo_r:lo_r + BLK]
                        m_prev = m_ref[h, st, :]
                        m_new = jnp.maximum(m_prev, jnp.max(sh, axis=-1, keepdims=True))
                        alpha = jnp.exp2(m_prev - m_new)
                        p = jnp.exp2(sh - jnp.concatenate([m_new] * kb, axis=1))
                        m_ref[h, st, :] = m_new
                        ps.append(p.astype(BF16))
                        alphas.append(alpha)
                pv = jnp.dot(jnp.concatenate(ps, axis=0), vt, preferred_element_type=F32)
                for h in range(R_B):
                    for a in range(qb):
                        lo_r = h * qt_rows + a * BLK
                        st = pl.ds(r0 + a * BLK, BLK)
                        acc_ref[h, st, :] = alphas[qb * h + a] * acc_ref[h, st, :] + pv[lo_r:lo_r + BLK]
            return c

        lax.fori_loop(t, nblk // kb, qstep, 0)
        return carry

    lax.fori_loop(0, nblk // kb, ktile, 0)

    def finish(n, carry):
        st = pl.ds(pl.multiple_of(n * BLK, BLK), BLK)
        for pr in range(R_B // 2):
            o = _normalise_pair(acc_ref[2 * pr, st, :], acc_ref[2 * pr + 1, st, :])
            o_ref[0, st, pr * LANES:(pr + 1) * LANES] = o.astype(o_ref.dtype)
        return carry

    lax.fori_loop(0, nblk, finish, 0, unroll=8)


def _sel_attention(qb, selb, ks_dup, vs_dup, table):
    b, s, _ = qb.shape
    gw = R_B * HEAD_DIM
    return pl.pallas_call(
        functools.partial(_sel_kernel, nblk=s // BLK),
        grid=(b, G_B),
        in_specs=[pl.BlockSpec((1, s, gw), lambda i, g: (i, 0, g)),
                  pl.BlockSpec((1, 1, s, LANES), lambda i, g: (i, g, 0, 0)),
                  pl.BlockSpec((1, s, LANES), lambda i, g: (i, 0, g)),
                  pl.BlockSpec((1, s, LANES), lambda i, g: (i, 0, g)),
                  pl.BlockSpec(table.shape, lambda i, g: (0, 0, 0, 0), pipeline_mode=pl.Buffered(1))],
        out_specs=pl.BlockSpec((1, s, gw), lambda i, g: (i, 0, g)),
        out_shape=jax.ShapeDtypeStruct((b, s, D_B), BF16),
        scratch_shapes=[pltpu.VMEM((s, LANES), BF16)] * 2 + [pltpu.VMEM((R_B, s, LANES), F32)] * 2,
        compiler_params=_cparams(("parallel", "parallel")),
        name="nsa_selected",
    )(qb, selb, ks_dup, vs_dup, table)


def _combine_kernel(x_ref, oa, za, oc, osl, ow, gb, zb, wo, ex, gate, lng, lnb, out_ref, *, alpha):
    f = lambda r: r[0].astype(F32)
    z = f(za)
    mix_a = f(oa) * (z * jax.nn.sigmoid(z))
    g = jax.nn.sigmoid(gb[0])
    g_hi = g.astype(BF16)
    g_lo = (g - g_hi.astype(F32)).astype(BF16)
    gx = (jnp.dot(g_hi, ex[...], preferred_element_type=F32)
          + jnp.dot(g_lo, ex[...], preferred_element_type=F32))
    ob = gx[:, :D_B] * f(oc) + gx[:, D_B:2 * D_B] * f(osl) + gx[:, 2 * D_B:] * f(ow)
    z = f(zb)
    mix_b = ob * (z * jax.nn.sigmoid(z))
    y = (jnp.dot(mix_a.astype(BF16), wo[:D_A], preferred_element_type=F32)
         + jnp.dot(mix_b.astype(BF16), wo[D_A:], preferred_element_type=F32))
    r = alpha * x_ref[0] + (1.0 + gate[0]) * y
    mu = jnp.mean(r, axis=-1, keepdims=True)
    rc = r - mu
    var = jnp.mean(rc * rc, axis=-1, keepdims=True)
    out_ref[0] = rc * lax.rsqrt(var + LN_EPS) * lng[...] + lnb[...]


def _combine(x, o_a, za, o_cmp, o_slc, o_win, gb, zb, w_out, expand, gate, ln_g, ln_b, alpha, layer, tm=512):
    b, s, d = x.shape
    bs = lambda n: pl.BlockSpec((1, tm, n), lambda i, j: (i, j, 0))
    full = lambda a: pl.BlockSpec(a.shape, lambda i, j: (0,) * a.ndim)
    return pl.pallas_call(
        functools.partial(_combine_kernel, alpha=alpha),
        grid=(b, s // tm),
        in_specs=[bs(d)] + [bs(D_A)] * 2 + [bs(D_B)] * 3 + [bs(LANES), bs(D_B),
                  pl.BlockSpec((None,) + w_out.shape[1:], lambda i, j: (layer, 0, 0)), full(expand),
                  pl.BlockSpec((1, 1, d), lambda i, j: (i, 0, 0)), full(ln_g), full(ln_b)],
        out_specs=bs(d),
        out_shape=jax.ShapeDtypeStruct((b, s, d), F32),
        compiler_params=_cparams(("parallel", "parallel")),
        name="merge_out_proj_ln",
    )(x, o_a, za, o_cmp, o_slc, o_win, gb, zb, w_out, expand, gate, ln_g, ln_b)


def _gate_expand():
    e = np.zeros((LANES, 3 * D_B), np.float32)
    for i in range(3):
        for h in range(H_B):
            e[i * H_B + h, i * D_B + h * HEAD_DIM:i * D_B + (h + 1) * HEAD_DIM] = 1.0
    return jnp.asarray(e, BF16)


def _overlap_t(n_col, n_slc):
    cs = np.arange(n_col)[None, :] * CMP_STRIDE
    ss = np.arange(n_slc)[:, None] * SLC_BLOCK
    return jnp.asarray(((cs < ss + SLC_BLOCK) & (cs + CMP_LEN > ss)).astype(np.float32), BF16)


def _layer(x, shift, scl, gate, w_in_b, w_zb, w_out_b, pe_b, w1_b, w2d_b, ln_g, ln_b, tables, consts, alpha, layer):
    b, s, d = x.shape
    (qa, ka, va, za, qb, kc_in, vc_in, ks, vs, kw, vw, zb, gb) = _project(x, shift, scl, w_in_b, w_zb, layer)

    o_a = _dilated_attention(qa, ka, va, tables["dilated"])

    kcd, vcd = _compress(kc_in, vc_in, w1_b, w2d_b, pe_b, layer)
    o_cmp, selb = _cmp_attention(qb, kcd, vcd, tables["cmp"], consts["overlap_t"])
    o_slc = _sel_attention(qb, selb, ks, vs, tables["sel"])
    o_win = _win_attention(qb, kw, vw, tables["win"], n_prev=-(-(WIN - 1) // BLK))

    return _combine(x, o_a, za, o_cmp, o_slc, o_win, gb, zb, w_out_b, consts["expand"], gate,
                    ln_g, ln_b, alpha, layer)


def kernel(x, c, w_in, w_out, cmp_pe, cmp_w1, cmp_w2, w_ada, b_ada, ln_g, ln_b, rel_bias):
    b, s, d = x.shape
    depth = w_in.shape[0]
    alpha = (2 * depth) ** 0.25
    n_pair = D_A // LANES

    bb = rel_bias.astype(F32) * LOG2E
    def dilated_table(window, dil):
        t = _band_table(bb[:, :H_A], 1, window // dil, dil).reshape(n_pair, 2 * BLK, 2 * BLK)
        first = t.at[..., :BLK].set(NEG)
        return jnp.stack([t, first], axis=1)

    n_prev_win = -(-(WIN - 1) // BLK)
    tables = {
        "dilated": [dilated_table(w, dl) for w, dl in DILATED],
        "win": _band_table(bb[:, H_A:], n_prev_win, WIN - 1, 1).reshape(G_B, R_B * BLK, (n_prev_win + 1) * BLK),
        "cmp": _cmp_table(bb[:, H_A:], s),
        "sel": _diag_table(bb[:, H_A:], 13),
    }
    consts = {"expand": _gate_expand(), "overlap_t": _overlap_t(s // CMP_STRIDE, s // SLC_BLOCK)}

    mod = _ada_mod(c, w_ada, b_ada)
    w_in_b = w_in.astype(BF16)
    w_zb = w_in[..., _C_ZB:].astype(BF16)
    w_out_b = w_out.astype(BF16)
    w1_b = cmp_w1.astype(BF16)
    w2d_b = jnp.concatenate([cmp_w2, cmp_w2], axis=-1).astype(BF16)
    pe_b = jnp.broadcast_to(cmp_pe.reshape(depth, 2, 1, CMP_LEN * HEAD_DIM), (depth, 2, 8, CMP_LEN * HEAD_DIM)).astype(BF16)

    for l in range(depth):
        m = mod[l, :b]
        shift, scl, gate = (m[:, i * d:(i + 1) * d].reshape(b, 1, d) for i in range(3))
        x = _layer(x, shift, scl, gate, w_in_b, w_zb, w_out_b, pe_b, w1_b, w2d_b,
                   ln_g[l].reshape(1, d), ln_b[l].reshape(1, d), tables, consts, alpha, l)
    return x
```
